```python
import jax, jax.numpy as jnp
from jax import lax
import numpy as np

D_MODEL = 2048
BATCH = 16
SEQ = 256
DEPTH = 4
DEC_BATCH = 8
DEC_SEQ = 2048
PAST_LEN = 256

GRID_W = 64
N_MOD = 9
D_FF = 5632
N_BRANCH = 4
BRANCH_W = 512
MLSTM_HEADS = 4
MLSTM_DK = 128
MLSTM_DV = 128
ATT_HEADS = 8
ATT_KV_HEADS = 2
ATT_HEAD_DIM = 64
ATT_GROUPS = ATT_HEADS // ATT_KV_HEADS
WINDOW = 128
QBLOCK = 128
CONV_CH = 512
CONV_WIDTH = 3
RET_HEADS = 4
RET_DK = 128
RET_DV = 128
CHUNK = 128
ROPE_BASE = 10000.0
EPS = 1e-6
SPLIT_SIZES = (
    MLSTM_HEADS * MLSTM_DK, MLSTM_HEADS * MLSTM_DK, MLSTM_HEADS * MLSTM_DV, MLSTM_HEADS * MLSTM_DV, 4 * MLSTM_HEADS,
    ATT_HEADS * ATT_HEAD_DIM, ATT_KV_HEADS * ATT_HEAD_DIM, ATT_KV_HEADS * ATT_HEAD_DIM,
    CONV_CH, CONV_CH, CONV_CH,
    RET_HEADS * RET_DK, RET_HEADS * RET_DK, RET_HEADS * RET_DV, RET_HEADS * RET_DV,
)
PROJ_W = sum(SPLIT_SIZES)

kernel_name = 'hybrid_dit_mlstm_swa_conv_retention_step'


def rms_norm(x, g):
    xf = x.astype(jnp.float32)
    y = xf * lax.rsqrt(jnp.mean(xf * xf, axis=-1, keepdims=True) + EPS)
    return (y * g.astype(jnp.float32)).astype(x.dtype)


def head_layer_norm(x, g):
    B, T, H, d = x.shape
    xf = x.astype(jnp.float32)
    mu = jnp.mean(xf, axis=-1, keepdims=True)
    var = jnp.mean(jnp.square(xf - mu), axis=-1, keepdims=True)
    y = ((xf - mu) * lax.rsqrt(var + EPS)).reshape(B, T, H * d)
    return (y * g.astype(jnp.float32)).astype(x.dtype)


def modulation(cvec, w_mod, b_mod):
    m = jax.nn.silu(cvec) @ w_mod + b_mod
    return m.reshape(cvec.shape[0], N_MOD, D_MODEL)


def swiglu(h, w_in, w_out):
    gate, up = jnp.split(h @ w_in, 2, axis=-1)
    return (jax.nn.silu(gate) * up) @ w_out


def rope_tables(T, d):
    rows = T // GRID_W
    row = jnp.repeat(jnp.arange(rows), GRID_W).astype(jnp.float32)
    col = jnp.tile(jnp.arange(GRID_W), rows).astype(jnp.float32)
    quarter = d // 4
    inv = jnp.power(ROPE_BASE, -jnp.arange(quarter, dtype=jnp.float32) / quarter)
    ar = row[:, None] * inv
    ac = col[:, None] * inv
    ang = jnp.concatenate([ar, ar, ac, ac], axis=-1)
    return jnp.cos(ang)[:, None, :], jnp.sin(ang)[:, None, :]


def apply_rope(x, cs):
    cos, sin = cs
    x1r, x2r, x1c, x2c = jnp.split(x, 4, axis=-1)
    rot = jnp.concatenate([-x2r, x1r, -x2c, x1c], axis=-1)
    return (x * cos + rot * sin).astype(x.dtype)


def flip_t(x):
    return jnp.flip(x, axis=1)


def sink_softmax(s, sink):
    m = jnp.maximum(jnp.max(s, axis=-1, keepdims=True), sink)
    e = jnp.exp(s - m)
    return e / (jnp.sum(e, axis=-1, keepdims=True) + jnp.exp(sink - m))


def attn_context(q, k, v, sink):
    B, T, H, hd = q.shape
    nq = T // QBLOCK
    qb = q.reshape(B, nq, QBLOCK, ATT_KV_HEADS, ATT_GROUPS, hd).transpose(1, 0, 2, 3, 4, 5)
    sk = sink.astype(jnp.float32).reshape(ATT_KV_HEADS, ATT_GROUPS, 1, 1)
    scale = hd ** -0.5

    def block(qi):
        s = jnp.einsum('bqkgd,bskd->bkgqs', qi, k).astype(jnp.float32) * scale
        p = sink_softmax(s, sk)
        return jnp.einsum('bkgqs,bskd->bqkgd', p.astype(v.dtype), v)

    o = lax.map(block, qb)
    return o.transpose(1, 0, 2, 3, 4, 5).reshape(B, T, H * hd)


def attn_latent(q, k, v, kc, vc, sink):
    B, T, H, hd = q.shape
    nb = T // QBLOCK
    qb = q.reshape(B, nb, QBLOCK, ATT_KV_HEADS, ATT_GROUPS, hd)

    def band(x):
        xp = jnp.pad(x, ((0, 0), (QBLOCK, QBLOCK), (0, 0), (0, 0))).reshape(B, nb + 2, QBLOCK, ATT_KV_HEADS, hd)
        return jnp.concatenate([xp[:, :-2], xp[:, 1:-1], xp[:, 2:]], axis=2)

    kn, vn = band(k), band(v)
    scale = hd ** -0.5
    s_loc = jnp.einsum('bnqkgd,bnskd->bnkgqs', qb, kn).astype(jnp.float32) * scale
    s_ctx = jnp.einsum('bnqkgd,bckd->bnkgqc', qb, kc).astype(jnp.float32) * scale
    blk = jnp.arange(nb)[:, None, None]
    qpos = blk * QBLOCK + jnp.arange(QBLOCK)[None, :, None]
    kpos = (blk - 1) * QBLOCK + jnp.arange(3 * QBLOCK)[None, None, :]
    valid = (jnp.abs(qpos - kpos) <= WINDOW) & (kpos >= 0) & (kpos < T)
    s_loc = jnp.where(valid[None, :, None, None], s_loc, -jnp.inf)
    s = jnp.concatenate([s_loc, s_ctx], axis=-1)
    p = sink_softmax(s, sink.astype(jnp.float32).reshape(1, 1, ATT_KV_HEADS, ATT_GROUPS, 1, 1)).astype(v.dtype)
    o = (jnp.einsum('bnkgqs,bnskd->bnqkgd', p[..., :3 * QBLOCK], vn)
         + jnp.einsum('bnkgqc,bckd->bnqkgd', p[..., 3 * QBLOCK:], vc))
    return o.reshape(B, T, H * hd)


def mlstm_chunked(q, k, v, ig, lf, S0, n0, m0):
    B, T, H, dk = q.shape
    nc = T // CHUNK
    qc = q.reshape(B, nc, CHUNK, H, dk)
    kc = k.reshape(B, nc, CHUNK, H, dk)
    vc = v.reshape(B, nc, CHUNK, H, v.shape[-1])
    igc = ig.reshape(B, nc, CHUNK, H).transpose(0, 1, 3, 2)
    F = jnp.cumsum(lf.reshape(B, nc, CHUNK, H).transpose(0, 1, 3, 2), axis=-1)
    F_last = F[..., -1]
    causal = jnp.tril(jnp.ones((CHUNK, CHUNK), dtype=bool))
    lw = jnp.where(causal, F[..., :, None] - F[..., None, :] + igc[..., None, :], -jnp.inf)
    lw_end = F_last[..., None] - F + igc
    a_end = jnp.max(lw_end, axis=-1)
    kw = kc * jnp.exp(lw_end - a_end[..., None]).transpose(0, 1, 3, 2)[..., None]
    U = jnp.einsum('bnshk,bnshv->bnhkv', kw, vc)
    u = jnp.sum(kw, axis=2)

    def step(carry, xs):
        S, n, m = carry
        fl, a, Uc, uc = xs
        m_new = jnp.maximum(fl + m, a)
        d_old = jnp.exp(fl + m - m_new)
        d_new = jnp.exp(a - m_new)
        S_new = d_old[..., None, None] * S + d_new[..., None, None] * Uc
        n_new = d_old[..., None] * n + d_new[..., None] * uc
        return (S_new, n_new, m_new), (S, n, m)

    xs = (jnp.moveaxis(F_last, 1, 0), jnp.moveaxis(a_end, 1, 0), jnp.moveaxis(U, 1, 0), jnp.moveaxis(u, 1, 0))
    fin, (S_prev, n_prev, m_prev) = lax.scan(step, (S0, n0, m0), xs)
    S_prev = jnp.moveaxis(S_prev, 0, 1)
    n_prev = jnp.moveaxis(n_prev, 0, 1)
    m_prev = jnp.moveaxis(m_prev, 0, 1)
    lw_inter = F + m_prev[..., None]
    m_tok = jnp.maximum(jnp.max(lw, axis=-1), lw_inter)
    A = jnp.einsum('bnqhk,bnshk->bnhqs', qc, kc) * jnp.exp(lw - m_tok[..., None])
    w_inter = jnp.exp(lw_inter - m_tok).transpose(0, 1, 3, 2)[..., None]
    num = jnp.einsum('bnhqs,bnshv->bnqhv', A, vc) + w_inter * jnp.einsum('bnqhk,bnhkv->bnqhv', qc, S_prev)
    den = (jnp.sum(A, axis=-1).transpose(0, 1, 3, 2)[..., None]
           + w_inter * jnp.einsum('bnqhk,bnhk->bnqh', qc, n_prev)[..., None])
    floor = jnp.exp(-m_tok).transpose(0, 1, 3, 2)[..., None]
    h = num / jnp.maximum(jnp.abs(den), floor)
    return h.reshape(B, T, H, -1), fin


def retention_chunked(q, k, v, log_decay, S0):
    B, T, H, dk = q.shape
    nc = T // CHUNK
    qc = q.reshape(B, nc, CHUNK, H, dk)
    kc = k.reshape(B, nc, CHUNK, H, dk)
    vc = v.reshape(B, nc, CHUNK, H, v.shape[-1])
    idx = jnp.arange(CHUNK, dtype=jnp.float32)
    diff = idx[:, None] - idx[None, :]
    Dm = jnp.where(diff >= 0, jnp.exp(jnp.maximum(diff, 0.0)[None] * log_decay[:, None, None]), 0.0)
    inner = jnp.einsum('bnhqs,bnshv->bnqhv', jnp.einsum('bnqhk,bnshk->bnhqs', qc, kc) * Dm, vc)
    w_end = jnp.exp((CHUNK - 1 - idx)[:, None] * log_decay[None, :])
    U = jnp.einsum('bnshk,bnshv->bnhkv', kc * w_end[:, :, None], vc)
    chunk_decay = jnp.exp(CHUNK * log_decay)[:, None, None]

    def step(S, Uc):
        return chunk_decay * S + Uc, S

    S_fin, S_prev = lax.scan(step, S0, jnp.moveaxis(U, 1, 0))
    S_prev = jnp.moveaxis(S_prev, 0, 1)
    w_cross = jnp.exp((idx + 1.0)[:, None] * log_decay[None, :])
    cross = jnp.einsum('bnqhk,bnhkv->bnqhv', qc, S_prev) * w_cross[:, :, None]
    return (inner + cross).reshape(B, T, H, -1), S_fin


def token_mix(h, lp, rope_att, rope_ret, cache):
    B, T, _ = h.shape
    f32 = jnp.float32
    split_at = np.cumsum(SPLIT_SIZES)[:-1].tolist()
    (a_q, a_k, a_v, a_o, a_g, b_q, b_k, b_v, c_b, c_c, c_x,
     d_q, d_k, d_v, d_g) = jnp.split(h @ lp['w_in'], split_at, axis=-1)

    def heads(t, n):
        return t.reshape(B, T, n, -1)

    qa = heads(a_q, MLSTM_HEADS).astype(f32)
    ka = heads(a_k, MLSTM_HEADS).astype(f32) * MLSTM_DK ** -0.5
    va = heads(a_v, MLSTM_HEADS).astype(f32)
    gates = a_g.reshape(B, T, 4, MLSTM_HEADS).astype(f32) + lp['mlstm_gate_b'].astype(f32)
    ig_f, ig_b = gates[:, :, 0], gates[:, :, 2]
    lf_f, lf_b = jax.nn.log_sigmoid(gates[:, :, 1]), jax.nn.log_sigmoid(gates[:, :, 3])
    if cache is None:
        zs = (jnp.zeros((B, MLSTM_HEADS, MLSTM_DK, MLSTM_DV), f32),
              jnp.zeros((B, MLSTM_HEADS, MLSTM_DK), f32),
              jnp.zeros((B, MLSTM_HEADS), f32))
        st_f = zs
        st_b = zs
    else:
        C0 = cache['C'].astype(f32)
        n0 = cache['n'].astype(f32)
        m0 = cache['m'].astype(f32)
        st_f = (C0[:, 0], n0[:, 0], m0[:, 0])
        st_b = (C0[:, 1], n0[:, 1], m0[:, 1])
    h_f, fin_f = mlstm_chunked(qa, ka, va, ig_f, lf_f, *st_f)
    h_b, fin_b = mlstm_chunked(flip_t(qa), flip_t(ka), flip_t(va), flip_t(ig_b), flip_t(lf_b), *st_b)
    out_a = jax.nn.sigmoid(a_o) * head_layer_norm((h_f + flip_t(h_b)).astype(h.dtype), lp['mlstm_norm_g'])

    qb = heads(b_q, ATT_HEADS)
    kb = heads(b_k, ATT_KV_HEADS)
    vb = heads(b_v, ATT_KV_HEADS)
    if cache is None:
        out_b = attn_context(qb, kb, vb, lp['attn_sink'])
    else:
        out_b = attn_latent(apply_rope(qb, rope_att), apply_rope(kb, rope_att), vb,
                            cache['k'].astype(h.dtype), cache['v'].astype(h.dtype), lp['attn_sink'])

    u = jnp.pad(c_c * c_x, ((0, 0), (1, 1), (0, 0)))
    w = lp['conv_w']
    out_c = c_b * (w[0] * u[:, :-2] + w[1] * u[:, 1:-1] + w[2] * u[:, 2:])

    qd = heads(d_q, RET_HEADS)
    kd = heads(d_k, RET_HEADS) * RET_DK ** -0.5
    vd = heads(d_v, RET_HEADS)
    if cache is None:
        S0 = jnp.zeros((B, 2, RET_HEADS, RET_DK, RET_DV), f32)
    else:
        qd = apply_rope(qd, rope_ret)
        kd = apply_rope(kd, rope_ret)
        S0 = cache['S'].astype(f32)
    qd, kd, vd = qd.astype(f32), kd.astype(f32), vd.astype(f32)
    lg = lp['ret_log_decay'].astype(f32)
    o_f, S_f = retention_chunked(qd, kd, vd, lg[0], S0[:, 0])
    o_b, S_b = retention_chunked(flip_t(qd), flip_t(kd), flip_t(vd), lg[1], S0[:, 1])
    out_d = jax.nn.silu(d_g) * head_layer_norm((o_f + flip_t(o_b)).astype(h.dtype), lp['ret_norm_g'])

    branches = (out_a, out_b, out_c, out_d)
    y = jax.nn.sigmoid(h @ lp['w_gate'][0]) * (out_a @ lp['w_branch'][0])
    for i in range(1, N_BRANCH):
        y = y + jax.nn.sigmoid(h @ lp['w_gate'][i]) * (branches[i] @ lp['w_branch'][i])
    y = y @ lp['w_out']
    if cache is not None:
        return y, None
    new_ctx = (kb, vb,
               jnp.stack([fin_f[0], fin_b[0]], axis=1),
               jnp.stack([fin_f[1], fin_b[1]], axis=1),
               jnp.stack([fin_f[2], fin_b[2]], axis=1),
               jnp.stack([S_f, S_b], axis=1))
    return y, new_ctx


def trunk_layer(x, mod, lp, rope_att, rope_ret, cache):
    g = lp['norm_g']
    sh, sc, gt = mod[:, None, 0], mod[:, None, 1], mod[:, None, 2]
    h = rms_norm(x, g[0]) * (1.0 + sc) + sh
    x = x + 0.5 * gt * rms_norm(swiglu(h, lp['ffn1_w_in'], lp['ffn1_w_out']), g[1])
    sh, sc, gt = mod[:, None, 3], mod[:, None, 4], mod[:, None, 5]
    h = rms_norm(x, g[2]) * (1.0 + sc) + sh
    y, new_ctx = token_mix(h, lp, rope_att, rope_ret, cache)
    x = x + gt * rms_norm(y, g[3])
    sh, sc, gt = mod[:, None, 6], mod[:, None, 7], mod[:, None, 8]
    h = rms_norm(x, g[4]) * (1.0 + sc) + sh
    x = x + 0.5 * gt * rms_norm(swiglu(h, lp['ffn2_w_in'], lp['ffn2_w_out']), g[5])
    return x, new_ctx


def setup_inputs(seed: int = 0) -> dict:
    key = jax.random.key(seed)
    ks = jax.random.split(key, 32)
    f32 = jnp.float32
    D = D_MODEL

    def nrm(k, shape, s):
        return s * jax.random.normal(k, shape, f32)

    base_decay = jnp.log(1.0 - jnp.power(2.0, -5.0 - jnp.arange(RET_HEADS, dtype=f32)))
    gate_base = jnp.array([0.0, 3.0, 0.0, 3.0], f32)[None, :, None]
    return {
        'x_prompt': nrm(ks[0], (BATCH, SEQ, D), 1.0),
        'x_sample': nrm(ks[1], (DEC_BATCH, DEC_SEQ, D), 1.0),
        'cache_attn_k': nrm(ks[2], (DEC_BATCH, DEPTH, PAST_LEN, ATT_KV_HEADS, ATT_HEAD_DIM), 1.0),
        'cache_attn_v': nrm(ks[3], (DEC_BATCH, DEPTH, PAST_LEN, ATT_KV_HEADS, ATT_HEAD_DIM), 1.0),
        'state_mlstm_C': nrm(ks[4], (DEC_BATCH, DEPTH, 2, MLSTM_HEADS, MLSTM_DK, MLSTM_DV), 0.05),
        'state_mlstm_n': nrm(ks[5], (DEC_BATCH, DEPTH, 2, MLSTM_HEADS, MLSTM_DK), 0.05),
        'state_mlstm_m': nrm(ks[6], (DEC_BATCH, DEPTH, 2, MLSTM_HEADS), 0.5),
        'state_ret_S': nrm(ks[7], (DEC_BATCH, DEPTH, 2, RET_HEADS, RET_DK, RET_DV), 0.05),
        'c': nrm(ks[8], (DEC_BATCH, D), 1.0),
        'c_ctx': nrm(ks[9], (D,), 1.0),
        'w_mod': nrm(ks[10], (DEPTH, D, N_MOD * D), 0.5 * D ** -0.5),
        'b_mod': nrm(ks[11], (DEPTH, N_MOD * D), 0.02),
        'norm_g': 1.0 + nrm(ks[12], (DEPTH, 6, D), 0.01),
        'ffn1_w_in': nrm(ks[13], (DEPTH, D, 2 * D_FF), D ** -0.5),
        'ffn1_w_out': nrm(ks[14], (DEPTH, D_FF, D), D_FF ** -0.5),
        'ffn2_w_in': nrm(ks[15], (DEPTH, D, 2 * D_FF), D ** -0.5),
        'ffn2_w_out': nrm(ks[16], (DEPTH, D_FF, D), D_FF ** -0.5),
        'w_in': nrm(ks[17], (DEPTH, D, PROJ_W), D ** -0.5),
        'mlstm_gate_b': gate_base + nrm(ks[18], (DEPTH, 4, MLSTM_HEADS), 0.1),
        'mlstm_norm_g': 1.0 + nrm(ks[19], (DEPTH, MLSTM_HEADS * MLSTM_DV), 0.01),
        'attn_sink': nrm(ks[20], (DEPTH, ATT_HEADS), 0.5),
        'conv_w': nrm(ks[21], (DEPTH, CONV_WIDTH, CONV_CH), CONV_WIDTH ** -0.5),
        'ret_log_decay': base_decay * (1.0 + nrm(ks[22], (DEPTH, 2, RET_HEADS), 0.05)),
        'ret_norm_g': 1.0 + nrm(ks[23], (DEPTH, RET_HEADS * RET_DV), 0.01),
        'w_branch': nrm(ks[24], (DEPTH, N_BRANCH, BRANCH_W, D), BRANCH_W ** -0.5),
        'w_gate': nrm(ks[25], (DEPTH, N_BRANCH, D, D), D ** -0.5),
        'w_out': nrm(ks[26], (DEPTH, D, D), D ** -0.5),
    }


def reference(x_prompt, x_sample, cache_attn_k, cache_attn_v, state_mlstm_C, state_mlstm_n, state_mlstm_m,
              state_ret_S, c, c_ctx, w_mod, b_mod, norm_g, ffn1_w_in, ffn1_w_out, ffn2_w_in, ffn2_w_out,
              w_in, mlstm_gate_b, mlstm_norm_g, attn_sink, conv_w, ret_log_decay, ret_norm_g,
              w_branch, w_gate, w_out):
    T = x_sample.shape[1]
    rope_att = rope_tables(T, ATT_HEAD_DIM)
    rope_ret = rope_tables(T, RET_DK)
    xp = x_prompt
    xs = x_sample
    ks_, vs_, Cs_, ns_, ms_, Ss_ = [], [], [], [], [], []
    for l in range(DEPTH):
        lp = {
            'norm_g': norm_g[l],
            'ffn1_w_in': ffn1_w_in[l], 'ffn1_w_out': ffn1_w_out[l],
            'ffn2_w_in': ffn2_w_in[l], 'ffn2_w_out': ffn2_w_out[l],
            'w_in': w_in[l], 'mlstm_gate_b': mlstm_gate_b[l], 'mlstm_norm_g': mlstm_norm_g[l],
            'attn_sink': attn_sink[l], 'conv_w': conv_w[l],
            'ret_log_decay': ret_log_decay[l], 'ret_norm_g': ret_norm_g[l],
            'w_branch': w_branch[l], 'w_gate': w_gate[l], 'w_out': w_out[l],
        }
        xp, ctx = trunk_layer(xp, modulation(c_ctx[None], w_mod[l], b_mod[l]), lp, None, None, None)
        ks_.append(ctx[0])
        vs_.append(ctx[1])
        Cs_.append(ctx[2])
        ns_.append(ctx[3])
        ms_.append(ctx[4])
        Ss_.append(ctx[5])
        cache = {
            'k': cache_attn_k[:, l], 'v': cache_attn_v[:, l],
            'C': state_mlstm_C[:, l], 'n': state_mlstm_n[:, l], 'm': state_mlstm_m[:, l],
            'S': state_ret_S[:, l],
        }
        xs, _ = trunk_layer(xs, modulation(c, w_mod[l], b_mod[l]), lp, rope_att, rope_ret, cache)
    new_attn_k = jnp.stack(ks_, axis=1)
    new_attn_v = jnp.stack(vs_, axis=1)
    new_mlstm_C = jnp.stack(Cs_, axis=1)
    new_mlstm_n = jnp.stack(ns_, axis=1)
    new_mlstm_m = jnp.stack(ms_, axis=1)
    new_ret_S = jnp.stack(Ss_, axis=1)
    return (xp, xs, new_attn_k, new_attn_v, new_mlstm_C, new_mlstm_n, new_mlstm_m, new_ret_S)
```

```python
import functools

import numpy as np
import jax
import jax.numpy as jnp
from jax import lax
from jax.experimental import pallas as pl
from jax.experimental.pallas import tpu as pltpu

F32 = jnp.float32
BF16 = jnp.bfloat16

N_MOD = 9
N_BRANCH = 4
BRANCH_W = 512
MIX_HEADS = 4
MIX_DIM = 128
ATT_HEADS = 8
ATT_KV_HEADS = 2
ATT_HEAD_DIM = 64
ATT_GROUPS = ATT_HEADS // ATT_KV_HEADS
QBLOCK = 128
CHUNK = 128
GRID_W = 64
ROPE_BASE = 10000.0
EPS = 1e-6
NEG = -1e30

LANE = 128
COL_A_Q, COL_A_K, COL_A_V, COL_A_O = 0, 4, 8, 12
COL_C_B, COL_C_C, COL_C_X = 16, 20, 24
COL_D_Q, COL_D_K, COL_D_V, COL_D_G = 28, 32, 36, 40
COL_B_Q, COL_B_K, COL_B_V = 44, 48, 49
COL_GATES = 50
PROJ_BLOCKS = 54
PROJ_P = PROJ_BLOCKS * LANE

VMEM_LIMIT = 52 * 1024 * 1024


def _params(sem, vmem=VMEM_LIMIT):
    return pltpu.CompilerParams(dimension_semantics=sem, vmem_limit_bytes=vmem)


def _pick_tile(n, pref):
    if n <= pref:
        return n
    t = (pref // LANE) * LANE
    while t >= LANE:
        if n % t == 0:
            return t
        t -= LANE
    raise ValueError(f"no lane-aligned tile divides {n}")


def _dot(a, b):
    return jnp.dot(a, b, preferred_element_type=F32)


def _dot_nt(a, b):
    return lax.dot_general(a, b, (((1,), (1,)), ((), ())), preferred_element_type=F32)


def _dot_tn(a, b):
    return lax.dot_general(a, b, (((0,), (0,)), ((), ())), preferred_element_type=F32)


def _sigmoid(x):
    return 1.0 / (1.0 + jnp.exp(-x))


def _rms(xf, g):
    ms = jnp.mean(xf * xf, axis=-1, keepdims=True)
    return xf * lax.rsqrt(ms + EPS) * g


def _modulated(x_ref, mod_ref, g_ref):
    return _rms(x_ref[...], g_ref[0:1, :]) * (1.0 + mod_ref[1:2, :]) + mod_ref[0:1, :]


def _head_norm(x, g):
    mu = jnp.mean(x, axis=-1, keepdims=True)
    xc = x - mu
    var = jnp.mean(xc * xc, axis=-1, keepdims=True)
    return xc * lax.rsqrt(var + EPS) * g


def _mod_kernel(c_ref, w_ref, b_ref, o_ref):
    c = c_ref[...]
    s = (c * _sigmoid(c)).astype(BF16)
    o_ref[...] = _dot(s, w_ref[...].astype(BF16)) + b_ref[...]


def _modulation(cpad, w_mod, b_mod):
    depth, d, n = w_mod.shape
    nbp = cpad.shape[0]
    tn = _pick_tile(n, 1024)
    return pl.pallas_call(
        _mod_kernel,
        grid=(depth, n // tn),
        in_specs=[
            pl.BlockSpec((nbp, d), lambda l, j: (0, 0)),
            pl.BlockSpec((None, d, tn), lambda l, j: (l, 0, j)),
            pl.BlockSpec((None, 1, tn), lambda l, j: (l, 0, j)),
        ],
        out_specs=pl.BlockSpec((None, nbp, tn), lambda l, j: (l, 0, j)),
        out_shape=jax.ShapeDtypeStruct((depth, nbp, n), F32),
        compiler_params=_params(("parallel", "parallel")),
        name="modulation",
    )(cpad, w_mod, b_mod.reshape(depth, 1, n))


class _Rows:
    def __init__(self, bs, ts, bp, tp, tm):
        self.bs, self.ts, self.bp, self.tp = bs, ts, bp, tp
        self.ms, self.mp = bs * ts, bp * tp
        self.m = self.ms + self.mp
        assert ts % tm == 0 and self.mp % tm == 0 and self.ms % tp == 0
        self.tm = tm
        self.lat_tiles = self.ms // tm
        self.tiles_per_seq = ts // tm

    def mod_index(self, i):
        return jnp.where(i < self.lat_tiles, 1 + i // self.tiles_per_seq, 0)


def _ffn_kernel(x_ref, mod_ref, g_ref, wg_ref, wu_ref, wo_ref, o_ref, h_scr, acc_scr):
    f = pl.program_id(1)

    @pl.when(f == 0)
    def _():
        h_scr[...] = _modulated(x_ref, mod_ref, g_ref).astype(BF16)
        acc_scr[...] = jnp.zeros_like(acc_scr)

    h = h_scr[...]
    gate = _dot(h, wg_ref[...])
    up = _dot(h, wu_ref[...])
    a = (gate * _sigmoid(gate)) * up
    acc_scr[...] += _dot(a.astype(BF16), wo_ref[...])

    @pl.when(f == pl.num_programs(1) - 1)
    def _():
        y = _rms(acc_scr[...], g_ref[1:2, :])
        o_ref[...] = x_ref[...] + (0.5 * mod_ref[2:3, :]) * y


def _ffn(rows, x, mod_s, g2, w_in, w_out):
    m, d = x.shape
    dff = w_out.shape[0]
    tm = rows.tm
    tf = _pick_tile(dff, 512)
    nf = dff // tf
    return pl.pallas_call(
        _ffn_kernel,
        grid=(m // tm, nf),
        in_specs=[
            pl.BlockSpec((tm, d), lambda i, f: (i, 0)),
            pl.BlockSpec((None, 3, d), lambda i, f: (rows.mod_index(i), 0, 0)),
            pl.BlockSpec((2, d), lambda i, f: (0, 0)),
            pl.BlockSpec((d, tf), lambda i, f: (0, f)),
            pl.BlockSpec((d, tf), lambda i, f: (0, f + nf)),
            pl.BlockSpec((tf, d), lambda i, f: (f, 0)),
        ],
        out_specs=pl.BlockSpec((tm, d), lambda i, f: (i, 0)),
        out_shape=jax.ShapeDtypeStruct((m, d), F32),
        scratch_shapes=[pltpu.VMEM((tm, d), BF16), pltpu.VMEM((tm, d), F32)],
        compiler_params=_params(("parallel", "arbitrary")),
        name="ffn",
    )(x, mod_s, g2, w_in, w_in, w_out)


def _inproj_kernel(x_ref, mod_ref, g_ref, w_ref, p_ref, h_scr):
    @pl.when(pl.program_id(1) == 0)
    def _():
        h_scr[...] = _modulated(x_ref, mod_ref, g_ref).astype(BF16)

    p_ref[...] = _dot(h_scr[...], w_ref[...])


def _inproj(rows, x, mod_s, g2, w_p):
    m, d = x.shape
    n = w_p.shape[1]
    tm = rows.tm
    tn = _pick_tile(n, 768)
    return pl.pallas_call(
        _inproj_kernel,
        grid=(m // tm, n // tn),
        in_specs=[
            pl.BlockSpec((tm, d), lambda i, j: (i, 0)),
            pl.BlockSpec((None, 3, d), lambda i, j: (rows.mod_index(i), 0, 0)),
            pl.BlockSpec((2, d), lambda i, j: (0, 0)),
            pl.BlockSpec((d, tn), lambda i, j: (0, j)),
        ],
        out_specs=pl.BlockSpec((tm, tn), lambda i, j: (i, j)),
        out_shape=jax.ShapeDtypeStruct((m, n), F32),
        scratch_shapes=[pltpu.VMEM((tm, d), BF16)],
        compiler_params=_params(("parallel", "arbitrary")),
        name="inproj",
    )(x, mod_s, g2, w_p)


def _merge_kernel(lat_tiles, x_ref, mod_ref, g_ref, *refs):
    lat_refs, ctx_refs = refs[0:N_BRANCH], refs[N_BRANCH:2 * N_BRANCH]
    wg_ref, wb_ref, y_ref, h_scr = refs[2 * N_BRANCH:]

    @pl.when(pl.program_id(1) == 0)
    def _():
        h_scr[...] = _modulated(x_ref, mod_ref, g_ref).astype(BF16)

    h = h_scr[...]
    is_lat = pl.program_id(0) < lat_tiles
    y = None
    for i in range(N_BRANCH):
        o = jnp.where(is_lat, lat_refs[i][...], ctx_refs[i][...])
        term = _sigmoid(_dot(h, wg_ref[i])) * _dot(o, wb_ref[i])
        y = term if y is None else y + term
    y_ref[...] = y.astype(BF16)


def _merge(rows, x, mod_s, g2, lat, ctx, w_gate, w_branch):
    m, d = x.shape
    tm = rows.tm
    tn = _pick_tile(d, 256)
    bw = w_branch.shape[1]
    last_lat = rows.lat_tiles - 1
    lat_spec = pl.BlockSpec((tm, bw), lambda i, j: (jnp.minimum(i, last_lat), 0))
    ctx_spec = pl.BlockSpec((tm, bw), lambda i, j: (jnp.maximum(i - rows.lat_tiles, 0), 0))
    return pl.pallas_call(
        functools.partial(_merge_kernel, rows.lat_tiles),
        grid=(m // tm, d // tn),
        in_specs=[
            pl.BlockSpec((tm, d), lambda i, j: (i, 0)),
            pl.BlockSpec((None, 3, d), lambda i, j: (rows.mod_index(i), 0, 0)),
            pl.BlockSpec((2, d), lambda i, j: (0, 0)),
            *([lat_spec] * N_BRANCH), *([ctx_spec] * N_BRANCH),
            pl.BlockSpec((N_BRANCH, d, tn), lambda i, j: (0, 0, j)),
            pl.BlockSpec((N_BRANCH, bw, tn), lambda i, j: (0, 0, j)),
        ],
        out_specs=pl.BlockSpec((tm, tn), lambda i, j: (i, j)),
        out_shape=jax.ShapeDtypeStruct((m, d), BF16),
        scratch_shapes=[pltpu.VMEM((tm, d), BF16)],
        compiler_params=_params(("parallel", "arbitrary")),
        name="merge",
    )(x, mod_s, g2, *lat, *ctx, w_gate, w_branch)


def _outproj_kernel(nt, tn, x_ref, mod_ref, g_ref, y_ref, w_ref, o_ref, z_scr):
    j = pl.program_id(1)
    z_scr[j] = _dot(y_ref[...], w_ref[...])

    @pl.when(j == nt - 1)
    def _():
        ss = None
        for k in range(nt):
            z = z_scr[k]
            s = jnp.sum(z * z, axis=-1, keepdims=True)
            ss = s if ss is None else ss + s
        inv = lax.rsqrt(ss * (1.0 / (nt * tn)) + EPS)
        for k in range(nt):
            sl = slice(k * tn, (k + 1) * tn)
            y = z_scr[k] * inv * g_ref[1:2, sl]
            o_ref[:, sl] = x_ref[:, sl] + mod_ref[2:3, sl] * y


def _outproj(rows, x, mod_s, g2, y, w_out):
    m, d = x.shape
    tm = rows.tm
    tn = _pick_tile(d, 512)
    nt = d // tn
    return pl.pallas_call(
        functools.partial(_outproj_kernel, nt, tn),
        grid=(m // tm, nt),
        in_specs=[
            pl.BlockSpec((tm, d), lambda i, j: (i, 0)),
            pl.BlockSpec((None, 3, d), lambda i, j: (rows.mod_index(i), 0, 0)),
            pl.BlockSpec((2, d), lambda i, j: (0, 0)),
            pl.BlockSpec((tm, d), lambda i, j: (i, 0)),
            pl.BlockSpec((d, tn), lambda i, j: (0, j)),
        ],
        out_specs=pl.BlockSpec((tm, d), lambda i, j: (i, 0)),
        out_shape=jax.ShapeDtypeStruct((m, d), F32),
        scratch_shapes=[pltpu.VMEM((nt, tm, tn), F32)],
        compiler_params=_params(("parallel", "arbitrary")),
        name="outproj",
    )(x, mod_s, g2, y, w_out)


def _rope_tables(t, d):
    rows = t // GRID_W
    row = jnp.repeat(jnp.arange(rows), GRID_W).astype(F32)
    col = jnp.tile(jnp.arange(GRID_W), rows).astype(F32)
    quarter = d // 4
    inv = jnp.power(ROPE_BASE, -jnp.arange(quarter, dtype=F32) / quarter)
    ar = row[:, None] * inv
    ac = col[:, None] * inv
    ang = jnp.concatenate([ar, ar, ac, ac], axis=-1)
    cos, sin = jnp.cos(ang), jnp.sin(ang)
    first = (jnp.arange(d) % (2 * quarter)) < quarter
    sa = jnp.where(first, -sin, 0.0)
    sb = jnp.where(first, 0.0, sin)
    reps = LANE // d
    return tuple(jnp.tile(a, (1, reps)) for a in (cos, sa, sb))


def _rope(x, cos, sa, sb, quarter):
    n = x.shape[-1]
    ahead = pltpu.roll(x, n - quarter, 1)
    behind = pltpu.roll(x, quarter, 1)
    return x * cos + ahead * sa + behind * sb


def _log_sigmoid(x):
    return jnp.minimum(x, 0.0) - jnp.log1p(jnp.exp(-jnp.abs(x)))


def _mlstm_kernel(t, has_state, emit_state, *refs):
    it = iter(refs)
    q_ref, k_ref, v_ref, og_ref, gt_ref, bias_ref, gain_ref = (next(it) for _ in range(7))
    if has_state:
        c0_ref, n0_ref, m0_ref = next(it), next(it), next(it)
    o_ref = next(it)
    if emit_state:
        cout_ref, nm_ref = next(it), next(it)
    hf_scr, hb_scr = next(it), next(it)

    L = CHUNK
    nc = t // L
    r_i = lax.broadcasted_iota(jnp.int32, (L, L), 0)
    c_i = lax.broadcasted_iota(jnp.int32, (L, L), 1)
    masks = (r_i >= c_i, r_i <= c_i)
    cum = tuple(mk.astype(BF16) for mk in masks)
    bias = bias_ref[...]
    scale = MIX_DIM ** -0.5

    def chunk_dir(c, d, S, n, m):
        rows = pl.ds(pl.multiple_of(c * L, L), L)
        q = q_ref[rows, :]
        k = k_ref[rows, :] * scale
        v = v_ref[rows, :]
        g = gt_ref[rows, :] + bias
        lf = _log_sigmoid(g)
        lf_hi = lf.astype(BF16)
        lf_lo = (lf - lf_hi.astype(F32)).astype(BF16)
        f_all = _dot(cum[d], lf_hi) + _dot(cum[d], lf_lo)
        fcol = f_all[:, 2 * d + 1:2 * d + 2]
        bcol = g[:, 2 * d:2 * d + 1] - fcol
        brow = jnp.broadcast_to(bcol, (L, L)).T[0:1, :]
        f_last = fcol[L - 1:L, :] if d == 0 else fcol[0:1, :]
        lw = jnp.where(masks[d], fcol + brow, NEG)
        inter = fcol + m
        m_tok = jnp.maximum(jnp.max(lw, axis=-1, keepdims=True), inter)
        qb, vb = q.astype(BF16), v.astype(BF16)
        a = _dot_nt(qb, k.astype(BF16)) * jnp.exp(lw - m_tok)
        w_inter = jnp.exp(inter - m_tok)
        num = _dot(a.astype(BF16), vb) + w_inter * _dot(qb, S.astype(BF16))
        den = jnp.sum(a, axis=-1, keepdims=True) + w_inter * jnp.sum(q * n, axis=-1, keepdims=True)
        hh = num * (1.0 / jnp.maximum(jnp.abs(den), jnp.exp(-m_tok)))
        bmax = jnp.max(bcol, axis=0, keepdims=True)
        a_end = f_last + bmax
        kw = k * jnp.exp(bcol - bmax)
        U = _dot_tn(kw.astype(BF16), vb)
        u = jnp.sum(kw, axis=0, keepdims=True)
        m_new = jnp.maximum(f_last + m, a_end)
        d_old = jnp.exp(f_last + m - m_new)
        d_new = jnp.exp(a_end - m_new)
        return hh, d_old * S + d_new * U, d_old * n + d_new * u, m_new

    def body(c, carry):
        sf, nf, mf, sb, nb, mb = carry
        hf, sf, nf, mf = chunk_dir(c, 0, sf, nf, mf)
        hf_scr[pl.ds(pl.multiple_of(c * L, L), L), :] = hf
        cb = nc - 1 - c
        hb, sb, nb, mb = chunk_dir(cb, 1, sb, nb, mb)
        hb_scr[pl.ds(pl.multiple_of(cb * L, L), L), :] = hb
        return sf, nf, mf, sb, nb, mb

    if has_state:
        bi, hi = pl.program_id(0), pl.program_id(1)
        init = (c0_ref[0], n0_ref[0:1, :], jnp.full((1, 1), m0_ref[bi, 0, hi], F32),
                c0_ref[1], n0_ref[1:2, :], jnp.full((1, 1), m0_ref[bi, 1, hi], F32))
    else:
        z_s, z_n, z_m = jnp.zeros((MIX_DIM, MIX_DIM), F32), jnp.zeros((1, MIX_DIM), F32), jnp.zeros((1, 1), F32)
        init = (z_s, z_n, z_m, z_s, z_n, z_m)
    sf, nf, mf, sb, nb, mb = lax.fori_loop(0, nc, body, init)

    gain = gain_ref[...]

    def finish(c, _):
        rows = pl.ds(pl.multiple_of(c * L, L), L)
        y = _head_norm(hf_scr[rows, :] + hb_scr[rows, :], gain)
        o_ref[rows, :] = (_sigmoid(og_ref[rows, :]) * y).astype(BF16)
        return 0

    lax.fori_loop(0, nc, finish, 0)

    if emit_state:
        cout_ref[0] = sf
        cout_ref[1] = sb
        nm_ref[0:1, :] = nf
        nm_ref[1:2, :] = nb
        nm_ref[2:3, :] = jnp.broadcast_to(mf, (1, MIX_DIM))
        nm_ref[3:4, :] = jnp.broadcast_to(mb, (1, MIX_DIM))
        nm_ref[4:8, :] = jnp.zeros((4, MIX_DIM), F32)


def _mlstm(p, nb, t, row_off, bias, gain, state):
    assert row_off % t == 0
    ob = row_off // t
    has_state = state is not None
    emit_state = not has_state

    def col(cb):
        return pl.BlockSpec((t, LANE), lambda b, h: (ob + b, cb + h))

    in_specs = [col(COL_A_Q), col(COL_A_K), col(COL_A_V), col(COL_A_O), col(COL_GATES),
                pl.BlockSpec((None, 1, LANE), lambda b, h: (h, 0, 0)),
                pl.BlockSpec((1, LANE), lambda b, h: (0, h))]
    args = [p, p, p, p, p, bias, gain]
    if has_state:
        c0, n0, m0 = state
        in_specs += [pl.BlockSpec((None, 2, None, MIX_DIM, MIX_DIM), lambda b, h: (b, 0, h, 0, 0)),
                     pl.BlockSpec((None, None, 2, MIX_DIM), lambda b, h: (b, h, 0, 0)),
                     pl.BlockSpec(memory_space=pltpu.SMEM)]
        args += [c0, n0, m0]
    out_shape = [jax.ShapeDtypeStruct((nb * t, BRANCH_W), BF16)]
    out_specs = [pl.BlockSpec((t, LANE), lambda b, h: (b, h))]
    if emit_state:
        out_shape += [jax.ShapeDtypeStruct((nb, 2, MIX_HEADS, MIX_DIM, MIX_DIM), F32),
                      jax.ShapeDtypeStruct((nb, MIX_HEADS, 8, MIX_DIM), F32)]
        out_specs += [pl.BlockSpec((None, 2, None, MIX_DIM, MIX_DIM), lambda b, h: (b, 0, h, 0, 0)),
                      pl.BlockSpec((None, None, 8, MIX_DIM), lambda b, h: (b, h, 0, 0))]
    return pl.pallas_call(
        functools.partial(_mlstm_kernel, t, has_state, emit_state),
        grid=(nb, MIX_HEADS),
        in_specs=in_specs,
        out_specs=out_specs,
        out_shape=out_shape,
        scratch_shapes=[pltpu.VMEM((t, MIX_DIM), F32), pltpu.VMEM((t, MIX_DIM), F32)],
        compiler_params=_params(("parallel", "parallel")),
        name="mlstm_lat" if has_state else "mlstm_ctx",
    )(*args)


def _ret_kernel(t, has_state, emit_state, *refs):
    it = iter(refs)
    q_ref, k_ref, v_ref, dg_ref, lg_ref, gain_ref = (next(it) for _ in range(6))
    if has_state:
        cos_ref, sa_ref, sb_ref, s0_ref = (next(it) for _ in range(4))
    o_ref = next(it)
    if emit_state:
        sout_ref = next(it)
    q_scr, k_scr, of_scr, ob_scr = (next(it) for _ in range(4))

    L = CHUNK
    nc = t // L
    hi = pl.program_id(1)
    lgs = (lg_ref[0, hi], lg_ref[1, hi])
    diff = (lax.broadcasted_iota(jnp.int32, (L, L), 0) - lax.broadcasted_iota(jnp.int32, (L, L), 1)).astype(F32)
    tcol = lax.broadcasted_iota(jnp.int32, (L, 1), 0).astype(F32)
    dist = (jnp.maximum(diff, 0.0), jnp.maximum(-diff, 0.0))
    keep = (diff >= 0.0, diff <= 0.0)
    dm = tuple(jnp.where(keep[d], jnp.exp(dist[d] * lgs[d]), 0.0) for d in range(2))
    w_end = (jnp.exp((L - 1.0 - tcol) * lgs[0]), jnp.exp(tcol * lgs[1]))
    w_cross = (jnp.exp((tcol + 1.0) * lgs[0]), jnp.exp((L - tcol) * lgs[1]))
    decay = tuple(jnp.exp(jnp.full((1, 1), float(L), F32) * lgs[d]) for d in range(2))
    scale = MIX_DIM ** -0.5

    def prep(c, _):
        rows = pl.ds(pl.multiple_of(c * L, L), L)
        q = q_ref[rows, :]
        k = k_ref[rows, :] * scale
        if has_state:
            cos, sa, sb = cos_ref[rows, :], sa_ref[rows, :], sb_ref[rows, :]
            q = _rope(q, cos, sa, sb, MIX_DIM // 4)
            k = _rope(k, cos, sa, sb, MIX_DIM // 4)
        q_scr[rows, :] = q.astype(BF16)
        k_scr[rows, :] = k
        return 0

    lax.fori_loop(0, nc, prep, 0)

    def chunk_dir(c, d, S):
        rows = pl.ds(pl.multiple_of(c * L, L), L)
        qb = q_scr[rows, :]
        k = k_scr[rows, :]
        vb = v_ref[rows, :].astype(BF16)
        inner = _dot((_dot_nt(qb, k.astype(BF16)) * dm[d]).astype(BF16), vb)
        cross = _dot(qb, S.astype(BF16)) * w_cross[d]
        U = _dot_tn((k * w_end[d]).astype(BF16), vb)
        return inner + cross, decay[d] * S + U

    def body(c, carry):
        sf, sb = carry
        of, sf = chunk_dir(c, 0, sf)
        of_scr[pl.ds(pl.multiple_of(c * L, L), L), :] = of
        cb = nc - 1 - c
        obk, sb = chunk_dir(cb, 1, sb)
        ob_scr[pl.ds(pl.multiple_of(cb * L, L), L), :] = obk
        return sf, sb

    if has_state:
        init = (s0_ref[0], s0_ref[1])
    else:
        init = (jnp.zeros((MIX_DIM, MIX_DIM), F32), jnp.zeros((MIX_DIM, MIX_DIM), F32))
    sf, sb = lax.fori_loop(0, nc, body, init)

    gain = gain_ref[...]

    def finish(c, _):
        rows = pl.ds(pl.multiple_of(c * L, L), L)
        y = _head_norm(of_scr[rows, :] + ob_scr[rows, :], gain)
        dg = dg_ref[rows, :]
        o_ref[rows, :] = ((dg * _sigmoid(dg)) * y).astype(BF16)
        return 0

    lax.fori_loop(0, nc, finish, 0)

    if emit_state:
        sout_ref[0] = sf
        sout_ref[1] = sb


def _retention(p, nb, t, row_off, log_decay, gain, state):
    assert row_off % t == 0
    ob = row_off // t
    has_state = state is not None
    emit_state = not has_state

    def col(cb):
        return pl.BlockSpec((t, LANE), lambda b, h: (ob + b, cb + h))

    in_specs = [col(COL_D_Q), col(COL_D_K), col(COL_D_V), col(COL_D_G),
                pl.BlockSpec(memory_space=pltpu.SMEM),
                pl.BlockSpec((1, LANE), lambda b, h: (0, h))]
    args = [p, p, p, p, log_decay, gain]
    if has_state:
        (cos, sa, sb), s0 = state
        tab = pl.BlockSpec((t, LANE), lambda b, h: (0, 0))
        in_specs += [tab, tab, tab,
                     pl.BlockSpec((None, 2, None, MIX_DIM, MIX_DIM), lambda b, h: (b, 0, h, 0, 0))]
        args += [cos, sa, sb, s0]
    out_shape = [jax.ShapeDtypeStruct((nb * t, BRANCH_W), BF16)]
    out_specs = [pl.BlockSpec((t, LANE), lambda b, h: (b, h))]
    if emit_state:
        out_shape.append(jax.ShapeDtypeStruct((nb, 2, MIX_HEADS, MIX_DIM, MIX_DIM), F32))
        out_specs.append(pl.BlockSpec((None, 2, None, MIX_DIM, MIX_DIM), lambda b, h: (b, 0, h, 0, 0)))
    return pl.pallas_call(
        functools.partial(_ret_kernel, t, has_state, emit_state),
        grid=(nb, MIX_HEADS),
        in_specs=in_specs,
        out_specs=out_specs,
        out_shape=out_shape,
        scratch_shapes=[pltpu.VMEM((t, MIX_DIM), BF16), pltpu.VMEM((t, MIX_DIM), F32),
                        pltpu.VMEM((t, MIX_DIM), F32), pltpu.VMEM((t, MIX_DIM), F32)],
        compiler_params=_params(("parallel", "parallel")),
        name="ret_lat" if has_state else "ret_ctx",
    )(*args)


def _softmax_sink_pv(s_parts, v_parts, sink):
    m = sink
    for s in s_parts:
        m = jnp.maximum(m, jnp.max(s, axis=-1, keepdims=True))
    es = [jnp.exp(s - m) for s in s_parts]
    den = jnp.exp(sink - m)
    for e in es:
        den = den + jnp.sum(e, axis=-1, keepdims=True)
    inv = 1.0 / den
    out = None
    for e, v in zip(es, v_parts):
        o = _dot((e * inv).astype(BF16), v)
        out = o if out is None else out + o
    return out


def _attn_ctx_kernel(q_ref, k_ref, v_ref, sink_ref, o_ref, ko_ref, vo_ref):
    k = k_ref[...]
    v = v_ref[...]
    ko_ref[...] = k
    vo_ref[...] = v
    hd = ATT_HEAD_DIM
    scale = hd ** -0.5
    for j in range(ATT_KV_HEADS):
        kj = k[:, j * hd:(j + 1) * hd].astype(BF16)
        vj = v[:, j * hd:(j + 1) * hd].astype(BF16)
        for g in range(ATT_GROUPS):
            hx = j * ATT_GROUPS + g
            qh = q_ref[:, hx * hd:(hx + 1) * hd].astype(BF16)
            s = _dot_nt(qh, kj) * scale
            sink = jnp.full((1, 1), sink_ref[hx], F32)
            o_ref[:, hx * hd:(hx + 1) * hd] = _softmax_sink_pv([s], [vj], sink).astype(BF16)


def _attn_ctx(p, nb, t, row_off, sink):
    assert row_off % t == 0
    ob = row_off // t
    qw = ATT_HEADS * ATT_HEAD_DIM
    kvw = ATT_KV_HEADS * ATT_HEAD_DIM
    return pl.pallas_call(
        _attn_ctx_kernel,
        grid=(nb,),
        in_specs=[
            pl.BlockSpec((t, qw), lambda b: (ob + b, COL_B_Q * LANE // qw)),
            pl.BlockSpec((t, kvw), lambda b: (ob + b, COL_B_K)),
            pl.BlockSpec((t, kvw), lambda b: (ob + b, COL_B_V)),
            pl.BlockSpec(memory_space=pltpu.SMEM),
        ],
        out_specs=[pl.BlockSpec((t, qw), lambda b: (b, 0)),
                   pl.BlockSpec((None, t, kvw), lambda b: (b, 0, 0)),
                   pl.BlockSpec((None, t, kvw), lambda b: (b, 0, 0))],
        out_shape=[jax.ShapeDtypeStruct((nb * t, qw), BF16),
                   jax.ShapeDtypeStruct((nb, t, kvw), F32),
                   jax.ShapeDtypeStruct((nb, t, kvw), F32)],
        compiler_params=_params(("parallel",)),
        name="attn_ctx",
    )(p, p, p, sink)


def _attn_lat_kernel(t, q_ref, k_ref, v_ref, kc_ref, vc_ref, cq_ref, saq_ref, sbq_ref,
                     ck_ref, sak_ref, sbk_ref, sink_ref, o_ref, kp_scr, vp_scr):
    n = pl.program_id(1)
    Q = QBLOCK
    hd = ATT_HEAD_DIM
    quarter = hd // 4

    @pl.when(n == 0)
    def _():
        zero = jnp.zeros((Q, LANE), BF16)
        kp_scr[0:Q, :] = zero
        vp_scr[0:Q, :] = zero
        kp_scr[Q + t:2 * Q + t, :] = zero
        vp_scr[Q + t:2 * Q + t, :] = zero

        def fill(c, _):
            rows = pl.ds(pl.multiple_of(c * Q, Q), Q)
            dst = pl.ds(pl.multiple_of(c * Q + Q, Q), Q)
            kk = _rope(k_ref[rows, :], ck_ref[rows, :], sak_ref[rows, :], sbk_ref[rows, :], quarter)
            kp_scr[dst, :] = kk.astype(BF16)
            vp_scr[dst, :] = v_ref[rows, :].astype(BF16)
            return 0

        lax.fori_loop(0, t // Q, fill, 0)

    band = pl.ds(pl.multiple_of(n * Q, Q), 3 * Q)
    kband = kp_scr[band, :]
    vband = vp_scr[band, :]
    kc = kc_ref[...].astype(BF16)
    vc = vc_ref[...].astype(BF16)
    cq, saq, sbq = cq_ref[...], saq_ref[...], sbq_ref[...]
    qr = [_rope(q_ref[:, c * LANE:(c + 1) * LANE], cq, saq, sbq, quarter).astype(BF16)
          for c in range(ATT_HEADS * hd // LANE)]

    G = ATT_GROUPS
    r_i = lax.broadcasted_iota(jnp.int32, (G * Q, 3 * Q), 0) & (Q - 1)
    c_i = lax.broadcasted_iota(jnp.int32, (G * Q, 3 * Q), 1)
    kpos = (n - 1) * Q + c_i
    valid = (c_i >= r_i) & (c_i <= r_i + 2 * Q) & (kpos >= 0) & (kpos < t)
    scale = hd ** -0.5
    per_lane = LANE // hd
    for j in range(ATT_KV_HEADS):
        heads = [j * G + g for g in range(G)]
        q4 = jnp.concatenate(
            [qr[hx // per_lane][:, (hx % per_lane) * hd:(hx % per_lane + 1) * hd] for hx in heads], axis=0)
        sink = jnp.concatenate([jnp.full((Q, 1), sink_ref[hx], F32) for hx in heads], axis=0)
        sl = slice(j * hd, (j + 1) * hd)
        s_loc = jnp.where(valid, _dot_nt(q4, kband[:, sl]) * scale, NEG)
        s_ctx = _dot_nt(q4, kc[:, sl]) * scale
        o4 = _softmax_sink_pv([s_loc, s_ctx], [vband[:, sl], vc[:, sl]], sink)
        for g, hx in enumerate(heads):
            o_ref[:, hx * hd:(hx + 1) * hd] = o4[g * Q:(g + 1) * Q, :].astype(BF16)


def _attn_lat(p, nb, t, kc, vc, tabs, sink):
    qw = ATT_HEADS * ATT_HEAD_DIM
    kvw = ATT_KV_HEADS * ATT_HEAD_DIM
    nq = t // QBLOCK
    past = kc.shape[1]
    cos, sa, sb = tabs
    tq = pl.BlockSpec((QBLOCK, LANE), lambda b, n: (n, 0))
    tk = pl.BlockSpec((t, LANE), lambda b, n: (0, 0))
    return pl.pallas_call(
        functools.partial(_attn_lat_kernel, t),
        grid=(nb, nq),
        in_specs=[
            pl.BlockSpec((QBLOCK, qw), lambda b, n: (b * nq + n, COL_B_Q * LANE // qw)),
            pl.BlockSpec((t, kvw), lambda b, n: (b, COL_B_K)),
            pl.BlockSpec((t, kvw), lambda b, n: (b, COL_B_V)),
            pl.BlockSpec((None, past, kvw), lambda b, n: (b, 0, 0)),
            pl.BlockSpec((None, past, kvw), lambda b, n: (b, 0, 0)),
            tq, tq, tq, tk, tk, tk,
            pl.BlockSpec(memory_space=pltpu.SMEM),
        ],
        out_specs=pl.BlockSpec((QBLOCK, qw), lambda b, n: (b * nq + n, 0)),
        out_shape=jax.ShapeDtypeStruct((nb * t, qw), BF16),
        scratch_shapes=[pltpu.VMEM((t + 2 * QBLOCK, kvw), BF16), pltpu.VMEM((t + 2 * QBLOCK, kvw), BF16)],
        compiler_params=_params(("parallel", "arbitrary")),
        name="attn_lat",
    )(p, p, p, kc, vc, cos, sa, sb, cos, sa, sb, sink)


def _conv_kernel(t, cb_ref, cc_ref, cx_ref, w_ref, o_ref):
    u = cc_ref[...] * cx_ref[...]
    tok = lax.broadcasted_iota(jnp.int32, u.shape, 0)
    prev = jnp.where(tok == 0, 0.0, pltpu.roll(u, 1, 0))
    nxt = jnp.where(tok == t - 1, 0.0, pltpu.roll(u, t - 1, 0))
    o_ref[...] = (cb_ref[...] * (w_ref[0:1, :] * prev + w_ref[1:2, :] * u + w_ref[2:3, :] * nxt)).astype(BF16)


def _conv(p, nb, t, row_off, w):
    assert row_off % t == 0
    ob = row_off // t
    nblk = w.shape[1] // LANE

    def col(cb):
        return pl.BlockSpec((t, LANE), lambda b, j: (ob + b, cb + j))

    return pl.pallas_call(
        functools.partial(_conv_kernel, t),
        grid=(nb, nblk),
        in_specs=[col(COL_C_B), col(COL_C_C), col(COL_C_X),
                  pl.BlockSpec((w.shape[0], LANE), lambda b, j: (0, j))],
        out_specs=pl.BlockSpec((t, LANE), lambda b, j: (b, j)),
        out_shape=jax.ShapeDtypeStruct((nb * t, w.shape[1]), BF16),
        compiler_params=_params(("parallel", "parallel")),
        name="conv",
    )(p, p, p, w)


def _stage_mixer_proj(w_in):
    depth, d, _ = w_in.shape
    a = w_in[..., 0:2048]
    gates = w_in[..., 2048:2064]
    b_q = w_in[..., 2064:2576]
    b_kv = w_in[..., 2576:2832]
    c = w_in[..., 2832:4368]
    dd = w_in[..., 4368:6416]
    g = gates.reshape(depth, d, 4, MIX_HEADS).transpose(0, 1, 3, 2)
    g = jnp.pad(g, ((0, 0), (0, 0), (0, 0), (0, LANE - 4))).reshape(depth, d, MIX_HEADS * LANE)
    out = jnp.concatenate([a, c, dd, b_q, b_kv, g], axis=-1).astype(BF16)
    assert out.shape[-1] == PROJ_P
    return out


def kernel(x_prompt, x_sample, cache_attn_k, cache_attn_v, state_mlstm_C, state_mlstm_n, state_mlstm_m,
           state_ret_S, c, c_ctx, w_mod, b_mod, norm_g, ffn1_w_in, ffn1_w_out, ffn2_w_in, ffn2_w_out,
           w_in, mlstm_gate_b, mlstm_norm_g, attn_sink, conv_w, ret_log_decay, ret_norm_g,
           w_branch, w_gate, w_out):
    bp, tp, d = x_prompt.shape
    bs, ts, _ = x_sample.shape
    depth = w_mod.shape[0]
    past = cache_attn_k.shape[2]
    kvw = ATT_KV_HEADS * ATT_HEAD_DIM
    tm = 512 if (ts % 512 == 0 and (bp * tp) % 512 == 0) else 256
    rows = _Rows(bs, ts, bp, tp, tm)

    nbp = -(-(1 + bs) // 8) * 8
    cpad = jnp.concatenate([c_ctx[None], c, jnp.zeros((nbp - 1 - bs, d), F32)], axis=0)
    mod = _modulation(cpad, w_mod, b_mod).reshape(depth, nbp, 3, 3, d)

    w_p = _stage_mixer_proj(w_in)
    ffn1_in, ffn1_out = ffn1_w_in.astype(BF16), ffn1_w_out.astype(BF16)
    ffn2_in, ffn2_out = ffn2_w_in.astype(BF16), ffn2_w_out.astype(BF16)
    wg_b, wb_b, wo_b = w_gate.astype(BF16), w_branch.astype(BF16), w_out.astype(BF16)

    gate_bias = jnp.pad(mlstm_gate_b.transpose(0, 2, 1), ((0, 0), (0, 0), (0, LANE - 4)))[:, :, None, :]
    tabs_att = _rope_tables(ts, ATT_HEAD_DIM)
    tabs_ret = _rope_tables(ts, MIX_DIM)

    x = jnp.concatenate([x_sample.reshape(bs * ts, d), x_prompt.reshape(bp * tp, d)], axis=0)
    ms = bs * ts
    ks_, vs_, cs_, ns_, mms_, ss_ = [], [], [], [], [], []
    for l in range(depth):
        g = norm_g[l]
        x = _ffn(rows, x, mod[l, :, 0], g[0:2], ffn1_in[l], ffn1_out[l])

        p = _inproj(rows, x, mod[l, :, 1], g[2:4], w_p[l])
        a_gain = mlstm_norm_g[l][None]
        d_gain = ret_norm_g[l][None]
        lat_a = _mlstm(p, bs, ts, 0, gate_bias[l], a_gain,
                       (state_mlstm_C[:, l], state_mlstm_n[:, l].transpose(0, 2, 1, 3), state_mlstm_m[:, l]))[0]
        lat_b = _attn_lat(p, bs, ts, cache_attn_k[:, l].reshape(bs, past, kvw),
                          cache_attn_v[:, l].reshape(bs, past, kvw), tabs_att, attn_sink[l])
        lat_c = _conv(p, bs, ts, 0, conv_w[l])
        lat_d = _retention(p, bs, ts, 0, ret_log_decay[l], d_gain, (tabs_ret, state_ret_S[:, l]))[0]
        ctx_a, c_fin, nm_fin = _mlstm(p, bp, tp, ms, gate_bias[l], a_gain, None)
        ctx_b, k_new, v_new = _attn_ctx(p, bp, tp, ms, attn_sink[l])
        ctx_c = _conv(p, bp, tp, ms, conv_w[l])
        ctx_d, s_fin = _retention(p, bp, tp, ms, ret_log_decay[l], d_gain, None)
        y = _merge(rows, x, mod[l, :, 1], g[2:4], (lat_a, lat_b, lat_c, lat_d), (ctx_a, ctx_b, ctx_c, ctx_d),
                   wg_b[l], wb_b[l])
        x = _outproj(rows, x, mod[l, :, 1], g[2:4], y, wo_b[l])

        x = _ffn(rows, x, mod[l, :, 2], g[4:6], ffn2_in[l], ffn2_out[l])

        ks_.append(k_new.reshape(bp, tp, ATT_KV_HEADS, ATT_HEAD_DIM))
        vs_.append(v_new.reshape(bp, tp, ATT_KV_HEADS, ATT_HEAD_DIM))
        cs_.append(c_fin)
        ns_.append(nm_fin[:, :, 0:2, :].transpose(0, 2, 1, 3))
        mms_.append(nm_fin[:, :, 2:4, 0].transpose(0, 2, 1))
        ss_.append(s_fin)

    y_sample = x[:ms].reshape(bs, ts, d)
    y_prompt = x[ms:].reshape(bp, tp, d)
    return (y_prompt, y_sample, jnp.stack(ks_, axis=1), jnp.stack(vs_, axis=1), jnp.stack(cs_, axis=1),
            jnp.stack(ns_, axis=1), jnp.stack(mms_, axis=1), jnp.stack(ss_, axis=1))
```

```python
import functools

import numpy as np
import jax
import jax.numpy as jnp
from jax import lax
from jax.experimental import pallas as pl
from jax.experimental.pallas import tpu as pltpu

F32 = jnp.float32
BF16 = jnp.bfloat16

N_MOD = 9
N_BRANCH = 4
BRANCH_W = 512
MIX_HEADS = 4
MIX_DIM = 128
ATT_HEADS = 8
ATT_KV_HEADS = 2
ATT_HEAD_DIM = 64
ATT_GROUPS = ATT_HEADS // ATT_KV_HEADS
QBLOCK = 128
CHUNK = 128
GRID_W = 64
ROPE_BASE = 10000.0
EPS = 1e-6
NEG = -1e30

LANE = 128
COL_A_Q, COL_A_K, COL_A_V, COL_A_O = 0, 4, 8, 12
COL_B_Q, COL_B_K, COL_B_V = 16, 20, 21
COL_C_B, COL_C_C, COL_C_X = 22, 26, 30
COL_D_Q, COL_D_K, COL_D_V, COL_D_G = 34, 38, 42, 46
COL_GATES = 50
PROJ_BLOCKS = 52
PROJ_P = PROJ_BLOCKS * LANE

MIB = 1024 * 1024
VMEM_CAP = 44 * MIB
VMEM_TEMPS = 12 * MIB
SINGLE = pl.Buffered(1)


def _params(sem, buffers):
    return pltpu.CompilerParams(dimension_semantics=sem,
                                vmem_limit_bytes=int(min(buffers + VMEM_TEMPS, VMEM_CAP)))


def _pick_tile(n, pref):
    if n <= pref:
        return n
    t = (pref // LANE) * LANE
    while t >= LANE:
        if n % t == 0:
            return t
        t -= LANE
    raise ValueError(f"no lane-aligned tile divides {n}")


def _dot(a, b):
    return jnp.dot(a, b, preferred_element_type=F32)


def _dot_nt(a, b):
    return lax.dot_general(a, b, (((1,), (1,)), ((), ())), preferred_element_type=F32)


def _dot_tn(a, b):
    return lax.dot_general(a, b, (((0,), (0,)), ((), ())), preferred_element_type=F32)


def _sigmoid(x):
    return 1.0 / (1.0 + jnp.exp(-x))


def _rms(xf, g):
    ms = jnp.mean(xf * xf, axis=-1, keepdims=True)
    return xf * lax.rsqrt(ms + EPS) * g


def _modulated(x_ref, mod_ref, g_ref):
    return _rms(x_ref[...], g_ref[0:1, :]) * (1.0 + mod_ref[1:2, :]) + mod_ref[0:1, :]


def _head_norm(x, g):
    mu = jnp.mean(x, axis=-1, keepdims=True)
    xc = x - mu
    var = jnp.mean(xc * xc, axis=-1, keepdims=True)
    return xc * lax.rsqrt(var + EPS) * g


def _chunk_rows(c, n=CHUNK):
    return pl.ds(pl.multiple_of(c * n, n), n)


def _mod_kernel(c_ref, w_ref, b_ref, o_ref):
    c = c_ref[...]
    s = (c * _sigmoid(c)).astype(BF16)
    o_ref[...] = _dot(s, w_ref[...].astype(BF16)) + b_ref[...]


def _modulation(cpad, w_mod, b_mod):
    depth, d, n = w_mod.shape
    nbp = cpad.shape[0]
    tn = _pick_tile(n, 1024)
    return pl.pallas_call(
        _mod_kernel,
        grid=(depth, n // tn),
        in_specs=[
            pl.BlockSpec((nbp, d), lambda l, j: (0, 0)),
            pl.BlockSpec((None, d, tn), lambda l, j: (l, 0, j)),
            pl.BlockSpec((None, 1, tn), lambda l, j: (l, 0, j)),
        ],
        out_specs=pl.BlockSpec((None, nbp, tn), lambda l, j: (l, 0, j)),
        out_shape=jax.ShapeDtypeStruct((depth, nbp, n), F32),
        compiler_params=_params(("parallel", "parallel"), 2 * d * tn * 4 + MIB),
        name="modulation",
    )(cpad, w_mod, b_mod.reshape(depth, 1, n))


class _Rows:
    def __init__(self, bs, ts, bp, tp, tm):
        self.bs, self.ts, self.bp, self.tp = bs, ts, bp, tp
        self.ms, self.mp = bs * ts, bp * tp
        self.m = self.ms + self.mp
        assert ts % tm == 0 and self.mp % tm == 0 and self.ms % tp == 0
        self.tm = tm
        self.lat_tiles = self.ms // tm
        self.tiles_per_seq = ts // tm

    def mod_index(self, i):
        return jnp.where(i < self.lat_tiles, 1 + i // self.tiles_per_seq, 0)

    def x_spec(self, d):
        return pl.BlockSpec((self.tm, d), lambda i, j: (i, 0), pipeline_mode=SINGLE)

    def mod_spec(self, d):
        return pl.BlockSpec((None, 3, d), lambda i, j: (self.mod_index(i), 0, 0))


def _ffn_kernel(x_ref, mod_ref, g_ref, wg_ref, wu_ref, wo_ref, o_ref, h_scr, acc_scr):
    f = pl.program_id(1)

    @pl.when(f == 0)
    def _():
        h_scr[...] = _modulated(x_ref, mod_ref, g_ref).astype(BF16)
        acc_scr[...] = jnp.zeros_like(acc_scr)

    h = h_scr[...]
    gate = _dot(h, wg_ref[...])
    up = _dot(h, wu_ref[...])
    a = (gate * _sigmoid(gate)) * up
    acc_scr[...] += _dot(a.astype(BF16), wo_ref[...])

    @pl.when(f == pl.num_programs(1) - 1)
    def _():
        y = _rms(acc_scr[...], g_ref[1:2, :])
        o_ref[...] = x_ref[...] + (0.5 * mod_ref[2:3, :]) * y


def _ffn(rows, x, mod_s, g2, w_in, w_out):
    m, d = x.shape
    dff = w_out.shape[0]
    tm = rows.tm
    tf = _pick_tile(dff, 512)
    nf = dff // tf
    buffers = tm * d * (4 + 4 + 4 + 2) + 3 * 2 * d * tf * 2
    return pl.pallas_call(
        _ffn_kernel,
        grid=(m // tm, nf),
        in_specs=[
            rows.x_spec(d),
            rows.mod_spec(d),
            pl.BlockSpec((2, d), lambda i, f: (0, 0)),
            pl.BlockSpec((d, tf), lambda i, f: (0, f)),
            pl.BlockSpec((d, tf), lambda i, f: (0, f + nf)),
            pl.BlockSpec((tf, d), lambda i, f: (f, 0)),
        ],
        out_specs=pl.BlockSpec((tm, d), lambda i, f: (i, 0), pipeline_mode=SINGLE),
        out_shape=jax.ShapeDtypeStruct((m, d), F32),
        scratch_shapes=[pltpu.VMEM((tm, d), BF16), pltpu.VMEM((tm, d), F32)],
        compiler_params=_params(("parallel", "arbitrary"), buffers),
        name="ffn",
    )(x, mod_s, g2, w_in, w_in, w_out)


def _inproj_kernel(x_ref, mod_ref, g_ref, w_ref, p_ref, h_scr):
    @pl.when(pl.program_id(1) == 0)
    def _():
        h_scr[...] = _modulated(x_ref, mod_ref, g_ref).astype(BF16)

    p_ref[...] = _dot(h_scr[...], w_ref[...])


def _inproj(rows, x, mod_s, g2, w_p):
    m, d = x.shape
    n = w_p.shape[1]
    tm = rows.tm
    tn = _pick_tile(n, 512)
    return pl.pallas_call(
        _inproj_kernel,
        grid=(m // tm, n // tn),
        in_specs=[
            rows.x_spec(d),
            rows.mod_spec(d),
            pl.BlockSpec((2, d), lambda i, j: (0, 0)),
            pl.BlockSpec((d, tn), lambda i, j: (0, j)),
        ],
        out_specs=pl.BlockSpec((tm, tn), lambda i, j: (i, j)),
        out_shape=jax.ShapeDtypeStruct((m, n), F32),
        scratch_shapes=[pltpu.VMEM((tm, d), BF16)],
        compiler_params=_params(("parallel", "arbitrary"), tm * d * (4 + 2) + 2 * d * tn * 2 + 2 * tm * tn * 4),
        name="inproj",
    )(x, mod_s, g2, w_p)


def _merge_kernel(lat_tiles, x_ref, mod_ref, g_ref, *refs):
    lat_refs, ctx_refs = refs[0:N_BRANCH], refs[N_BRANCH:2 * N_BRANCH]
    wg_ref, wb_ref, y_ref, h_scr = refs[2 * N_BRANCH:]

    @pl.when(pl.program_id(1) == 0)
    def _():
        h_scr[...] = _modulated(x_ref, mod_ref, g_ref).astype(BF16)

    h = h_scr[...]
    is_lat = pl.program_id(0) < lat_tiles
    y = None
    for i in range(N_BRANCH):
        o = jnp.where(is_lat, lat_refs[i][...], ctx_refs[i][...])
        term = _sigmoid(_dot(h, wg_ref[i])) * _dot(o, wb_ref[i])
        y = term if y is None else y + term
    y_ref[...] = y.astype(BF16)


def _merge(rows, x, mod_s, g2, lat, ctx, w_gate, w_branch):
    m, d = x.shape
    tm = rows.tm
    tn = _pick_tile(d, 256)
    bw = w_branch.shape[1]
    last_lat = rows.lat_tiles - 1
    lat_spec = pl.BlockSpec((tm, bw), lambda i, j: (jnp.minimum(i, last_lat), 0), pipeline_mode=SINGLE)
    ctx_spec = pl.BlockSpec((tm, bw), lambda i, j: (jnp.maximum(i - rows.lat_tiles, 0), 0), pipeline_mode=SINGLE)
    return pl.pallas_call(
        functools.partial(_merge_kernel, rows.lat_tiles),
        grid=(m // tm, d // tn),
        in_specs=[
            rows.x_spec(d),
            rows.mod_spec(d),
            pl.BlockSpec((2, d), lambda i, j: (0, 0)),
            *([lat_spec] * N_BRANCH), *([ctx_spec] * N_BRANCH),
            pl.BlockSpec((N_BRANCH, d, tn), lambda i, j: (0, 0, j)),
            pl.BlockSpec((N_BRANCH, bw, tn), lambda i, j: (0, 0, j)),
        ],
        out_specs=pl.BlockSpec((tm, tn), lambda i, j: (i, j)),
        out_shape=jax.ShapeDtypeStruct((m, d), BF16),
        scratch_shapes=[pltpu.VMEM((tm, d), BF16)],
        compiler_params=_params(("parallel", "arbitrary"),
                                tm * d * (4 + 2) + 2 * N_BRANCH * tm * bw * 2
                                + 2 * N_BRANCH * (d + bw) * tn * 2 + 2 * tm * tn * 2),
        name="merge",
    )(x, mod_s, g2, *lat, *ctx, w_gate, w_branch)


def _outproj_kernel(nt, tn, x_ref, mod_ref, g_ref, y_ref, w_ref, o_ref, z_scr):
    j = pl.program_id(1)
    z_scr[j] = _dot(y_ref[...], w_ref[...])

    @pl.when(j == nt - 1)
    def _():
        ss = None
        for k in range(nt):
            z = z_scr[k]
            s = jnp.sum(z * z, axis=-1, keepdims=True)
            ss = s if ss is None else ss + s
        inv = lax.rsqrt(ss * (1.0 / (nt * tn)) + EPS)
        for k in range(nt):
            sl = slice(k * tn, (k + 1) * tn)
            y = z_scr[k] * inv * g_ref[1:2, sl]
            o_ref[:, sl] = x_ref[:, sl] + mod_ref[2:3, sl] * y


def _outproj(rows, x, mod_s, g2, y, w_out):
    m, d = x.shape
    tm = rows.tm
    tn = _pick_tile(d, 512)
    nt = d // tn
    return pl.pallas_call(
        functools.partial(_outproj_kernel, nt, tn),
        grid=(m // tm, nt),
        in_specs=[
            rows.x_spec(d),
            rows.mod_spec(d),
            pl.BlockSpec((2, d), lambda i, j: (0, 0)),
            rows.x_spec(d),
            pl.BlockSpec((d, tn), lambda i, j: (0, j)),
        ],
        out_specs=pl.BlockSpec((tm, d), lambda i, j: (i, 0), pipeline_mode=SINGLE),
        out_shape=jax.ShapeDtypeStruct((m, d), F32),
        scratch_shapes=[pltpu.VMEM((nt, tm, tn), F32)],
        compiler_params=_params(("parallel", "arbitrary"), tm * d * (4 + 4 + 4 + 2) + 2 * d * tn * 2),
        name="outproj",
    )(x, mod_s, g2, y, w_out)


def _rope_tables(t, d):
    rows = t // GRID_W
    row = jnp.repeat(jnp.arange(rows), GRID_W).astype(F32)
    col = jnp.tile(jnp.arange(GRID_W), rows).astype(F32)
    quarter = d // 4
    inv = jnp.power(ROPE_BASE, -jnp.arange(quarter, dtype=F32) / quarter)
    ar = row[:, None] * inv
    ac = col[:, None] * inv
    ang = jnp.concatenate([ar, ar, ac, ac], axis=-1)
    cos, sin = jnp.cos(ang), jnp.sin(ang)
    first = (jnp.arange(d) % (2 * quarter)) < quarter
    sa = jnp.where(first, -sin, 0.0)
    sb = jnp.where(first, 0.0, sin)
    reps = LANE // d
    return tuple(jnp.tile(a, (1, reps)) for a in (cos, sa, sb))


def _rope(x, cos, sa, sb, quarter):
    n = x.shape[-1]
    ahead = pltpu.roll(x, n - quarter, 1)
    behind = pltpu.roll(x, quarter, 1)
    return x * cos + ahead * sa + behind * sb


def _log_sigmoid(x):
    return jnp.minimum(x, 0.0) - jnp.log1p(jnp.exp(-jnp.abs(x)))


def _split_bf16(x):
    hi = x.astype(BF16)
    return hi, (x - hi.astype(F32)).astype(BF16)


def _mlstm_kernel(t, has_state, emit_state, *refs):
    it = iter(refs)
    q_ref, k_ref, v_ref, og_ref, gt_ref, bias_ref, gain_ref = (next(it) for _ in range(7))
    if has_state:
        c0_ref, n0_ref, m0_ref = next(it), next(it), next(it)
    o_ref = next(it)
    if emit_state:
        cout_ref, nout_ref, mout_ref = next(it), next(it), next(it)
    h_scr, s_scr = next(it), next(it)

    L, D, H = CHUNK, MIX_DIM, MIX_HEADS
    nc = t // L
    r_i = lax.broadcasted_iota(jnp.int32, (L, L), 0)
    c_i = lax.broadcasted_iota(jnp.int32, (L, L), 1)
    masks = (r_i >= c_i, r_i <= c_i)
    cum = tuple(mk.astype(BF16) for mk in masks)
    lane0 = lax.broadcasted_iota(jnp.int32, (L, D), 1) == 0
    e0 = lane0.astype(BF16)
    bias = bias_ref[...]
    scale = D ** -0.5

    ms0 = []
    for d in range(2):
        for h in range(H):
            if has_state:
                ncol = jnp.broadcast_to(n0_ref[d, h:h + 1, :], (D, D)).T
                s_scr[d, h, :, 0:D] = c0_ref[d, h]
                s_scr[d, h, :, D:2 * D] = jnp.where(lane0, ncol, 0.0)
                ms0.append(jnp.full((1, 1), m0_ref[pl.program_id(0), d, h], F32))
            else:
                s_scr[d, h] = jnp.zeros((D, 2 * D), F32)
                ms0.append(jnp.zeros((1, 1), F32))

    def zero_rows(c, _):
        h_scr[_chunk_rows(c), :] = jnp.zeros((L, H * D), F32)
        return 0

    lax.fori_loop(0, nc, zero_rows, 0)

    def chunk_dir(c, d, ms):
        rows = _chunk_rows(c)
        g = gt_ref[rows, :] + bias
        lf_hi, lf_lo = _split_bf16(_log_sigmoid(g))
        f_all = _dot(cum[d], lf_hi) + _dot(cum[d], lf_lo)
        g_t = g.T[0:4 * H, :]
        lt_hi, lt_lo = _split_bf16(_log_sigmoid(g_t))
        f_t = _dot_nt(lt_hi, cum[d]) + _dot_nt(lt_lo, cum[d])
        out = []
        for h in range(H):
            j = 4 * h + 2 * d
            cols = slice(h * D, (h + 1) * D)
            q = q_ref[rows, cols]
            k = k_ref[rows, cols] * scale
            qb = q.astype(BF16)
            v_aug = jnp.concatenate([v_ref[rows, cols].astype(BF16), e0], axis=1)
            fcol = f_all[:, j + 1:j + 2]
            bcol = g[:, j:j + 1] - fcol
            brow = g_t[j:j + 1, :] - f_t[j + 1:j + 2, :]
            f_last = fcol[L - 1:L, :] if d == 0 else fcol[0:1, :]
            m = ms[d * H + h]
            lw = jnp.where(masks[d], fcol + brow, NEG)
            inter = fcol + m
            m_tok = jnp.maximum(jnp.max(lw, axis=-1, keepdims=True), inter)
            a = _dot_nt(qb, k.astype(BF16)) * jnp.exp(lw - m_tok)
            w_inter = jnp.exp(inter - m_tok)
            s_old = s_scr[d, h]
            pv = _dot(a.astype(BF16), v_aug)
            qs = _dot(qb, s_old.astype(BF16))
            num = pv[:, 0:D] + w_inter * qs[:, 0:D]
            den = pv[:, D:D + 1] + w_inter * qs[:, D:D + 1]
            hh = num * (1.0 / jnp.maximum(jnp.abs(den), jnp.exp(-m_tok)))
            h_scr[rows, cols] += hh
            bmax = jnp.max(bcol, axis=0, keepdims=True)
            a_end = f_last + bmax
            kw = k * jnp.exp(bcol - bmax)
            u_aug = _dot_tn(kw.astype(BF16), v_aug)
            m_new = jnp.maximum(f_last + m, a_end)
            s_scr[d, h] = jnp.exp(f_last + m - m_new) * s_old + jnp.exp(a_end - m_new) * u_aug
            out.append(m_new)
        return out

    def body(c, ms):
        mf = chunk_dir(c, 0, ms)
        mb = chunk_dir(nc - 1 - c, 1, ms)
        return tuple(mf + mb)

    ms = lax.fori_loop(0, nc, body, tuple(ms0))

    gain = gain_ref[...]

    def finish(c, _):
        rows = _chunk_rows(c)
        for h in range(H):
            cols = slice(h * D, (h + 1) * D)
            y = _head_norm(h_scr[rows, cols], gain[:, cols])
            o_ref[rows, cols] = (_sigmoid(og_ref[rows, cols]) * y).astype(BF16)
        return 0

    lax.fori_loop(0, nc, finish, 0)

    if emit_state:
        for d in range(2):
            for h in range(H):
                cout_ref[d, h] = s_scr[d, h, :, 0:D]
                nout_ref[d, h:h + 1, :] = s_scr[d, h, :, D:2 * D].T[0:1, :]
                mout_ref[d * H + h:d * H + h + 1, :] = jnp.broadcast_to(ms[d * H + h], (1, D))


def _mlstm(p, nb, t, row_off, bias, gain, state):
    assert row_off % t == 0
    ob = row_off // t
    has_state = state is not None
    emit_state = not has_state
    hw = MIX_HEADS * MIX_DIM
    st_c = pl.BlockSpec((None, 2, MIX_HEADS, MIX_DIM, MIX_DIM), lambda b: (b, 0, 0, 0, 0))
    st_n = pl.BlockSpec((None, 2, MIX_HEADS, MIX_DIM), lambda b: (b, 0, 0, 0))

    def col(cb):
        return pl.BlockSpec((t, hw), lambda b: (ob + b, cb * LANE // hw), pipeline_mode=mode)

    mode = SINGLE if t * hw * 4 >= 2 * MIB else None
    nbuf = 1 if mode is SINGLE else 2
    buffers = (nbuf * (4 * t * hw + t * LANE) * 4 + 2 * t * hw * 2 + t * hw * 4
               + 6 * 2 * MIX_HEADS * MIX_DIM * MIX_DIM * 4)
    in_specs = [col(COL_A_Q), col(COL_A_K), col(COL_A_V), col(COL_A_O),
                pl.BlockSpec((t, LANE), lambda b: (ob + b, COL_GATES), pipeline_mode=mode),
                pl.BlockSpec((1, LANE), lambda b: (0, 0)),
                pl.BlockSpec((1, hw), lambda b: (0, 0))]
    args = [p, p, p, p, p, bias, gain]
    if has_state:
        in_specs += [st_c, st_n, pl.BlockSpec(memory_space=pltpu.SMEM)]
        args += list(state)
    out_shape = [jax.ShapeDtypeStruct((nb * t, hw), BF16)]
    out_specs = [pl.BlockSpec((t, hw), lambda b: (b, 0))]
    if emit_state:
        out_shape += [jax.ShapeDtypeStruct((nb, 2, MIX_HEADS, MIX_DIM, MIX_DIM), F32),
                      jax.ShapeDtypeStruct((nb, 2, MIX_HEADS, MIX_DIM), F32),
                      jax.ShapeDtypeStruct((nb, 2 * MIX_HEADS, MIX_DIM), F32)]
        out_specs += [st_c, st_n, pl.BlockSpec((None, 2 * MIX_HEADS, MIX_DIM), lambda b: (b, 0, 0))]
    return pl.pallas_call(
        functools.partial(_mlstm_kernel, t, has_state, emit_state),
        grid=(nb,),
        in_specs=in_specs,
        out_specs=out_specs,
        out_shape=out_shape,
        scratch_shapes=[pltpu.VMEM((t, hw), F32),
                        pltpu.VMEM((2, MIX_HEADS, MIX_DIM, 2 * MIX_DIM), F32)],
        compiler_params=_params(("parallel",), buffers),
        name="mlstm_lat" if has_state else "mlstm_ctx",
    )(*args)


def _ret_kernel(t, has_state, emit_state, *refs):
    it = iter(refs)
    q_ref, k_ref, v_ref, dg_ref, lg_ref, gain_ref = (next(it) for _ in range(6))
    if has_state:
        cos_ref, sa_ref, sb_ref, s0_ref = (next(it) for _ in range(4))
    o_ref = next(it)
    if emit_state:
        sout_ref = next(it)
    q_scr, u_scr, o_scr = (next(it) for _ in range(3))

    L, D = CHUNK, MIX_DIM
    nc = t // L
    hi = pl.program_id(1)
    lgs = (lg_ref[0, hi], lg_ref[1, hi])
    diff = (lax.broadcasted_iota(jnp.int32, (L, L), 0) - lax.broadcasted_iota(jnp.int32, (L, L), 1)).astype(F32)
    tcol = lax.broadcasted_iota(jnp.int32, (L, 1), 0).astype(F32)
    dm = (jnp.where(diff >= 0.0, jnp.exp(jnp.maximum(diff, 0.0) * lgs[0]), 0.0)
          + jnp.where(diff <= 0.0, jnp.exp(jnp.maximum(-diff, 0.0) * lgs[1]), 0.0))
    w_end = (jnp.exp((L - 1.0 - tcol) * lgs[0]), jnp.exp(tcol * lgs[1]))
    w_cross = (jnp.exp((tcol + 1.0) * lgs[0]), jnp.exp((L - tcol) * lgs[1]))
    decay = tuple(jnp.exp(jnp.full((1, 1), float(L), F32) * lgs[d]) for d in range(2))
    scale = D ** -0.5

    def local(c, _):
        rows = _chunk_rows(c)
        q = q_ref[rows, :]
        k = k_ref[rows, :] * scale
        if has_state:
            cos, sa, sb = cos_ref[rows, :], sa_ref[rows, :], sb_ref[rows, :]
            q = _rope(q, cos, sa, sb, D // 4)
            k = _rope(k, cos, sa, sb, D // 4)
        qb = q.astype(BF16)
        vb = v_ref[rows, :].astype(BF16)
        q_scr[rows, :] = qb
        o_scr[rows, :] = _dot((_dot_nt(qb, k.astype(BF16)) * dm).astype(BF16), vb)
        kw = jnp.concatenate([(k * w_end[0]).astype(BF16), (k * w_end[1]).astype(BF16)], axis=1)
        u_scr[c] = _dot_tn(kw, vb)
        return 0

    lax.fori_loop(0, nc, local, 0, unroll=2)

    def scan(c, carry):
        sf, sb = carry
        rows = _chunk_rows(c)
        o_scr[rows, :] += _dot(q_scr[rows, :], sf.astype(BF16)) * w_cross[0]
        sf = decay[0] * sf + u_scr[c, 0:D, :]
        cb = nc - 1 - c
        rows_b = _chunk_rows(cb)
        o_scr[rows_b, :] += _dot(q_scr[rows_b, :], sb.astype(BF16)) * w_cross[1]
        sb = decay[1] * sb + u_scr[cb, D:2 * D, :]
        return sf, sb

    if has_state:
        init = (s0_ref[0], s0_ref[1])
    else:
        init = (jnp.zeros((D, D), F32), jnp.zeros((D, D), F32))
    sf, sb = lax.fori_loop(0, nc, scan, init, unroll=2)

    gain = gain_ref[...]

    def finish(c, _):
        rows = _chunk_rows(c)
        y = _head_norm(o_scr[rows, :], gain)
        dg = dg_ref[rows, :]
        o_ref[rows, :] = ((dg * _sigmoid(dg)) * y).astype(BF16)
        return 0

    lax.fori_loop(0, nc, finish, 0, unroll=2)

    if emit_state:
        sout_ref[0] = sf
        sout_ref[1] = sb


def _retention(p, nb, t, row_off, log_decay, gain, state):
    assert row_off % t == 0
    ob = row_off // t
    has_state = state is not None
    emit_state = not has_state

    def col(cb):
        return pl.BlockSpec((t, LANE), lambda b, h: (ob + b, cb + h))

    in_specs = [col(COL_D_Q), col(COL_D_K), col(COL_D_V), col(COL_D_G),
                pl.BlockSpec(memory_space=pltpu.SMEM),
                pl.BlockSpec((1, LANE), lambda b, h: (0, h))]
    args = [p, p, p, p, log_decay, gain]
    if has_state:
        (cos, sa, sb), s0 = state
        tab = pl.BlockSpec((t, LANE), lambda b, h: (0, 0))
        in_specs += [tab, tab, tab,
                     pl.BlockSpec((None, 2, None, MIX_DIM, MIX_DIM), lambda b, h: (b, 0, h, 0, 0))]
        args += [cos, sa, sb, s0]
    out_shape = [jax.ShapeDtypeStruct((nb * t, BRANCH_W), BF16)]
    out_specs = [pl.BlockSpec((t, LANE), lambda b, h: (b, h))]
    if emit_state:
        out_shape.append(jax.ShapeDtypeStruct((nb, 2, MIX_HEADS, MIX_DIM, MIX_DIM), F32))
        out_specs.append(pl.BlockSpec((None, 2, None, MIX_DIM, MIX_DIM), lambda b, h: (b, 0, h, 0, 0)))
    return pl.pallas_call(
        functools.partial(_ret_kernel, t, has_state, emit_state),
        grid=(nb, MIX_HEADS),
        in_specs=in_specs,
        out_specs=out_specs,
        out_shape=out_shape,
        scratch_shapes=[pltpu.VMEM((t, MIX_DIM), BF16),
                        pltpu.VMEM((t // CHUNK, 2 * MIX_DIM, MIX_DIM), F32),
                        pltpu.VMEM((t, MIX_DIM), F32)],
        compiler_params=_params(("parallel", "parallel"), 2 * 7 * t * LANE * 4 + 3 * t * LANE * 4 + 2 * MIB),
        name="ret_lat" if has_state else "ret_ctx",
    )(*args)


def _softmax_sink_pv(s_parts, v_parts, sink):
    m = sink
    for s in s_parts:
        m = jnp.maximum(m, jnp.max(s, axis=-1, keepdims=True))
    es = [jnp.exp(s - m) for s in s_parts]
    den = jnp.exp(sink - m)
    for e in es:
        den = den + jnp.sum(e, axis=-1, keepdims=True)
    out = None
    for e, v in zip(es, v_parts):
        o = _dot(e.astype(BF16), v)
        out = o if out is None else out + o
    return out * (1.0 / den)


def _attn_ctx_kernel(q_ref, k_ref, v_ref, sink_ref, o_ref, ko_ref, vo_ref):
    k = k_ref[...]
    v = v_ref[...]
    ko_ref[...] = k
    vo_ref[...] = v
    hd = ATT_HEAD_DIM
    scale = hd ** -0.5
    for j in range(ATT_KV_HEADS):
        kj = k[:, j * hd:(j + 1) * hd].astype(BF16)
        vj = v[:, j * hd:(j + 1) * hd].astype(BF16)
        for g in range(ATT_GROUPS):
            hx = j * ATT_GROUPS + g
            qh = q_ref[:, hx * hd:(hx + 1) * hd].astype(BF16)
            s = _dot_nt(qh, kj) * scale
            sink = jnp.full((1, 1), sink_ref[hx], F32)
            o_ref[:, hx * hd:(hx + 1) * hd] = _softmax_sink_pv([s], [vj], sink).astype(BF16)


def _attn_ctx(p, nb, t, row_off, sink):
    assert row_off % t == 0
    ob = row_off // t
    qw = ATT_HEADS * ATT_HEAD_DIM
    kvw = ATT_KV_HEADS * ATT_HEAD_DIM
    return pl.pallas_call(
        _attn_ctx_kernel,
        grid=(nb,),
        in_specs=[
            pl.BlockSpec((t, qw), lambda b: (ob + b, COL_B_Q * LANE // qw)),
            pl.BlockSpec((t, kvw), lambda b: (ob + b, COL_B_K)),
            pl.BlockSpec((t, kvw), lambda b: (ob + b, COL_B_V)),
            pl.BlockSpec(memory_space=pltpu.SMEM),
        ],
        out_specs=[pl.BlockSpec((t, qw), lambda b: (b, 0)),
                   pl.BlockSpec((None, t, kvw), lambda b: (b, 0, 0)),
                   pl.BlockSpec((None, t, kvw), lambda b: (b, 0, 0))],
        out_shape=[jax.ShapeDtypeStruct((nb * t, qw), BF16),
                   jax.ShapeDtypeStruct((nb, t, kvw), F32),
                   jax.ShapeDtypeStruct((nb, t, kvw), F32)],
        compiler_params=_params(("parallel",), 2 * t * (qw * 6 + 4 * kvw * 4)),
        name="attn_ctx",
    )(p, p, p, sink)


def _band_bias():
    Q = QBLOCK
    r = np.arange(Q)[:, None]
    c = np.arange(3 * Q)[None, :]
    band = (c >= r) & (c <= r + 2 * Q)
    out = []
    for kind in range(3):
        ok = band.copy()
        if kind == 0:
            ok &= c >= Q
        if kind == 2:
            ok &= c < 2 * Q
        out.append(np.where(ok, 0.0, NEG).astype(np.float32))
    return jnp.asarray(np.stack(out))


def _half_variants(x):
    low = lax.broadcasted_iota(jnp.int32, x.shape, 1) < ATT_HEAD_DIM
    sw = pltpu.roll(x, ATT_HEAD_DIM, 1)
    zero = jnp.zeros_like(x)
    return [jnp.where(low, a, b).astype(BF16) for a, b in ((x, zero), (zero, sw), (sw, zero), (zero, x))]


def _attn_lat_kernel(t, q_ref, k_ref, v_ref, kc_ref, vc_ref, cq_ref, saq_ref, sbq_ref,
                     ck_ref, sak_ref, sbk_ref, bias_ref, sink_ref, o_ref, kp_scr, vp_scr, kc_scr, vc_scr):
    n = pl.program_id(1)
    Q = QBLOCK
    hd = ATT_HEAD_DIM
    quarter = hd // 4
    nvar = 2 * ATT_KV_HEADS

    @pl.when(n == 0)
    def _():
        zero = jnp.zeros((Q, LANE), BF16)
        for vi in range(nvar):
            kp_scr[vi, 0:Q, :] = zero
            vp_scr[vi, 0:Q, :] = zero
            kp_scr[vi, Q + t:2 * Q + t, :] = zero
            vp_scr[vi, Q + t:2 * Q + t, :] = zero

        def fill(c, _):
            rows = _chunk_rows(c, Q)
            dst = pl.ds(pl.multiple_of(c * Q + Q, Q), Q)
            kk = _rope(k_ref[rows, :], ck_ref[rows, :], sak_ref[rows, :], sbk_ref[rows, :], quarter)
            for vi, (kv, vv) in enumerate(zip(_half_variants(kk), _half_variants(v_ref[rows, :]))):
                kp_scr[vi, dst, :] = kv
                vp_scr[vi, dst, :] = vv
            return 0

        lax.fori_loop(0, t // Q, fill, 0)
        for vi, (kv, vv) in enumerate(zip(_half_variants(kc_ref[...]), _half_variants(vc_ref[...]))):
            kc_scr[vi] = kv
            vc_scr[vi] = vv

    band = pl.ds(pl.multiple_of(n * Q, Q), 3 * Q)
    cq, saq, sbq = cq_ref[...], saq_ref[...], sbq_ref[...]
    bias = bias_ref[...]
    scale = hd ** -0.5
    heads_per_chunk = LANE // hd
    for c in range(ATT_HEADS // heads_per_chunk):
        sl = slice(c * LANE, (c + 1) * LANE)
        qc = (_rope(q_ref[:, sl], cq, saq, sbq, quarter) * scale).astype(BF16)
        acc = None
        for half in range(heads_per_chunk):
            hx = c * heads_per_chunk + half
            vi = 2 * (hx // ATT_GROUPS) + half
            s_loc = _dot_nt(qc, kp_scr[vi, band, :]) + bias
            s_ctx = _dot_nt(qc, kc_scr[vi])
            sink = jnp.full((1, 1), sink_ref[hx], F32)
            o = _softmax_sink_pv([s_loc, s_ctx], [vp_scr[vi, band, :], vc_scr[vi]], sink)
            acc = o if acc is None else acc + o
        o_ref[:, sl] = acc.astype(BF16)


def _attn_lat(p, nb, t, kc, vc, tabs, sink):
    assert (ATT_HEAD_DIM ** -0.5) == 2.0 ** round(np.log2(ATT_HEAD_DIM ** -0.5))
    qw = ATT_HEADS * ATT_HEAD_DIM
    kvw = ATT_KV_HEADS * ATT_HEAD_DIM
    assert kvw == LANE
    nq = t // QBLOCK
    past = kc.shape[1]
    nvar = 2 * ATT_KV_HEADS
    cos, sa, sb = tabs
    bias = _band_bias()
    tq = pl.BlockSpec((QBLOCK, LANE), lambda b, n: (n, 0))
    tk = pl.BlockSpec((t, LANE), lambda b, n: (0, 0))
    return pl.pallas_call(
        functools.partial(_attn_lat_kernel, t),
        grid=(nb, nq),
        in_specs=[
            pl.BlockSpec((QBLOCK, qw), lambda b, n: (b * nq + n, COL_B_Q * LANE // qw)),
            pl.BlockSpec((t, kvw), lambda b, n: (b, COL_B_K)),
            pl.BlockSpec((t, kvw), lambda b, n: (b, COL_B_V)),
            pl.BlockSpec((None, past, kvw), lambda b, n: (b, 0, 0)),
            pl.BlockSpec((None, past, kvw), lambda b, n: (b, 0, 0)),
            tq, tq, tq, tk, tk, tk,
            pl.BlockSpec((None,) + bias.shape[1:],
                         lambda b, n: (jnp.where(n == 0, 0, jnp.where(n == nq - 1, 2, 1)), 0, 0)),
            pl.BlockSpec(memory_space=pltpu.SMEM),
        ],
        out_specs=pl.BlockSpec((QBLOCK, qw), lambda b, n: (b * nq + n, 0)),
        out_shape=jax.ShapeDtypeStruct((nb * t, qw), BF16),
        scratch_shapes=[pltpu.VMEM((nvar, t + 2 * QBLOCK, LANE), BF16), pltpu.VMEM((nvar, t + 2 * QBLOCK, LANE), BF16),
                        pltpu.VMEM((nvar, past, LANE), BF16), pltpu.VMEM((nvar, past, LANE), BF16)],
        compiler_params=_params(("parallel", "arbitrary"),
                                2 * 5 * t * LANE * 4 + 2 * nvar * (t + 2 * QBLOCK + past) * LANE * 2 + 2 * MIB),
        name="attn_lat",
    )(p, p, p, kc, vc, cos, sa, sb, cos, sa, sb, bias, sink)


def _conv_kernel(t, cb_ref, cc_ref, cx_ref, w_ref, o_ref):
    u = cc_ref[...] * cx_ref[...]
    tok = lax.broadcasted_iota(jnp.int32, u.shape, 0)
    prev = jnp.where(tok == 0, 0.0, pltpu.roll(u, 1, 0))
    nxt = jnp.where(tok == t - 1, 0.0, pltpu.roll(u, t - 1, 0))
    o_ref[...] = (cb_ref[...] * (w_ref[0:1, :] * prev + w_ref[1:2, :] * u + w_ref[2:3, :] * nxt)).astype(BF16)


def _conv(p, nb, t, row_off, w):
    assert row_off % t == 0
    ob = row_off // t
    nblk = w.shape[1] // LANE

    def col(cb):
        return pl.BlockSpec((t, LANE), lambda b, j: (ob + b, cb + j))

    return pl.pallas_call(
        functools.partial(_conv_kernel, t),
        grid=(nb, nblk),
        in_specs=[col(COL_C_B), col(COL_C_C), col(COL_C_X),
                  pl.BlockSpec((w.shape[0], LANE), lambda b, j: (0, j))],
        out_specs=pl.BlockSpec((t, LANE), lambda b, j: (b, j)),
        out_shape=jax.ShapeDtypeStruct((nb * t, w.shape[1]), BF16),
        compiler_params=_params(("parallel", "parallel"), 2 * 4 * t * LANE * 4),
        name="conv",
    )(p, p, p, w)


GATE_SRC = 4 * MIX_HEADS * MIX_DIM
N_GATES = 4 * MIX_HEADS


def _gate_lanes(x):
    g = x.reshape(x.shape[:-1] + (4, MIX_HEADS))
    g = jnp.swapaxes(g, -1, -2).reshape(x.shape[:-1] + (N_GATES,))
    return jnp.pad(g, [(0, 0)] * (x.ndim - 1) + [(0, LANE - N_GATES)])


def _stage_mixer_proj(w_in):
    head = w_in[..., 0:GATE_SRC]
    tail = w_in[..., GATE_SRC + N_GATES:]
    gates = _gate_lanes(w_in[..., GATE_SRC:GATE_SRC + N_GATES])
    pad = jnp.zeros(w_in.shape[:-1] + (PROJ_P - head.shape[-1] - tail.shape[-1] - LANE,), w_in.dtype)
    assert head.shape[-1] + tail.shape[-1] == COL_GATES * LANE
    return jnp.concatenate([head, tail, gates, pad], axis=-1).astype(BF16)


def kernel(x_prompt, x_sample, cache_attn_k, cache_attn_v, state_mlstm_C, state_mlstm_n, state_mlstm_m,
           state_ret_S, c, c_ctx, w_mod, b_mod, norm_g, ffn1_w_in, ffn1_w_out, ffn2_w_in, ffn2_w_out,
           w_in, mlstm_gate_b, mlstm_norm_g, attn_sink, conv_w, ret_log_decay, ret_norm_g,
           w_branch, w_gate, w_out):
    bp, tp, d = x_prompt.shape
    bs, ts, _ = x_sample.shape
    depth = w_mod.shape[0]
    past = cache_attn_k.shape[2]
    kvw = ATT_KV_HEADS * ATT_HEAD_DIM
    fits = [t for t in (1024, 512, 256) if ts % t == 0 and (bp * tp) % t == 0]
    rows = _Rows(bs, ts, bp, tp, fits[0])
    rows_ffn = _Rows(bs, ts, bp, tp, fits[min(1, len(fits) - 1)])

    nbp = -(-(1 + bs) // 8) * 8
    cpad = jnp.concatenate([c_ctx[None], c, jnp.zeros((nbp - 1 - bs, d), F32)], axis=0)
    mod = _modulation(cpad, w_mod, b_mod).reshape(depth, nbp, 3, 3, d)

    w_p = _stage_mixer_proj(w_in)
    ffn1_in, ffn1_out = ffn1_w_in.astype(BF16), ffn1_w_out.astype(BF16)
    ffn2_in, ffn2_out = ffn2_w_in.astype(BF16), ffn2_w_out.astype(BF16)
    wg_b, wb_b, wo_b = w_gate.astype(BF16), w_branch.astype(BF16), w_out.astype(BF16)

    gate_bias = _gate_lanes(mlstm_gate_b.reshape(depth, 1, N_GATES))
    tabs_att = _rope_tables(ts, ATT_HEAD_DIM)
    tabs_ret = _rope_tables(ts, MIX_DIM)

    x = jnp.concatenate([x_sample.reshape(bs * ts, d), x_prompt.reshape(bp * tp, d)], axis=0)
    ms = bs * ts
    ks_, vs_, cs_, ns_, mms_, ss_ = [], [], [], [], [], []
    for l in range(depth):
        g = norm_g[l]
        x = _ffn(rows_ffn, x, mod[l, :, 0], g[0:2], ffn1_in[l], ffn1_out[l])

        p = _inproj(rows, x, mod[l, :, 1], g[2:4], w_p[l])
        a_gain = mlstm_norm_g[l][None]
        d_gain = ret_norm_g[l][None]
        lat_a = _mlstm(p, bs, ts, 0, gate_bias[l], a_gain,
                       (state_mlstm_C[:, l], state_mlstm_n[:, l], state_mlstm_m[:, l]))[0]
        lat_b = _attn_lat(p, bs, ts, cache_attn_k[:, l].reshape(bs, past, kvw),
                          cache_attn_v[:, l].reshape(bs, past, kvw), tabs_att, attn_sink[l])
        lat_c = _conv(p, bs, ts, 0, conv_w[l])
        lat_d = _retention(p, bs, ts, 0, ret_log_decay[l], d_gain, (tabs_ret, state_ret_S[:, l]))[0]
        ctx_a, c_fin, n_fin, m_fin = _mlstm(p, bp, tp, ms, gate_bias[l], a_gain, None)
        ctx_b, k_new, v_new = _attn_ctx(p, bp, tp, ms, attn_sink[l])
        ctx_c = _conv(p, bp, tp, ms, conv_w[l])
        ctx_d, s_fin = _retention(p, bp, tp, ms, ret_log_decay[l], d_gain, None)
        y = _merge(rows, x, mod[l, :, 1], g[2:4], (lat_a, lat_b, lat_c, lat_d), (ctx_a, ctx_b, ctx_c, ctx_d),
                   wg_b[l], wb_b[l])
        x = _outproj(rows, x, mod[l, :, 1], g[2:4], y, wo_b[l])

        x = _ffn(rows_ffn, x, mod[l, :, 2], g[4:6], ffn2_in[l], ffn2_out[l])

        ks_.append(k_new.reshape(bp, tp, ATT_KV_HEADS, ATT_HEAD_DIM))
        vs_.append(v_new.reshape(bp, tp, ATT_KV_HEADS, ATT_HEAD_DIM))
        cs_.append(c_fin)
        ns_.append(n_fin)
        mms_.append(m_fin[:, :, 0].reshape(bp, 2, MIX_HEADS))
        ss_.append(s_fin)

    y_sample = x[:ms].reshape(bs, ts, d)
    y_prompt = x[ms:].reshape(bp, tp, d)
    return (y_prompt, y_sample, jnp.stack(ks_, axis=1), jnp.stack(vs_, axis=1), jnp.stack(cs_, axis=1),
            jnp.stack(ns_, axis=1), jnp.stack(mms_, axis=1), jnp.stack(ss_, axis=1))
```

```python
import functools

import numpy as np
import jax
import jax.numpy as jnp
from jax import lax
from jax.experimental import pallas as pl
from jax.experimental.pallas import tpu as pltpu

F32 = jnp.float32
BF16 = jnp.bfloat16

N_MOD = 9
N_BRANCH = 4
BRANCH_W = 512
MIX_HEADS = 4
MIX_DIM = 128
ATT_HEADS = 8
ATT_KV_HEADS = 2
ATT_HEAD_DIM = 64
ATT_GROUPS = ATT_HEADS // ATT_KV_HEADS
QBLOCK = 128
CHUNK = 128
GRID_W = 64
ROPE_BASE = 10000.0
EPS = 1e-6
NEG = -1e30

LANE = 128
COL_A_Q, COL_A_K, COL_A_V, COL_A_O = 0, 4, 8, 12
COL_B_Q, COL_B_K, COL_B_V = 16, 20, 21
COL_C_B, COL_C_C, COL_C_X = 22, 26, 30
COL_D_Q, COL_D_K, COL_D_V, COL_D_G = 34, 38, 42, 46
COL_GATES = 50
PROJ_BLOCKS = 52
PROJ_P = PROJ_BLOCKS * LANE

MIB = 1024 * 1024
VMEM_CAP = 44 * MIB
VMEM_TEMPS = 12 * MIB
SINGLE = pl.Buffered(1)


def _params(sem, buffers):
    return pltpu.CompilerParams(dimension_semantics=sem,
                                vmem_limit_bytes=int(min(buffers + VMEM_TEMPS, VMEM_CAP)))


def _pick_tile(n, pref):
    if n <= pref:
        return n
    t = (pref // LANE) * LANE
    while t >= LANE:
        if n % t == 0:
            return t
        t -= LANE
    raise ValueError(f"no lane-aligned tile divides {n}")


def _dot(a, b):
    return jnp.dot(a, b, preferred_element_type=F32)


def _dot_nt(a, b):
    return lax.dot_general(a, b, (((1,), (1,)), ((), ())), preferred_element_type=F32)


def _dot_tn(a, b):
    return lax.dot_general(a, b, (((0,), (0,)), ((), ())), preferred_element_type=F32)


def _sigmoid(x):
    return 1.0 / (1.0 + jnp.exp(-x))


def _rms(xf, g):
    ms = jnp.mean(xf * xf, axis=-1, keepdims=True)
    return xf * lax.rsqrt(ms + EPS) * g


def _modulated(x_ref, mod_ref, g_ref):
    return _rms(x_ref[...], g_ref[0:1, :]) * (1.0 + mod_ref[1:2, :]) + mod_ref[0:1, :]


def _head_norm(x, g):
    mu = jnp.mean(x, axis=-1, keepdims=True)
    xc = x - mu
    var = jnp.mean(xc * xc, axis=-1, keepdims=True)
    return xc * lax.rsqrt(var + EPS) * g


def _chunk_rows(c, n=CHUNK):
    return pl.ds(pl.multiple_of(c * n, n), n)


def _mod_kernel(c_ref, w_ref, b_ref, o_ref):
    c = c_ref[...]
    s = (c * _sigmoid(c)).astype(BF16)
    o_ref[...] = _dot(s, w_ref[...].astype(BF16)) + b_ref[...]


def _modulation(cpad, w_mod, b_mod):
    depth, d, n = w_mod.shape
    nbp = cpad.shape[0]
    tn = _pick_tile(n, 1024)
    return pl.pallas_call(
        _mod_kernel,
        grid=(depth, n // tn),
        in_specs=[
            pl.BlockSpec((nbp, d), lambda l, j: (0, 0)),
            pl.BlockSpec((None, d, tn), lambda l, j: (l, 0, j)),
            pl.BlockSpec((None, 1, tn), lambda l, j: (l, 0, j)),
        ],
        out_specs=pl.BlockSpec((None, nbp, tn), lambda l, j: (l, 0, j)),
        out_shape=jax.ShapeDtypeStruct((depth, nbp, n), F32),
        compiler_params=_params(("parallel", "parallel"), 2 * d * tn * 4 + MIB),
        name="modulation",
    )(cpad, w_mod, b_mod.reshape(depth, 1, n))


class _Rows:
    def __init__(self, bs, ts, bp, tp, tm):
        self.bs, self.ts, self.bp, self.tp = bs, ts, bp, tp
        self.ms, self.mp = bs * ts, bp * tp
        self.m = self.ms + self.mp
        assert ts % tm == 0 and self.mp % tm == 0 and self.ms % tp == 0
        self.tm = tm
        self.lat_tiles = self.ms // tm
        self.tiles_per_seq = ts // tm

    def mod_index(self, i):
        return jnp.where(i < self.lat_tiles, 1 + i // self.tiles_per_seq, 0)

    def x_spec(self, d, mode=None):
        return pl.BlockSpec((self.tm, d), lambda i, j: (i, 0), pipeline_mode=mode)

    def mod_spec(self, d):
        return pl.BlockSpec((None, 3, d), lambda i, j: (self.mod_index(i), 0, 0))


def _layer_spec(block, index_map, l):
    return pl.BlockSpec((None,) + tuple(block), lambda *ids: (l,) + tuple(index_map(*ids)))


def _ffn_kernel(x_ref, mod_ref, g_ref, wg_ref, wu_ref, wo_ref, o_ref, h_scr, acc_scr):
    f = pl.program_id(1)

    @pl.when(f == 0)
    def _():
        h_scr[...] = _modulated(x_ref, mod_ref, g_ref).astype(BF16)
        acc_scr[...] = jnp.zeros_like(acc_scr)

    h = h_scr[...]
    gate = _dot(h, wg_ref[...])
    up = _dot(h, wu_ref[...])
    a = (gate * _sigmoid(gate)) * up
    acc_scr[...] += _dot(a.astype(BF16), wo_ref[...])

    @pl.when(f == pl.num_programs(1) - 1)
    def _():
        y = _rms(acc_scr[...], g_ref[1:2, :])
        o_ref[...] = x_ref[...] + (0.5 * mod_ref[2:3, :]) * y


def _ffn(rows, x, mod_s, g2, w_in, w_out, l):
    m, d = x.shape
    dff = w_out.shape[1]
    tm = rows.tm
    tf = _pick_tile(dff, 512)
    nf = dff // tf
    buffers = tm * d * (2 * 4 + 2 * 4 + 4 + 2) + 3 * 2 * d * tf * 2
    return pl.pallas_call(
        _ffn_kernel,
        grid=(m // tm, nf),
        in_specs=[
            rows.x_spec(d),
            rows.mod_spec(d),
            pl.BlockSpec((2, d), lambda i, f: (0, 0)),
            _layer_spec((d, tf), lambda i, f: (0, f), l),
            _layer_spec((d, tf), lambda i, f: (0, f + nf), l),
            _layer_spec((tf, d), lambda i, f: (f, 0), l),
        ],
        out_specs=pl.BlockSpec((tm, d), lambda i, f: (i, 0)),
        out_shape=jax.ShapeDtypeStruct((m, d), F32),
        scratch_shapes=[pltpu.VMEM((tm, d), BF16), pltpu.VMEM((tm, d), F32)],
        compiler_params=_params(("parallel", "arbitrary"), buffers),
        name="ffn",
    )(x, mod_s, g2, w_in, w_in, w_out)


def _inproj_kernel(x_ref, mod_ref, g_ref, w_ref, p_ref, h_ref):
    @pl.when(pl.program_id(1) == 0)
    def _():
        h_ref[...] = _modulated(x_ref, mod_ref, g_ref).astype(BF16)

    p_ref[...] = _dot(h_ref[...], w_ref[...])


def _inproj(rows, x, mod_s, g2, w_p, l):
    m, d = x.shape
    n = w_p.shape[2]
    tm = rows.tm
    tn = _pick_tile(n, 512)
    return pl.pallas_call(
        _inproj_kernel,
        grid=(m // tm, n // tn),
        in_specs=[
            rows.x_spec(d, SINGLE),
            rows.mod_spec(d),
            pl.BlockSpec((2, d), lambda i, j: (0, 0)),
            _layer_spec((d, tn), lambda i, j: (0, j), l),
        ],
        out_specs=[pl.BlockSpec((tm, tn), lambda i, j: (i, j)),
                   pl.BlockSpec((tm, d), lambda i, j: (i, 0))],
        out_shape=[jax.ShapeDtypeStruct((m, n), F32), jax.ShapeDtypeStruct((m, d), BF16)],
        compiler_params=_params(("parallel", "arbitrary"),
                                tm * d * (4 + 2 * 2) + 2 * d * tn * 2 + 2 * tm * tn * 4),
        name="inproj",
    )(x, mod_s, g2, w_p)


def _merge_kernel(lat_tiles, h_ref, *refs):
    lat_refs, ctx_refs = refs[0:N_BRANCH], refs[N_BRANCH:2 * N_BRANCH]
    wg_ref, wb_ref, y_ref = refs[2 * N_BRANCH:]
    h = h_ref[...]
    is_lat = pl.program_id(0) < lat_tiles
    y = None
    for i in range(N_BRANCH):
        o = jnp.where(is_lat, lat_refs[i][...], ctx_refs[i][...])
        term = _sigmoid(_dot(h, wg_ref[i])) * _dot(o, wb_ref[i])
        y = term if y is None else y + term
    y_ref[...] = y.astype(BF16)


def _merge(rows, h, lat, ctx, w_gate, w_branch, l):
    m, d = h.shape
    tm = rows.tm
    tn = _pick_tile(d, 256)
    bw = w_branch.shape[2]
    last_lat = rows.lat_tiles - 1
    lat_spec = pl.BlockSpec((tm, bw), lambda i, j: (jnp.minimum(i, last_lat), 0), pipeline_mode=SINGLE)
    ctx_spec = pl.BlockSpec((tm, bw), lambda i, j: (jnp.maximum(i - rows.lat_tiles, 0), 0), pipeline_mode=SINGLE)
    return pl.pallas_call(
        functools.partial(_merge_kernel, rows.lat_tiles),
        grid=(m // tm, d // tn),
        in_specs=[
            rows.x_spec(d),
            *([lat_spec] * N_BRANCH), *([ctx_spec] * N_BRANCH),
            _layer_spec((N_BRANCH, d, tn), lambda i, j: (0, 0, j), l),
            _layer_spec((N_BRANCH, bw, tn), lambda i, j: (0, 0, j), l),
        ],
        out_specs=pl.BlockSpec((tm, tn), lambda i, j: (i, j)),
        out_shape=jax.ShapeDtypeStruct((m, d), BF16),
        compiler_params=_params(("parallel", "arbitrary"),
                                2 * tm * d * 2 + 2 * N_BRANCH * tm * bw * 2
                                + 2 * N_BRANCH * (d + bw) * tn * 2 + 2 * tm * tn * 2),
        name="merge",
    )(h, *lat, *ctx, w_gate, w_branch)


def _outproj_kernel(nt, tn, x_ref, mod_ref, g_ref, y_ref, w_ref, o_ref, z_scr):
    j = pl.program_id(1)
    z_scr[j] = _dot(y_ref[...], w_ref[...])

    @pl.when(j == nt - 1)
    def _():
        ss = None
        for k in range(nt):
            z = z_scr[k]
            s = jnp.sum(z * z, axis=-1, keepdims=True)
            ss = s if ss is None else ss + s
        inv = lax.rsqrt(ss * (1.0 / (nt * tn)) + EPS)
        for k in range(nt):
            sl = slice(k * tn, (k + 1) * tn)
            y = z_scr[k] * inv * g_ref[1:2, sl]
            o_ref[:, sl] = x_ref[:, sl] + mod_ref[2:3, sl] * y


def _outproj(rows, x, mod_s, g2, y, w_out, l):
    m, d = x.shape
    tm = rows.tm
    tn = _pick_tile(d, 512)
    nt = d // tn
    return pl.pallas_call(
        functools.partial(_outproj_kernel, nt, tn),
        grid=(m // tm, nt),
        in_specs=[
            rows.x_spec(d),
            rows.mod_spec(d),
            pl.BlockSpec((2, d), lambda i, j: (0, 0)),
            rows.x_spec(d),
            _layer_spec((d, tn), lambda i, j: (0, j), l),
        ],
        out_specs=pl.BlockSpec((tm, d), lambda i, j: (i, 0)),
        out_shape=jax.ShapeDtypeStruct((m, d), F32),
        scratch_shapes=[pltpu.VMEM((nt, tm, tn), F32)],
        compiler_params=_params(("parallel", "arbitrary"), tm * d * (2 * 4 + 2 * 4 + 4 + 2 * 2) + 2 * d * tn * 2),
        name="outproj",
    )(x, mod_s, g2, y, w_out)


def _rope_tables(t, d):
    rows = t // GRID_W
    row = jnp.repeat(jnp.arange(rows), GRID_W).astype(F32)
    col = jnp.tile(jnp.arange(GRID_W), rows).astype(F32)
    quarter = d // 4
    inv = jnp.power(ROPE_BASE, -jnp.arange(quarter, dtype=F32) / quarter)
    ar = row[:, None] * inv
    ac = col[:, None] * inv
    ang = jnp.concatenate([ar, ar, ac, ac], axis=-1)
    cos, sin = jnp.cos(ang), jnp.sin(ang)
    first = (jnp.arange(d) % (2 * quarter)) < quarter
    sa = jnp.where(first, -sin, 0.0)
    sb = jnp.where(first, 0.0, sin)
    reps = LANE // d
    return tuple(jnp.tile(a, (1, reps)) for a in (cos, sa, sb))


def _rope(x, cos, sa, sb, quarter):
    n = x.shape[-1]
    ahead = pltpu.roll(x, n - quarter, 1)
    behind = pltpu.roll(x, quarter, 1)
    return x * cos + ahead * sa + behind * sb


def _log_sigmoid(x):
    return jnp.minimum(x, 0.0) - jnp.log1p(jnp.exp(-jnp.abs(x)))


def _split_bf16(x):
    hi = x.astype(BF16)
    return hi, (x - hi.astype(F32)).astype(BF16)


def _mlstm_kernel(t, layer, emit_state, *refs):
    has_state = layer is not None
    it = iter(refs)
    q_ref, k_ref, v_ref, og_ref, gt_ref, bias_ref, gain_ref = (next(it) for _ in range(7))
    if has_state:
        c0_ref, n0_ref, m0_ref = next(it), next(it), next(it)
    o_ref = next(it)
    if emit_state:
        cout_ref, nout_ref, mout_ref = next(it), next(it), next(it)
    h_scr, s_scr = next(it), next(it)

    L, D, H = CHUNK, MIX_DIM, MIX_HEADS
    nc = t // L
    r_i = lax.broadcasted_iota(jnp.int32, (L, L), 0)
    c_i = lax.broadcasted_iota(jnp.int32, (L, L), 1)
    masks = (r_i >= c_i, r_i <= c_i)
    cum = tuple(mk.astype(BF16) for mk in masks)
    lane0 = lax.broadcasted_iota(jnp.int32, (L, D), 1) == 0
    e0 = lane0.astype(BF16)
    bias = bias_ref[...]
    scale = D ** -0.5

    ms0 = []
    for d in range(2):
        for h in range(H):
            if has_state:
                ncol = jnp.broadcast_to(n0_ref[d, h:h + 1, :], (D, D)).T
                s_scr[d, h, :, 0:D] = c0_ref[d, h]
                s_scr[d, h, :, D:2 * D] = jnp.where(lane0, ncol, 0.0)
                ms0.append(jnp.full((1, 1), m0_ref[pl.program_id(0), layer, d, h], F32))
            else:
                s_scr[d, h] = jnp.zeros((D, 2 * D), F32)
                ms0.append(jnp.zeros((1, 1), F32))

    def zero_rows(c, _):
        h_scr[_chunk_rows(c), :] = jnp.zeros((L, H * D), F32)
        return 0

    lax.fori_loop(0, nc, zero_rows, 0)

    def chunk_dir(c, d, ms):
        rows = _chunk_rows(c)
        g = gt_ref[rows, :] + bias
        lf_hi, lf_lo = _split_bf16(_log_sigmoid(g))
        f_all = _dot(cum[d], lf_hi) + _dot(cum[d], lf_lo)
        g_t = g.T[0:4 * H, :]
        lt_hi, lt_lo = _split_bf16(_log_sigmoid(g_t))
        f_t = _dot_nt(lt_hi, cum[d]) + _dot_nt(lt_lo, cum[d])
        out = []
        for h in range(H):
            j = 4 * h + 2 * d
            cols = slice(h * D, (h + 1) * D)
            q = q_ref[rows, cols]
            k = k_ref[rows, cols] * scale
            qb = q.astype(BF16)
            v_aug = jnp.concatenate([v_ref[rows, cols].astype(BF16), e0], axis=1)
            fcol = f_all[:, j + 1:j + 2]
            bcol = g[:, j:j + 1] - fcol
            brow = g_t[j:j + 1, :] - f_t[j + 1:j + 2, :]
            f_last = fcol[L - 1:L, :] if d == 0 else fcol[0:1, :]
            m = ms[d * H + h]
            lw = jnp.where(masks[d], fcol + brow, NEG)
            inter = fcol + m
            m_tok = jnp.maximum(jnp.max(lw, axis=-1, keepdims=True), inter)
            a = _dot_nt(qb, k.astype(BF16)) * jnp.exp(lw - m_tok)
            w_inter = jnp.exp(inter - m_tok)
            s_old = s_scr[d, h]
            pv = _dot(a.astype(BF16), v_aug)
            qs = _dot(qb, s_old.astype(BF16))
            num = pv[:, 0:D] + w_inter * qs[:, 0:D]
            den = pv[:, D:D + 1] + w_inter * qs[:, D:D + 1]
            hh = num * (1.0 / jnp.maximum(jnp.abs(den), jnp.exp(-m_tok)))
            h_scr[rows, cols] += hh
            bmax = jnp.max(bcol, axis=0, keepdims=True)
            a_end = f_last + bmax
            kw = k * jnp.exp(bcol - bmax)
            u_aug = _dot_tn(kw.astype(BF16), v_aug)
            m_new = jnp.maximum(f_last + m, a_end)
            s_scr[d, h] = jnp.exp(f_last + m - m_new) * s_old + jnp.exp(a_end - m_new) * u_aug
            out.append(m_new)
        return out

    def body(c, ms):
        mf = chunk_dir(c, 0, ms)
        mb = chunk_dir(nc - 1 - c, 1, ms)
        return tuple(mf + mb)

    ms = lax.fori_loop(0, nc, body, tuple(ms0))

    gain = gain_ref[...]

    def finish(c, _):
        rows = _chunk_rows(c)
        for h in range(H):
            cols = slice(h * D, (h + 1) * D)
            y = _head_norm(h_scr[rows, cols], gain[:, cols])
            o_ref[rows, cols] = (_sigmoid(og_ref[rows, cols]) * y).astype(BF16)
        return 0

    lax.fori_loop(0, nc, finish, 0)

    if emit_state:
        for d in range(2):
            for h in range(H):
                cout_ref[d, h] = s_scr[d, h, :, 0:D]
                nout_ref[d, h:h + 1, :] = s_scr[d, h, :, D:2 * D].T[0:1, :]
                mout_ref[d * H + h:d * H + h + 1, :] = jnp.broadcast_to(ms[d * H + h], (1, D))


def _mlstm(p, nb, t, row_off, bias, gain, state):
    assert row_off % t == 0
    ob = row_off // t
    has_state = state is not None
    emit_state = not has_state
    hw = MIX_HEADS * MIX_DIM
    st_c = pl.BlockSpec((None, 2, MIX_HEADS, MIX_DIM, MIX_DIM), lambda b: (b, 0, 0, 0, 0))
    st_n = pl.BlockSpec((None, 2, MIX_HEADS, MIX_DIM), lambda b: (b, 0, 0, 0))

    def col(cb):
        return pl.BlockSpec((t, hw), lambda b: (ob + b, cb * LANE // hw), pipeline_mode=mode)

    mode = SINGLE if t * hw * 4 >= 2 * MIB else None
    nbuf = 1 if mode is SINGLE else 2
    buffers = (nbuf * (4 * t * hw + t * LANE) * 4 + 2 * t * hw * 2 + t * hw * 4
               + 6 * 2 * MIX_HEADS * MIX_DIM * MIX_DIM * 4)
    in_specs = [col(COL_A_Q), col(COL_A_K), col(COL_A_V), col(COL_A_O),
                pl.BlockSpec((t, LANE), lambda b: (ob + b, COL_GATES), pipeline_mode=mode),
                pl.BlockSpec((1, LANE), lambda b: (0, 0)),
                pl.BlockSpec((1, hw), lambda b: (0, 0))]
    args = [p, p, p, p, p, bias, gain]
    layer = None
    if has_state:
        layer = state[3]
        in_specs += [pl.BlockSpec((None, None, 2, MIX_HEADS, MIX_DIM, MIX_DIM), lambda b: (b, layer, 0, 0, 0, 0)),
                     pl.BlockSpec((None, None, 2, MIX_HEADS, MIX_DIM), lambda b: (b, layer, 0, 0, 0)),
                     pl.BlockSpec(memory_space=pltpu.SMEM)]
        args += list(state[0:3])
    out_shape = [jax.ShapeDtypeStruct((nb * t, hw), BF16)]
    out_specs = [pl.BlockSpec((t, hw), lambda b: (b, 0))]
    if emit_state:
        out_shape += [jax.ShapeDtypeStruct((nb, 2, MIX_HEADS, MIX_DIM, MIX_DIM), F32),
                      jax.ShapeDtypeStruct((nb, 2, MIX_HEADS, MIX_DIM), F32),
                      jax.ShapeDtypeStruct((nb, 2 * MIX_HEADS, MIX_DIM), F32)]
        out_specs += [st_c, st_n, pl.BlockSpec((None, 2 * MIX_HEADS, MIX_DIM), lambda b: (b, 0, 0))]
    return pl.pallas_call(
        functools.partial(_mlstm_kernel, t, layer, emit_state),
        grid=(nb,),
        in_specs=in_specs,
        out_specs=out_specs,
        out_shape=out_shape,
        scratch_shapes=[pltpu.VMEM((t, hw), F32),
                        pltpu.VMEM((2, MIX_HEADS, MIX_DIM, 2 * MIX_DIM), F32)],
        compiler_params=_params(("parallel",), buffers),
        name="mlstm_lat" if has_state else "mlstm_ctx",
    )(*args)


def _ret_kernel(t, has_state, emit_state, *refs):
    it = iter(refs)
    q_ref, k_ref, v_ref, dg_ref, lg_ref, gain_ref = (next(it) for _ in range(6))
    if has_state:
        cos_ref, sa_ref, sb_ref, s0_ref = (next(it) for _ in range(4))
    o_ref = next(it)
    if emit_state:
        sout_ref = next(it)
    q_scr, u_scr, o_scr = (next(it) for _ in range(3))

    L, D = CHUNK, MIX_DIM
    nc = t // L
    hi = pl.program_id(1)
    lgs = (lg_ref[0, hi], lg_ref[1, hi])
    diff = (lax.broadcasted_iota(jnp.int32, (L, L), 0) - lax.broadcasted_iota(jnp.int32, (L, L), 1)).astype(F32)
    tcol = lax.broadcasted_iota(jnp.int32, (L, 1), 0).astype(F32)
    dm = (jnp.where(diff >= 0.0, jnp.exp(jnp.maximum(diff, 0.0) * lgs[0]), 0.0)
          + jnp.where(diff <= 0.0, jnp.exp(jnp.maximum(-diff, 0.0) * lgs[1]), 0.0))
    w_end = (jnp.exp((L - 1.0 - tcol) * lgs[0]), jnp.exp(tcol * lgs[1]))
    w_cross = (jnp.exp((tcol + 1.0) * lgs[0]), jnp.exp((L - tcol) * lgs[1]))
    decay = tuple(jnp.exp(jnp.full((1, 1), float(L), F32) * lgs[d]) for d in range(2))
    scale = D ** -0.5

    def local(c, _):
        rows = _chunk_rows(c)
        q = q_ref[rows, :]
        k = k_ref[rows, :] * scale
        if has_state:
            cos, sa, sb = cos_ref[rows, :], sa_ref[rows, :], sb_ref[rows, :]
            q = _rope(q, cos, sa, sb, D // 4)
            k = _rope(k, cos, sa, sb, D // 4)
        qb = q.astype(BF16)
        vb = v_ref[rows, :].astype(BF16)
        q_scr[rows, :] = qb
        o_scr[rows, :] = _dot((_dot_nt(qb, k.astype(BF16)) * dm).astype(BF16), vb)
        kw = jnp.concatenate([(k * w_end[0]).astype(BF16), (k * w_end[1]).astype(BF16)], axis=1)
        u_scr[c] = _dot_tn(kw, vb)
        return 0

    lax.fori_loop(0, nc, local, 0, unroll=2)

    def scan(c, carry):
        sf, sb = carry
        rows = _chunk_rows(c)
        o_scr[rows, :] += _dot(q_scr[rows, :], sf.astype(BF16)) * w_cross[0]
        sf = decay[0] * sf + u_scr[c, 0:D, :]
        cb = nc - 1 - c
        rows_b = _chunk_rows(cb)
        o_scr[rows_b, :] += _dot(q_scr[rows_b, :], sb.astype(BF16)) * w_cross[1]
        sb = decay[1] * sb + u_scr[cb, D:2 * D, :]
        return sf, sb

    if has_state:
        init = (s0_ref[0], s0_ref[1])
    else:
        init = (jnp.zeros((D, D), F32), jnp.zeros((D, D), F32))
    sf, sb = lax.fori_loop(0, nc, scan, init, unroll=2)

    gain = gain_ref[...]

    def finish(c, _):
        rows = _chunk_rows(c)
        y = _head_norm(o_scr[rows, :], gain)
        dg = dg_ref[rows, :]
        o_ref[rows, :] = ((dg * _sigmoid(dg)) * y).astype(BF16)
        return 0

    lax.fori_loop(0, nc, finish, 0, unroll=2)

    if emit_state:
        sout_ref[0] = sf
        sout_ref[1] = sb


def _retention(p, nb, t, row_off, log_decay, gain, state):
    assert row_off % t == 0
    ob = row_off // t
    has_state = state is not None
    emit_state = not has_state

    def col(cb):
        return pl.BlockSpec((t, LANE), lambda b, h: (ob + b, cb + h))

    in_specs = [col(COL_D_Q), col(COL_D_K), col(COL_D_V), col(COL_D_G),
                pl.BlockSpec(memory_space=pltpu.SMEM),
                pl.BlockSpec((1, LANE), lambda b, h: (0, h))]
    args = [p, p, p, p, log_decay, gain]
    if has_state:
        (cos, sa, sb), s0, layer = state
        tab = pl.BlockSpec((t, LANE), lambda b, h: (0, 0))
        in_specs += [tab, tab, tab,
                     pl.BlockSpec((None, None, 2, None, MIX_DIM, MIX_DIM), lambda b, h: (b, layer, 0, h, 0, 0))]
        args += [cos, sa, sb, s0]
    out_shape = [jax.ShapeDtypeStruct((nb * t, BRANCH_W), BF16)]
    out_specs = [pl.BlockSpec((t, LANE), lambda b, h: (b, h))]
    if emit_state:
        out_shape.append(jax.ShapeDtypeStruct((nb, 2, MIX_HEADS, MIX_DIM, MIX_DIM), F32))
        out_specs.append(pl.BlockSpec((None, 2, None, MIX_DIM, MIX_DIM), lambda b, h: (b, 0, h, 0, 0)))
    return pl.pallas_call(
        functools.partial(_ret_kernel, t, has_state, emit_state),
        grid=(nb, MIX_HEADS),
        in_specs=in_specs,
        out_specs=out_specs,
        out_shape=out_shape,
        scratch_shapes=[pltpu.VMEM((t, MIX_DIM), BF16),
                        pltpu.VMEM((t // CHUNK, 2 * MIX_DIM, MIX_DIM), F32),
                        pltpu.VMEM((t, MIX_DIM), F32)],
        compiler_params=_params(("parallel", "parallel"), 2 * 7 * t * LANE * 4 + 3 * t * LANE * 4 + 2 * MIB),
        name="ret_lat" if has_state else "ret_ctx",
    )(*args)


def _softmax_sink_pv(s_parts, v_parts, sink):
    m = sink
    for s in s_parts:
        m = jnp.maximum(m, jnp.max(s, axis=-1, keepdims=True))
    es = [jnp.exp(s - m) for s in s_parts]
    den = jnp.exp(sink - m)
    for e in es:
        den = den + jnp.sum(e, axis=-1, keepdims=True)
    out = None
    for e, v in zip(es, v_parts):
        o = _dot(e.astype(BF16), v)
        out = o if out is None else out + o
    return out * (1.0 / den)


def _attn_ctx_kernel(q_ref, k_ref, v_ref, sink_ref, o_ref, ko_ref, vo_ref):
    k = k_ref[...]
    v = v_ref[...]
    ko_ref[...] = k
    vo_ref[...] = v
    hd = ATT_HEAD_DIM
    scale = hd ** -0.5
    for j in range(ATT_KV_HEADS):
        kj = k[:, j * hd:(j + 1) * hd].astype(BF16)
        vj = v[:, j * hd:(j + 1) * hd].astype(BF16)
        for g in range(ATT_GROUPS):
            hx = j * ATT_GROUPS + g
            qh = q_ref[:, hx * hd:(hx + 1) * hd].astype(BF16)
            s = _dot_nt(qh, kj) * scale
            sink = jnp.full((1, 1), sink_ref[hx], F32)
            o_ref[:, hx * hd:(hx + 1) * hd] = _softmax_sink_pv([s], [vj], sink).astype(BF16)


def _attn_ctx(p, nb, t, row_off, sink):
    assert row_off % t == 0
    ob = row_off // t
    qw = ATT_HEADS * ATT_HEAD_DIM
    kvw = ATT_KV_HEADS * ATT_HEAD_DIM
    return pl.pallas_call(
        _attn_ctx_kernel,
        grid=(nb,),
        in_specs=[
            pl.BlockSpec((t, qw), lambda b: (ob + b, COL_B_Q * LANE // qw)),
            pl.BlockSpec((t, kvw), lambda b: (ob + b, COL_B_K)),
            pl.BlockSpec((t, kvw), lambda b: (ob + b, COL_B_V)),
            pl.BlockSpec(memory_space=pltpu.SMEM),
        ],
        out_specs=[pl.BlockSpec((t, qw), lambda b: (b, 0)),
                   pl.BlockSpec((None, t, kvw), lambda b: (b, 0, 0)),
                   pl.BlockSpec((None, t, kvw), lambda b: (b, 0, 0))],
        out_shape=[jax.ShapeDtypeStruct((nb * t, qw), BF16),
                   jax.ShapeDtypeStruct((nb, t, kvw), F32),
                   jax.ShapeDtypeStruct((nb, t, kvw), F32)],
        compiler_params=_params(("parallel",), 2 * t * (qw * 6 + 4 * kvw * 4)),
        name="attn_ctx",
    )(p, p, p, sink)


def _band_bias():
    Q = QBLOCK
    r = np.arange(Q)[:, None]
    c = np.arange(3 * Q)[None, :]
    band = (c >= r) & (c <= r + 2 * Q)
    out = []
    for kind in range(3):
        ok = band.copy()
        if kind == 0:
            ok &= c >= Q
        if kind == 2:
            ok &= c < 2 * Q
        out.append(np.where(ok, 0.0, NEG).astype(np.float32))
    return jnp.asarray(np.stack(out))


def _half_variants(x):
    low = lax.broadcasted_iota(jnp.int32, x.shape, 1) < ATT_HEAD_DIM
    sw = pltpu.roll(x, ATT_HEAD_DIM, 1)
    zero = jnp.zeros_like(x)
    return [jnp.where(low, a, b).astype(BF16) for a, b in ((x, zero), (zero, sw), (sw, zero), (zero, x))]


def _attn_lat_kernel(t, q_ref, k_ref, v_ref, kc_ref, vc_ref, cq_ref, saq_ref, sbq_ref,
                     ck_ref, sak_ref, sbk_ref, bias_ref, sink_ref, o_ref, kp_scr, vp_scr, kc_scr, vc_scr):
    n = pl.program_id(1)
    Q = QBLOCK
    hd = ATT_HEAD_DIM
    quarter = hd // 4
    nvar = 2 * ATT_KV_HEADS

    @pl.when(n == 0)
    def _():
        zero = jnp.zeros((Q, LANE), BF16)
        for vi in range(nvar):
            kp_scr[vi, 0:Q, :] = zero
            vp_scr[vi, 0:Q, :] = zero
            kp_scr[vi, Q + t:2 * Q + t, :] = zero
            vp_scr[vi, Q + t:2 * Q + t, :] = zero

        def fill(c, _):
            rows = _chunk_rows(c, Q)
            dst = pl.ds(pl.multiple_of(c * Q + Q, Q), Q)
            kk = _rope(k_ref[rows, :], ck_ref[rows, :], sak_ref[rows, :], sbk_ref[rows, :], quarter)
            for vi, (kv, vv) in enumerate(zip(_half_variants(kk), _half_variants(v_ref[rows, :]))):
                kp_scr[vi, dst, :] = kv
                vp_scr[vi, dst, :] = vv
            return 0

        lax.fori_loop(0, t // Q, fill, 0)
        for vi, (kv, vv) in enumerate(zip(_half_variants(kc_ref[...]), _half_variants(vc_ref[...]))):
            kc_scr[vi] = kv
            vc_scr[vi] = vv

    band = pl.ds(pl.multiple_of(n * Q, Q), 3 * Q)
    cq, saq, sbq = cq_ref[...], saq_ref[...], sbq_ref[...]
    bias = bias_ref[...]
    scale = hd ** -0.5
    heads_per_chunk = LANE // hd
    for c in range(ATT_HEADS // heads_per_chunk):
        sl = slice(c * LANE, (c + 1) * LANE)
        qc = (_rope(q_ref[:, sl], cq, saq, sbq, quarter) * scale).astype(BF16)
        acc = None
        for half in range(heads_per_chunk):
            hx = c * heads_per_chunk + half
            vi = 2 * (hx // ATT_GROUPS) + half
            s_loc = _dot_nt(qc, kp_scr[vi, band, :]) + bias
            s_ctx = _dot_nt(qc, kc_scr[vi])
            sink = jnp.full((1, 1), sink_ref[hx], F32)
            o = _softmax_sink_pv([s_loc, s_ctx], [vp_scr[vi, band, :], vc_scr[vi]], sink)
            acc = o if acc is None else acc + o
        o_ref[:, sl] = acc.astype(BF16)


def _attn_lat(p, nb, t, kc, vc, layer, tabs, sink):
    assert (ATT_HEAD_DIM ** -0.5) == 2.0 ** round(np.log2(ATT_HEAD_DIM ** -0.5))
    qw = ATT_HEADS * ATT_HEAD_DIM
    kvw = ATT_KV_HEADS * ATT_HEAD_DIM
    assert kvw == LANE
    nq = t // QBLOCK
    past = kc.shape[2]
    nvar = 2 * ATT_KV_HEADS
    cos, sa, sb = tabs
    bias = _band_bias()
    tq = pl.BlockSpec((QBLOCK, LANE), lambda b, n: (n, 0))
    tk = pl.BlockSpec((t, LANE), lambda b, n: (0, 0))
    return pl.pallas_call(
        functools.partial(_attn_lat_kernel, t),
        grid=(nb, nq),
        in_specs=[
            pl.BlockSpec((QBLOCK, qw), lambda b, n: (b * nq + n, COL_B_Q * LANE // qw)),
            pl.BlockSpec((t, kvw), lambda b, n: (b, COL_B_K)),
            pl.BlockSpec((t, kvw), lambda b, n: (b, COL_B_V)),
            pl.BlockSpec((None, None, past, kvw), lambda b, n: (b, layer, 0, 0)),
            pl.BlockSpec((None, None, past, kvw), lambda b, n: (b, layer, 0, 0)),
            tq, tq, tq, tk, tk, tk,
            pl.BlockSpec((None,) + bias.shape[1:],
                         lambda b, n: (jnp.where(n == 0, 0, jnp.where(n == nq - 1, 2, 1)), 0, 0)),
            pl.BlockSpec(memory_space=pltpu.SMEM),
        ],
        out_specs=pl.BlockSpec((QBLOCK, qw), lambda b, n: (b * nq + n, 0)),
        out_shape=jax.ShapeDtypeStruct((nb * t, qw), BF16),
        scratch_shapes=[pltpu.VMEM((nvar, t + 2 * QBLOCK, LANE), BF16), pltpu.VMEM((nvar, t + 2 * QBLOCK, LANE), BF16),
                        pltpu.VMEM((nvar, past, LANE), BF16), pltpu.VMEM((nvar, past, LANE), BF16)],
        compiler_params=_params(("parallel", "arbitrary"),
                                2 * 5 * t * LANE * 4 + 2 * nvar * (t + 2 * QBLOCK + past) * LANE * 2 + 2 * MIB),
        name="attn_lat",
    )(p, p, p, kc, vc, cos, sa, sb, cos, sa, sb, bias, sink)


def _conv_kernel(t, cb_ref, cc_ref, cx_ref, w_ref, o_ref):
    u = cc_ref[...] * cx_ref[...]
    tok = lax.broadcasted_iota(jnp.int32, u.shape, 0)
    prev = jnp.where(tok == 0, 0.0, pltpu.roll(u, 1, 0))
    nxt = jnp.where(tok == t - 1, 0.0, pltpu.roll(u, t - 1, 0))
    o_ref[...] = (cb_ref[...] * (w_ref[0:1, :] * prev + w_ref[1:2, :] * u + w_ref[2:3, :] * nxt)).astype(BF16)


def _conv(p, nb, t, row_off, w):
    assert row_off % t == 0
    ob = row_off // t
    nblk = w.shape[1] // LANE

    def col(cb):
        return pl.BlockSpec((t, LANE), lambda b, j: (ob + b, cb + j))

    return pl.pallas_call(
        functools.partial(_conv_kernel, t),
        grid=(nb, nblk),
        in_specs=[col(COL_C_B), col(COL_C_C), col(COL_C_X),
                  pl.BlockSpec((w.shape[0], LANE), lambda b, j: (0, j))],
        out_specs=pl.BlockSpec((t, LANE), lambda b, j: (b, j)),
        out_shape=jax.ShapeDtypeStruct((nb * t, w.shape[1]), BF16),
        compiler_params=_params(("parallel", "parallel"), 2 * 4 * t * LANE * 4),
        name="conv",
    )(p, p, p, w)


GATE_SRC = 4 * MIX_HEADS * MIX_DIM
N_GATES = 4 * MIX_HEADS


def _gate_lanes(x):
    g = x.reshape(x.shape[:-1] + (4, MIX_HEADS))
    g = jnp.swapaxes(g, -1, -2).reshape(x.shape[:-1] + (N_GATES,))
    return jnp.pad(g, [(0, 0)] * (x.ndim - 1) + [(0, LANE - N_GATES)])


def _stage_mixer_proj(w_in):
    head = w_in[..., 0:GATE_SRC]
    tail = w_in[..., GATE_SRC + N_GATES:]
    gates = _gate_lanes(w_in[..., GATE_SRC:GATE_SRC + N_GATES])
    pad = jnp.zeros(w_in.shape[:-1] + (PROJ_P - head.shape[-1] - tail.shape[-1] - LANE,), w_in.dtype)
    assert head.shape[-1] + tail.shape[-1] == COL_GATES * LANE
    return jnp.concatenate([head, tail, gates, pad], axis=-1).astype(BF16)


def kernel(x_prompt, x_sample, cache_attn_k, cache_attn_v, state_mlstm_C, state_mlstm_n, state_mlstm_m,
           state_ret_S, c, c_ctx, w_mod, b_mod, norm_g, ffn1_w_in, ffn1_w_out, ffn2_w_in, ffn2_w_out,
           w_in, mlstm_gate_b, mlstm_norm_g, attn_sink, conv_w, ret_log_decay, ret_norm_g,
           w_branch, w_gate, w_out):
    bp, tp, d = x_prompt.shape
    bs, ts, _ = x_sample.shape
    depth = w_mod.shape[0]
    past = cache_attn_k.shape[2]
    kvw = ATT_KV_HEADS * ATT_HEAD_DIM
    fits = [t for t in (1024, 512, 256) if ts % t == 0 and (bp * tp) % t == 0]
    rows = _Rows(bs, ts, bp, tp, fits[0])
    rows_ffn = _Rows(bs, ts, bp, tp, fits[min(1, len(fits) - 1)])

    nbp = -(-(1 + bs) // 8) * 8
    cpad = jnp.concatenate([c_ctx[None], c, jnp.zeros((nbp - 1 - bs, d), F32)], axis=0)
    mod = _modulation(cpad, w_mod, b_mod).reshape(depth, nbp, 3, 3, d)

    w_p = _stage_mixer_proj(w_in)
    ffn1_in, ffn1_out = ffn1_w_in.astype(BF16), ffn1_w_out.astype(BF16)
    ffn2_in, ffn2_out = ffn2_w_in.astype(BF16), ffn2_w_out.astype(BF16)
    wg_b, wb_b, wo_b = w_gate.astype(BF16), w_branch.astype(BF16), w_out.astype(BF16)

    gate_bias = _gate_lanes(mlstm_gate_b.reshape(depth, 1, N_GATES))
    tabs_att = _rope_tables(ts, ATT_HEAD_DIM)
    tabs_ret = _rope_tables(ts, MIX_DIM)

    x = jnp.concatenate([x_sample.reshape(bs * ts, d), x_prompt.reshape(bp * tp, d)], axis=0)
    ms = bs * ts
    kc_all = cache_attn_k.reshape(bs, depth, past, kvw)
    vc_all = cache_attn_v.reshape(bs, depth, past, kvw)
    ks_, vs_, cs_, ns_, mms_, ss_ = [], [], [], [], [], []
    for l in range(depth):
        g = norm_g[l]
        x = _ffn(rows_ffn, x, mod[l, :, 0], g[0:2], ffn1_in, ffn1_out, l)

        p, h = _inproj(rows, x, mod[l, :, 1], g[2:4], w_p, l)
        a_gain = mlstm_norm_g[l][None]
        d_gain = ret_norm_g[l][None]
        lat_a = _mlstm(p, bs, ts, 0, gate_bias[l], a_gain, (state_mlstm_C, state_mlstm_n, state_mlstm_m, l))[0]
        lat_b = _attn_lat(p, bs, ts, kc_all, vc_all, l, tabs_att, attn_sink[l])
        lat_c = _conv(p, bs, ts, 0, conv_w[l])
        lat_d = _retention(p, bs, ts, 0, ret_log_decay[l], d_gain, (tabs_ret, state_ret_S, l))[0]
        ctx_a, c_fin, n_fin, m_fin = _mlstm(p, bp, tp, ms, gate_bias[l], a_gain, None)
        ctx_b, k_new, v_new = _attn_ctx(p, bp, tp, ms, attn_sink[l])
        ctx_c = _conv(p, bp, tp, ms, conv_w[l])
        ctx_d, s_fin = _retention(p, bp, tp, ms, ret_log_decay[l], d_gain, None)
        y = _merge(rows, h, (lat_a, lat_b, lat_c, lat_d), (ctx_a, ctx_b, ctx_c, ctx_d), wg_b, wb_b, l)
        x = _outproj(rows_ffn, x, mod[l, :, 1], g[2:4], y, wo_b, l)

        x = _ffn(rows_ffn, x, mod[l, :, 2], g[4:6], ffn2_in, ffn2_out, l)

        ks_.append(k_new.reshape(bp, tp, ATT_KV_HEADS, ATT_HEAD_DIM))
        vs_.append(v_new.reshape(bp, tp, ATT_KV_HEADS, ATT_HEAD_DIM))
        cs_.append(c_fin)
        ns_.append(n_fin)
        mms_.append(m_fin[:, :, 0].reshape(bp, 2, MIX_HEADS))
        ss_.append(s_fin)

    y_sample = x[:ms].reshape(bs, ts, d)
    y_prompt = x[ms:].reshape(bp, tp, d)
    return (y_prompt, y_sample, jnp.stack(ks_, axis=1), jnp.stack(vs_, axis=1), jnp.stack(cs_, axis=1),
            jnp.stack(ns_, axis=1), jnp.stack(mms_, axis=1), jnp.stack(ss_, axis=1))
```

```python
import functools

import numpy as np
import jax
import jax.numpy as jnp
from jax import lax
from jax.experimental import pallas as pl
from jax.experimental.pallas import tpu as pltpu

F32 = jnp.float32
BF16 = jnp.bfloat16

N_MOD = 9
N_BRANCH = 4
BRANCH_W = 512
MIX_HEADS = 4
MIX_DIM = 128
ATT_HEADS = 8
ATT_KV_HEADS = 2
ATT_HEAD_DIM = 64
ATT_GROUPS = ATT_HEADS // ATT_KV_HEADS
QBLOCK = 128
CHUNK = 128
GRID_W = 64
ROPE_BASE = 10000.0
EPS = 1e-6
NEG = -1e30

LANE = 128
COL_A_Q, COL_A_K, COL_A_V, COL_A_O = 0, 4, 8, 12
COL_B_Q, COL_B_K, COL_B_V = 16, 20, 21
COL_C_B, COL_C_C, COL_C_X = 22, 26, 30
COL_D_Q, COL_D_K, COL_D_V, COL_D_G = 34, 38, 42, 46
COL_GATES = 50
PROJ_BLOCKS = 52
PROJ_P = PROJ_BLOCKS * LANE

MIB = 1024 * 1024
VMEM_CAP = 44 * MIB
VMEM_TEMPS = 12 * MIB
SINGLE = pl.Buffered(1)


def _params(sem, buffers):
    return pltpu.CompilerParams(dimension_semantics=sem,
                                vmem_limit_bytes=int(min(buffers + VMEM_TEMPS, VMEM_CAP)))


def _pick_tile(n, pref):
    if n <= pref:
        return n
    t = (pref // LANE) * LANE
    while t >= LANE:
        if n % t == 0:
            return t
        t -= LANE
    raise ValueError(f"no lane-aligned tile divides {n}")


def _dot(a, b):
    return jnp.dot(a, b, preferred_element_type=F32)


def _dot_nt(a, b):
    return lax.dot_general(a, b, (((1,), (1,)), ((), ())), preferred_element_type=F32)


def _dot_tn(a, b):
    return lax.dot_general(a, b, (((0,), (0,)), ((), ())), preferred_element_type=F32)


def _sigmoid(x):
    return 1.0 / (1.0 + jnp.exp(-x))


def _rms(xf, g):
    ms = jnp.mean(xf * xf, axis=-1, keepdims=True)
    return xf * lax.rsqrt(ms + EPS) * g


def _modulated(x_ref, mod_ref, g_ref):
    return _rms(x_ref[...], g_ref[0:1, :]) * (1.0 + mod_ref[1:2, :]) + mod_ref[0:1, :]


def _head_norm(x, g):
    mu = jnp.mean(x, axis=-1, keepdims=True)
    xc = x - mu
    var = jnp.mean(xc * xc, axis=-1, keepdims=True)
    return xc * lax.rsqrt(var + EPS) * g


def _chunk_rows(c, n=CHUNK):
    return pl.ds(pl.multiple_of(c * n, n), n)


def _mod_kernel(c_ref, w_ref, b_ref, o_ref):
    c = c_ref[...]
    s = (c * _sigmoid(c)).astype(BF16)
    o_ref[...] = _dot(s, w_ref[...].astype(BF16)) + b_ref[...]


def _modulation(cpad, w_mod, b_mod):
    depth, d, n = w_mod.shape
    nbp = cpad.shape[0]
    tn = _pick_tile(n, 1024)
    return pl.pallas_call(
        _mod_kernel,
        grid=(depth, n // tn),
        in_specs=[
            pl.BlockSpec((nbp, d), lambda l, j: (0, 0)),
            pl.BlockSpec((None, d, tn), lambda l, j: (l, 0, j)),
            pl.BlockSpec((None, 1, tn), lambda l, j: (l, 0, j)),
        ],
        out_specs=pl.BlockSpec((None, nbp, tn), lambda l, j: (l, 0, j)),
        out_shape=jax.ShapeDtypeStruct((depth, nbp, n), F32),
        compiler_params=_params(("parallel", "parallel"), 2 * d * tn * 4 + MIB),
        name="modulation",
    )(cpad, w_mod, b_mod.reshape(depth, 1, n))


class _Rows:
    def __init__(self, bs, ts, bp, tp, tm):
        self.bs, self.ts, self.bp, self.tp = bs, ts, bp, tp
        self.ms, self.mp = bs * ts, bp * tp
        self.m = self.ms + self.mp
        assert ts % tm == 0 and self.mp % tm == 0 and self.ms % tp == 0
        self.tm = tm
        self.lat_tiles = self.ms // tm
        self.tiles_per_seq = ts // tm

    def mod_index(self, i):
        return jnp.where(i < self.lat_tiles, 1 + i // self.tiles_per_seq, 0)

    def x_spec(self, d, mode=None):
        return pl.BlockSpec((self.tm, d), lambda i, *_: (i, 0), pipeline_mode=mode)

    def mod_spec(self, d):
        return pl.BlockSpec((None, 3, d), lambda i, *_: (self.mod_index(i), 0, 0))


def _layer_spec(block, index_map, l, mode=None):
    return pl.BlockSpec((None,) + tuple(block), lambda *ids: (l,) + tuple(index_map(*ids)), pipeline_mode=mode)


def _ffn_kernel(nf, x_ref, mod_ref, g_ref, wg_ref, wu_ref, wo_ref, o_ref, h_scr, acc_scr):
    f = pl.program_id(1)

    def hidden_slab(h):
        gate = _dot(h, wg_ref[...])
        up = _dot(h, wu_ref[...])
        a = (gate * _sigmoid(gate)) * up
        return _dot(a.astype(BF16), wo_ref[...])

    def pre():
        h = _modulated(x_ref, mod_ref, g_ref).astype(BF16)
        h_scr[...] = h
        return h

    def post(z):
        o_ref[...] = x_ref[...] + (0.5 * mod_ref[2:3, :]) * _rms(z, g_ref[1:2, :])

    if nf == 1:
        post(hidden_slab(pre()))
        return

    @pl.when(f == 0)
    def _():
        acc_scr[...] = hidden_slab(pre())

    @pl.when((f > 0) & (f < nf - 1))
    def _():
        acc_scr[...] += hidden_slab(h_scr[...])

    @pl.when(f == nf - 1)
    def _():
        post(acc_scr[...] + hidden_slab(h_scr[...]))


def _ffn(rows, x, mod_s, g2, w_in, w_out, l):
    m, d = x.shape
    dff = w_out.shape[1]
    tm = rows.tm
    tf = _pick_tile(dff, 512)
    nf = dff // tf
    buffers = tm * d * (2 * 4 + 2 * 4 + 4 + 2) + 3 * 2 * d * tf * 2
    return pl.pallas_call(
        functools.partial(_ffn_kernel, nf),
        grid=(m // tm, nf),
        in_specs=[
            rows.x_spec(d),
            rows.mod_spec(d),
            pl.BlockSpec((2, d), lambda i, f: (0, 0)),
            _layer_spec((d, tf), lambda i, f: (0, f), l),
            _layer_spec((d, tf), lambda i, f: (0, f + nf), l),
            _layer_spec((tf, d), lambda i, f: (f, 0), l),
        ],
        out_specs=pl.BlockSpec((tm, d), lambda i, f: (i, 0)),
        out_shape=jax.ShapeDtypeStruct((m, d), F32),
        scratch_shapes=[pltpu.VMEM((tm, d), BF16), pltpu.VMEM((tm, d), F32)],
        compiler_params=_params(("parallel", "arbitrary"), buffers),
        name="ffn",
    )(x, mod_s, g2, w_in, w_in, w_out)


def _inproj_kernel(x_ref, mod_ref, g_ref, w_ref, p_ref, h_ref):
    @pl.when(pl.program_id(1) == 0)
    def _():
        h = _modulated(x_ref, mod_ref, g_ref).astype(BF16)
        h_ref[...] = h
        p_ref[...] = _dot(h, w_ref[...])

    @pl.when(pl.program_id(1) > 0)
    def _():
        p_ref[...] = _dot(h_ref[...], w_ref[...])


def _inproj(rows, x, mod_s, g2, w_p, l):
    m, d = x.shape
    n = w_p.shape[2]
    tm = rows.tm
    tn = _pick_tile(n, 512)
    return pl.pallas_call(
        _inproj_kernel,
        grid=(m // tm, n // tn),
        in_specs=[
            rows.x_spec(d, SINGLE),
            rows.mod_spec(d),
            pl.BlockSpec((2, d), lambda i, j: (0, 0)),
            _layer_spec((d, tn), lambda i, j: (0, j), l),
        ],
        out_specs=[pl.BlockSpec((tm, tn), lambda i, j: (i, j)),
                   pl.BlockSpec((tm, d), lambda i, j: (i, 0))],
        out_shape=[jax.ShapeDtypeStruct((m, n), F32), jax.ShapeDtypeStruct((m, d), BF16)],
        compiler_params=_params(("parallel", "arbitrary"),
                                tm * d * (4 + 2 * 2) + 2 * d * tn * 2 + 2 * tm * tn * 4),
        name="inproj",
    )(x, mod_s, g2, w_p)


def _merge_kernel(lat_tiles, h_ref, *refs):
    lat_refs, ctx_refs = refs[0:N_BRANCH], refs[N_BRANCH:2 * N_BRANCH]
    wg_ref, wb_ref, y_ref = refs[2 * N_BRANCH:]
    h = h_ref[...]
    is_lat = pl.program_id(0) < lat_tiles
    y = None
    for i in range(N_BRANCH):
        o = jnp.where(is_lat, lat_refs[i][...], ctx_refs[i][...])
        term = _sigmoid(_dot(h, wg_ref[i])) * _dot(o, wb_ref[i])
        y = term if y is None else y + term
    y_ref[...] = y.astype(BF16)


def _merge(rows, h, lat, ctx, w_gate, w_branch, l):
    m, d = h.shape
    tm = rows.tm
    tn = _pick_tile(d, 256)
    bw = w_branch.shape[2]
    last_lat = rows.lat_tiles - 1
    lat_spec = pl.BlockSpec((tm, bw), lambda i, j: (jnp.minimum(i, last_lat), 0), pipeline_mode=SINGLE)
    ctx_spec = pl.BlockSpec((tm, bw), lambda i, j: (jnp.maximum(i - rows.lat_tiles, 0), 0), pipeline_mode=SINGLE)
    return pl.pallas_call(
        functools.partial(_merge_kernel, rows.lat_tiles),
        grid=(m // tm, d // tn),
        in_specs=[
            rows.x_spec(d),
            *([lat_spec] * N_BRANCH), *([ctx_spec] * N_BRANCH),
            _layer_spec((N_BRANCH, d, tn), lambda i, j: (0, 0, j), l),
            _layer_spec((N_BRANCH, bw, tn), lambda i, j: (0, 0, j), l),
        ],
        out_specs=pl.BlockSpec((tm, tn), lambda i, j: (i, j)),
        out_shape=jax.ShapeDtypeStruct((m, d), BF16),
        compiler_params=_params(("parallel", "arbitrary"),
                                2 * tm * d * 2 + 2 * N_BRANCH * tm * bw * 2
                                + 2 * N_BRANCH * (d + bw) * tn * 2 + 2 * tm * tn * 2),
        name="merge",
    )(h, *lat, *ctx, w_gate, w_branch)


def _outproj_kernel(x_ref, mod_ref, g_ref, y_ref, w_ref, o_ref):
    z = _dot(y_ref[...], w_ref[...])
    o_ref[...] = x_ref[...] + mod_ref[2:3, :] * _rms(z, g_ref[1:2, :])


def _outproj(rows, x, mod_s, g2, y, w_out, l):
    m, d = x.shape
    tm = rows.tm
    return pl.pallas_call(
        _outproj_kernel,
        grid=(m // tm,),
        in_specs=[
            rows.x_spec(d),
            rows.mod_spec(d),
            pl.BlockSpec((2, d), lambda i: (0, 0)),
            rows.x_spec(d),
            _layer_spec((d, d), lambda i: (0, 0), l, SINGLE),
        ],
        out_specs=pl.BlockSpec((tm, d), lambda i: (i, 0)),
        out_shape=jax.ShapeDtypeStruct((m, d), F32),
        compiler_params=_params(("parallel",), tm * d * (2 * 4 + 2 * 4 + 4 + 2 * 2) + d * d * 2),
        name="outproj",
    )(x, mod_s, g2, y, w_out)


def _rope_tables(t, d):
    rows = t // GRID_W
    row = jnp.repeat(jnp.arange(rows), GRID_W).astype(F32)
    col = jnp.tile(jnp.arange(GRID_W), rows).astype(F32)
    quarter = d // 4
    inv = jnp.power(ROPE_BASE, -jnp.arange(quarter, dtype=F32) / quarter)
    ar = row[:, None] * inv
    ac = col[:, None] * inv
    ang = jnp.concatenate([ar, ar, ac, ac], axis=-1)
    cos, sin = jnp.cos(ang), jnp.sin(ang)
    first = (jnp.arange(d) % (2 * quarter)) < quarter
    sa = jnp.where(first, -sin, 0.0)
    sb = jnp.where(first, 0.0, sin)
    reps = LANE // d
    return tuple(jnp.tile(a, (1, reps)) for a in (cos, sa, sb))


def _rope(x, cos, sa, sb, quarter):
    n = x.shape[-1]
    ahead = pltpu.roll(x, n - quarter, 1)
    behind = pltpu.roll(x, quarter, 1)
    return x * cos + ahead * sa + behind * sb


def _log_sigmoid(x):
    return jnp.minimum(x, 0.0) - jnp.log1p(jnp.exp(-jnp.abs(x)))


def _split_bf16(x):
    hi = x.astype(BF16)
    return hi, (x - hi.astype(F32)).astype(BF16)


def _mlstm_kernel(t, layer, emit_state, *refs):
    has_state = layer is not None
    it = iter(refs)
    q_ref, k_ref, v_ref, og_ref, gt_ref, bias_ref, gain_ref = (next(it) for _ in range(7))
    if has_state:
        c0_ref, n0_ref, m0_ref = next(it), next(it), next(it)
    o_ref = next(it)
    if emit_state:
        cout_ref, nout_ref, mout_ref = next(it), next(it), next(it)
    h_scr, s_scr = next(it), next(it)

    L, D, H = CHUNK, MIX_DIM, MIX_HEADS
    nc = t // L
    r_i = lax.broadcasted_iota(jnp.int32, (L, L), 0)
    c_i = lax.broadcasted_iota(jnp.int32, (L, L), 1)
    masks = (r_i >= c_i, r_i <= c_i)
    cum = tuple(mk.astype(BF16) for mk in masks)
    ones = jnp.ones((L, D), BF16)
    sel = [(r_i == g).astype(BF16) for g in range(2 * H)]
    bias = bias_ref[...]
    scale = D ** -0.5

    ms0 = []
    for d in range(2):
        for h in range(H):
            if has_state:
                s_scr[d, h, :, 0:D] = c0_ref[d, h]
                s_scr[d, h, :, D:2 * D] = jnp.broadcast_to(n0_ref[d, h:h + 1, :], (D, D)).T
                ms0.append(jnp.full((1, D), m0_ref[pl.program_id(0), layer, d, h], F32))
            else:
                s_scr[d, h] = jnp.zeros((D, 2 * D), F32)
                ms0.append(jnp.zeros((1, D), F32))

    def zero_rows(c, _):
        h_scr[_chunk_rows(c), :] = jnp.zeros((L, H * D), F32)
        return 0

    lax.fori_loop(0, nc, zero_rows, 0)

    def chunk_dir(c, d, ms):
        rows = _chunk_rows(c)
        g = gt_ref[rows, :] + bias
        lf_hi, lf_lo = _split_bf16(_log_sigmoid(g[:, LANE:2 * LANE]))
        f_all = _dot(cum[d], lf_hi) + _dot(cum[d], lf_lo)
        b_all = g[:, 0:LANE] - f_all
        b_t = b_all.T
        f_hi, f_lo = _split_bf16(f_all)
        b_hi, b_lo = _split_bf16(b_all)
        out = []
        for h in range(H):
            j = d * H + h
            cols = slice(h * D, (h + 1) * D)
            q = q_ref[rows, cols]
            k = k_ref[rows, cols] * scale
            qb = q.astype(BF16)
            v_aug = jnp.concatenate([v_ref[rows, cols].astype(BF16), ones], axis=1)
            f_rep = _dot(f_hi, sel[j]) + _dot(f_lo, sel[j])
            b_rep = _dot(b_hi, sel[j]) + _dot(b_lo, sel[j])
            brow = b_t[j:j + 1, :]
            f_last = f_rep[L - 1:L, :] if d == 0 else f_rep[0:1, :]
            m = ms[j]
            lw = jnp.where(masks[d], f_rep + brow, NEG)
            inter = f_rep + m
            m_tok = jnp.maximum(jnp.max(lw, axis=-1, keepdims=True), inter)
            a = _dot_nt(qb, k.astype(BF16)) * jnp.exp(lw - m_tok)
            w_inter = jnp.exp(inter - m_tok)
            s_old = s_scr[d, h]
            pv = _dot(a.astype(BF16), v_aug)
            qs = _dot(qb, s_old.astype(BF16))
            num = pv[:, 0:D] + w_inter * qs[:, 0:D]
            den = pv[:, D:2 * D] + w_inter * qs[:, D:2 * D]
            hh = num * (1.0 / jnp.maximum(jnp.abs(den), jnp.exp(-m_tok)))
            h_scr[rows, cols] += hh
            bmax = jnp.max(b_rep, axis=0, keepdims=True)
            a_end = f_last + bmax
            kw = k * jnp.exp(b_rep - bmax)
            u_aug = _dot_tn(kw.astype(BF16), v_aug)
            m_new = jnp.maximum(f_last + m, a_end)
            d_old = jnp.exp(f_last + m - m_new)
            d_new = jnp.exp(a_end - m_new)
            for half in (slice(0, D), slice(D, 2 * D)):
                s_scr[d, h, :, half] = d_old * s_old[:, half] + d_new * u_aug[:, half]
            out.append(m_new)
        return out

    def body(c, ms):
        mf = chunk_dir(c, 0, ms)
        mb = chunk_dir(nc - 1 - c, 1, ms)
        return tuple(mf + mb)

    ms = lax.fori_loop(0, nc, body, tuple(ms0))

    gain = gain_ref[...]

    def finish(c, _):
        rows = _chunk_rows(c)
        for h in range(H):
            cols = slice(h * D, (h + 1) * D)
            y = _head_norm(h_scr[rows, cols], gain[:, cols])
            o_ref[rows, cols] = (_sigmoid(og_ref[rows, cols]) * y).astype(BF16)
        return 0

    lax.fori_loop(0, nc, finish, 0)

    if emit_state:
        for d in range(2):
            for h in range(H):
                cout_ref[d, h] = s_scr[d, h, :, 0:D]
                nout_ref[d, h:h + 1, :] = s_scr[d, h, :, D:2 * D].T[0:1, :]
                mout_ref[d * H + h:d * H + h + 1, :] = ms[d * H + h]


def _mlstm(p, nb, t, row_off, bias, gain, state):
    assert row_off % t == 0
    ob = row_off // t
    has_state = state is not None
    emit_state = not has_state
    hw = MIX_HEADS * MIX_DIM
    st_c = pl.BlockSpec((None, 2, MIX_HEADS, MIX_DIM, MIX_DIM), lambda b: (b, 0, 0, 0, 0))
    st_n = pl.BlockSpec((None, 2, MIX_HEADS, MIX_DIM), lambda b: (b, 0, 0, 0))

    def col(cb):
        return pl.BlockSpec((t, hw), lambda b: (ob + b, cb * LANE // hw), pipeline_mode=mode)

    mode = SINGLE if t * hw * 4 >= 2 * MIB else None
    nbuf = 1 if mode is SINGLE else 2
    buffers = (nbuf * (4 * t * hw + t * 2 * LANE) * 4 + 2 * t * hw * 2 + t * hw * 4
               + 6 * 2 * MIX_HEADS * MIX_DIM * MIX_DIM * 4)
    in_specs = [col(COL_A_Q), col(COL_A_K), col(COL_A_V), col(COL_A_O),
                pl.BlockSpec((t, 2 * LANE), lambda b: (ob + b, COL_GATES // 2), pipeline_mode=mode),
                pl.BlockSpec((1, 2 * LANE), lambda b: (0, 0)),
                pl.BlockSpec((1, hw), lambda b: (0, 0))]
    args = [p, p, p, p, p, bias, gain]
    layer = None
    if has_state:
        layer = state[3]
        in_specs += [pl.BlockSpec((None, None, 2, MIX_HEADS, MIX_DIM, MIX_DIM), lambda b: (b, layer, 0, 0, 0, 0)),
                     pl.BlockSpec((None, None, 2, MIX_HEADS, MIX_DIM), lambda b: (b, layer, 0, 0, 0)),
                     pl.BlockSpec(memory_space=pltpu.SMEM)]
        args += list(state[0:3])
    out_shape = [jax.ShapeDtypeStruct((nb * t, hw), BF16)]
    out_specs = [pl.BlockSpec((t, hw), lambda b: (b, 0))]
    if emit_state:
        out_shape += [jax.ShapeDtypeStruct((nb, 2, MIX_HEADS, MIX_DIM, MIX_DIM), F32),
                      jax.ShapeDtypeStruct((nb, 2, MIX_HEADS, MIX_DIM), F32),
                      jax.ShapeDtypeStruct((nb, 2 * MIX_HEADS, MIX_DIM), F32)]
        out_specs += [st_c, st_n, pl.BlockSpec((None, 2 * MIX_HEADS, MIX_DIM), lambda b: (b, 0, 0))]
    return pl.pallas_call(
        functools.partial(_mlstm_kernel, t, layer, emit_state),
        grid=(nb,),
        in_specs=in_specs,
        out_specs=out_specs,
        out_shape=out_shape,
        scratch_shapes=[pltpu.VMEM((t, hw), F32),
                        pltpu.VMEM((2, MIX_HEADS, MIX_DIM, 2 * MIX_DIM), F32)],
        compiler_params=_params(("parallel",), buffers),
        name="mlstm_lat" if has_state else "mlstm_ctx",
    )(*args)


def _ret_kernel(t, has_state, emit_state, *refs):
    it = iter(refs)
    q_ref, k_ref, v_ref, dg_ref, lg_ref, gain_ref = (next(it) for _ in range(6))
    if has_state:
        cos_ref, sa_ref, sb_ref, s0_ref = (next(it) for _ in range(4))
    o_ref = next(it)
    if emit_state:
        sout_ref = next(it)
    q_scr, u_scr, o_scr = (next(it) for _ in range(3))

    L, D = CHUNK, MIX_DIM
    nc = t // L
    hi = pl.program_id(1)
    lgs = (lg_ref[0, hi], lg_ref[1, hi])
    diff = (lax.broadcasted_iota(jnp.int32, (L, L), 0) - lax.broadcasted_iota(jnp.int32, (L, L), 1)).astype(F32)
    tcol = lax.broadcasted_iota(jnp.int32, (L, 1), 0).astype(F32)
    dm = (jnp.where(diff >= 0.0, jnp.exp(jnp.maximum(diff, 0.0) * lgs[0]), 0.0)
          + jnp.where(diff <= 0.0, jnp.exp(jnp.maximum(-diff, 0.0) * lgs[1]), 0.0))
    w_end = (jnp.exp((L - 1.0 - tcol) * lgs[0]), jnp.exp(tcol * lgs[1]))
    w_cross = (jnp.exp((tcol + 1.0) * lgs[0]), jnp.exp((L - tcol) * lgs[1]))
    decay = tuple(jnp.exp(jnp.full((1, 1), float(L), F32) * lgs[d]) for d in range(2))
    scale = D ** -0.5

    def local(c, _):
        rows = _chunk_rows(c)
        q = q_ref[rows, :]
        k = k_ref[rows, :] * scale
        if has_state:
            cos, sa, sb = cos_ref[rows, :], sa_ref[rows, :], sb_ref[rows, :]
            q = _rope(q, cos, sa, sb, D // 4)
            k = _rope(k, cos, sa, sb, D // 4)
        qb = q.astype(BF16)
        vb = v_ref[rows, :].astype(BF16)
        q_scr[rows, :] = qb
        o_scr[rows, :] = _dot((_dot_nt(qb, k.astype(BF16)) * dm).astype(BF16), vb)
        kw = jnp.concatenate([(k * w_end[0]).astype(BF16), (k * w_end[1]).astype(BF16)], axis=1)
        u_scr[c] = _dot_tn(kw, vb)
        return 0

    lax.fori_loop(0, nc, local, 0, unroll=2)

    def scan(c, carry):
        sf, sb = carry
        rows = _chunk_rows(c)
        o_scr[rows, :] += _dot(q_scr[rows, :], sf.astype(BF16)) * w_cross[0]
        sf = decay[0] * sf + u_scr[c, 0:D, :]
        cb = nc - 1 - c
        rows_b = _chunk_rows(cb)
        o_scr[rows_b, :] += _dot(q_scr[rows_b, :], sb.astype(BF16)) * w_cross[1]
        sb = decay[1] * sb + u_scr[cb, D:2 * D, :]
        return sf, sb

    if has_state:
        init = (s0_ref[0], s0_ref[1])
    else:
        init = (jnp.zeros((D, D), F32), jnp.zeros((D, D), F32))
    sf, sb = lax.fori_loop(0, nc, scan, init, unroll=2)

    gain = gain_ref[...]

    def finish(c, _):
        rows = _chunk_rows(c)
        y = _head_norm(o_scr[rows, :], gain)
        dg = dg_ref[rows, :]
        o_ref[rows, :] = ((dg * _sigmoid(dg)) * y).astype(BF16)
        return 0

    lax.fori_loop(0, nc, finish, 0, unroll=2)

    if emit_state:
        sout_ref[0] = sf
        sout_ref[1] = sb


def _retention(p, nb, t, row_off, log_decay, gain, state):
    assert row_off % t == 0
    ob = row_off // t
    has_state = state is not None
    emit_state = not has_state

    def col(cb):
        return pl.BlockSpec((t, LANE), lambda b, h: (ob + b, cb + h))

    in_specs = [col(COL_D_Q), col(COL_D_K), col(COL_D_V), col(COL_D_G),
                pl.BlockSpec(memory_space=pltpu.SMEM),
                pl.BlockSpec((1, LANE), lambda b, h: (0, h))]
    args = [p, p, p, p, log_decay, gain]
    if has_state:
        (cos, sa, sb), s0, layer = state
        tab = pl.BlockSpec((t, LANE), lambda b, h: (0, 0))
        in_specs += [tab, tab, tab,
                     pl.BlockSpec((None, None, 2, None, MIX_DIM, MIX_DIM), lambda b, h: (b, layer, 0, h, 0, 0))]
        args += [cos, sa, sb, s0]
    out_shape = [jax.ShapeDtypeStruct((nb * t, BRANCH_W), BF16)]
    out_specs = [pl.BlockSpec((t, LANE), lambda b, h: (b, h))]
    if emit_state:
        out_shape.append(jax.ShapeDtypeStruct((nb, 2, MIX_HEADS, MIX_DIM, MIX_DIM), F32))
        out_specs.append(pl.BlockSpec((None, 2, None, MIX_DIM, MIX_DIM), lambda b, h: (b, 0, h, 0, 0)))
    return pl.pallas_call(
        functools.partial(_ret_kernel, t, has_state, emit_state),
        grid=(nb, MIX_HEADS),
        in_specs=in_specs,
        out_specs=out_specs,
        out_shape=out_shape,
        scratch_shapes=[pltpu.VMEM((t, MIX_DIM), BF16),
                        pltpu.VMEM((t // CHUNK, 2 * MIX_DIM, MIX_DIM), F32),
                        pltpu.VMEM((t, MIX_DIM), F32)],
        compiler_params=_params(("parallel", "parallel"), 2 * 7 * t * LANE * 4 + 3 * t * LANE * 4 + 2 * MIB),
        name="ret_lat" if has_state else "ret_ctx",
    )(*args)


def _softmax_sink_pv(s_parts, v_parts, sink):
    m = sink
    for s in s_parts:
        m = jnp.maximum(m, jnp.max(s, axis=-1, keepdims=True))
    es = [jnp.exp(s - m) for s in s_parts]
    den = jnp.exp(sink - m)
    for e in es:
        den = den + jnp.sum(e, axis=-1, keepdims=True)
    out = None
    for e, v in zip(es, v_parts):
        o = _dot(e.astype(BF16), v)
        out = o if out is None else out + o
    return out * (1.0 / den)


def _attn_ctx_kernel(q_ref, k_ref, v_ref, sink_ref, o_ref, ko_ref, vo_ref):
    k = k_ref[...]
    v = v_ref[...]
    ko_ref[...] = k
    vo_ref[...] = v
    hd = ATT_HEAD_DIM
    scale = hd ** -0.5
    for j in range(ATT_KV_HEADS):
        kj = k[:, j * hd:(j + 1) * hd].astype(BF16)
        vj = v[:, j * hd:(j + 1) * hd].astype(BF16)
        for g in range(ATT_GROUPS):
            hx = j * ATT_GROUPS + g
            qh = q_ref[:, hx * hd:(hx + 1) * hd].astype(BF16)
            s = _dot_nt(qh, kj) * scale
            sink = jnp.full((1, 1), sink_ref[hx], F32)
            o_ref[:, hx * hd:(hx + 1) * hd] = _softmax_sink_pv([s], [vj], sink).astype(BF16)


def _attn_ctx(p, nb, t, row_off, sink):
    assert row_off % t == 0
    ob = row_off // t
    qw = ATT_HEADS * ATT_HEAD_DIM
    kvw = ATT_KV_HEADS * ATT_HEAD_DIM
    return pl.pallas_call(
        _attn_ctx_kernel,
        grid=(nb,),
        in_specs=[
            pl.BlockSpec((t, qw), lambda b: (ob + b, COL_B_Q * LANE // qw)),
            pl.BlockSpec((t, kvw), lambda b: (ob + b, COL_B_K)),
            pl.BlockSpec((t, kvw), lambda b: (ob + b, COL_B_V)),
            pl.BlockSpec(memory_space=pltpu.SMEM),
        ],
        out_specs=[pl.BlockSpec((t, qw), lambda b: (b, 0)),
                   pl.BlockSpec((None, t, kvw), lambda b: (b, 0, 0)),
                   pl.BlockSpec((None, t, kvw), lambda b: (b, 0, 0))],
        out_shape=[jax.ShapeDtypeStruct((nb * t, qw), BF16),
                   jax.ShapeDtypeStruct((nb, t, kvw), F32),
                   jax.ShapeDtypeStruct((nb, t, kvw), F32)],
        compiler_params=_params(("parallel",), 2 * t * (qw * 6 + 4 * kvw * 4)),
        name="attn_ctx",
    )(p, p, p, sink)


def _band_bias():
    Q = QBLOCK
    r = np.arange(Q)[:, None]
    c = np.arange(3 * Q)[None, :]
    band = (c >= r) & (c <= r + 2 * Q)
    out = []
    for kind in range(3):
        ok = band.copy()
        if kind == 0:
            ok &= c >= Q
        if kind == 2:
            ok &= c < 2 * Q
        out.append(np.where(ok, 0.0, NEG).astype(np.float32))
    return jnp.asarray(np.stack(out))


def _half_variants(x):
    low = lax.broadcasted_iota(jnp.int32, x.shape, 1) < ATT_HEAD_DIM
    sw = pltpu.roll(x, ATT_HEAD_DIM, 1)
    zero = jnp.zeros_like(x)
    return [jnp.where(low, a, b).astype(BF16) for a, b in ((x, zero), (zero, sw), (sw, zero), (zero, x))]


def _attn_lat_kernel(t, q_ref, k_ref, v_ref, kc_ref, vc_ref, cq_ref, saq_ref, sbq_ref,
                     ck_ref, sak_ref, sbk_ref, bias_ref, sink_ref, o_ref, kp_scr, vp_scr, kc_scr, vc_scr):
    n = pl.program_id(1)
    Q = QBLOCK
    hd = ATT_HEAD_DIM
    quarter = hd // 4
    nvar = 2 * ATT_KV_HEADS

    @pl.when(n == 0)
    def _():
        zero = jnp.zeros((Q, LANE), BF16)
        for vi in range(nvar):
            kp_scr[vi, 0:Q, :] = zero
            vp_scr[vi, 0:Q, :] = zero
            kp_scr[vi, Q + t:2 * Q + t, :] = zero
            vp_scr[vi, Q + t:2 * Q + t, :] = zero

        def fill(c, _):
            rows = _chunk_rows(c, Q)
            dst = pl.ds(pl.multiple_of(c * Q + Q, Q), Q)
            kk = _rope(k_ref[rows, :], ck_ref[rows, :], sak_ref[rows, :], sbk_ref[rows, :], quarter)
            for vi, (kv, vv) in enumerate(zip(_half_variants(kk), _half_variants(v_ref[rows, :]))):
                kp_scr[vi, dst, :] = kv
                vp_scr[vi, dst, :] = vv
            return 0

        lax.fori_loop(0, t // Q, fill, 0)
        for vi, (kv, vv) in enumerate(zip(_half_variants(kc_ref[...]), _half_variants(vc_ref[...]))):
            kc_scr[vi] = kv
            vc_scr[vi] = vv

    band = pl.ds(pl.multiple_of(n * Q, Q), 3 * Q)
    cq, saq, sbq = cq_ref[...], saq_ref[...], sbq_ref[...]
    bias = bias_ref[...]
    scale = hd ** -0.5
    heads_per_chunk = LANE // hd
    for c in range(ATT_HEADS // heads_per_chunk):
        sl = slice(c * LANE, (c + 1) * LANE)
        qc = (_rope(q_ref[:, sl], cq, saq, sbq, quarter) * scale).astype(BF16)
        acc = None
        for half in range(heads_per_chunk):
            hx = c * heads_per_chunk + half
            vi = 2 * (hx // ATT_GROUPS) + half
            s_loc = _dot_nt(qc, kp_scr[vi, band, :]) + bias
            s_ctx = _dot_nt(qc, kc_scr[vi])
            sink = jnp.full((1, 1), sink_ref[hx], F32)
            o = _softmax_sink_pv([s_loc, s_ctx], [vp_scr[vi, band, :], vc_scr[vi]], sink)
            acc = o if acc is None else acc + o
        o_ref[:, sl] = acc.astype(BF16)


def _attn_lat(p, nb, t, kc, vc, layer, tabs, sink):
    assert (ATT_HEAD_DIM ** -0.5) == 2.0 ** round(np.log2(ATT_HEAD_DIM ** -0.5))
    qw = ATT_HEADS * ATT_HEAD_DIM
    kvw = ATT_KV_HEADS * ATT_HEAD_DIM
    assert kvw == LANE
    nq = t // QBLOCK
    past = kc.shape[2]
    nvar = 2 * ATT_KV_HEADS
    cos, sa, sb = tabs
    bias = _band_bias()
    tq = pl.BlockSpec((QBLOCK, LANE), lambda b, n: (n, 0))
    tk = pl.BlockSpec((t, LANE), lambda b, n: (0, 0))
    return pl.pallas_call(
        functools.partial(_attn_lat_kernel, t),
        grid=(nb, nq),
        in_specs=[
            pl.BlockSpec((QBLOCK, qw), lambda b, n: (b * nq + n, COL_B_Q * LANE // qw)),
            pl.BlockSpec((t, kvw), lambda b, n: (b, COL_B_K)),
            pl.BlockSpec((t, kvw), lambda b, n: (b, COL_B_V)),
            pl.BlockSpec((None, None, past, kvw), lambda b, n: (b, layer, 0, 0)),
            pl.BlockSpec((None, None, past, kvw), lambda b, n: (b, layer, 0, 0)),
            tq, tq, tq, tk, tk, tk,
            pl.BlockSpec((None,) + bias.shape[1:],
                         lambda b, n: (jnp.where(n == 0, 0, jnp.where(n == nq - 1, 2, 1)), 0, 0)),
            pl.BlockSpec(memory_space=pltpu.SMEM),
        ],
        out_specs=pl.BlockSpec((QBLOCK, qw), lambda b, n: (b * nq + n, 0)),
        out_shape=jax.ShapeDtypeStruct((nb * t, qw), BF16),
        scratch_shapes=[pltpu.VMEM((nvar, t + 2 * QBLOCK, LANE), BF16), pltpu.VMEM((nvar, t + 2 * QBLOCK, LANE), BF16),
                        pltpu.VMEM((nvar, past, LANE), BF16), pltpu.VMEM((nvar, past, LANE), BF16)],
        compiler_params=_params(("parallel", "arbitrary"),
                                2 * 5 * t * LANE * 4 + 2 * nvar * (t + 2 * QBLOCK + past) * LANE * 2 + 2 * MIB),
        name="attn_lat",
    )(p, p, p, kc, vc, cos, sa, sb, cos, sa, sb, bias, sink)


def _conv_kernel(t, cb_ref, cc_ref, cx_ref, w_ref, o_ref):
    u = cc_ref[...] * cx_ref[...]
    tok = lax.broadcasted_iota(jnp.int32, u.shape, 0)
    prev = jnp.where(tok == 0, 0.0, pltpu.roll(u, 1, 0))
    nxt = jnp.where(tok == t - 1, 0.0, pltpu.roll(u, t - 1, 0))
    o_ref[...] = (cb_ref[...] * (w_ref[0:1, :] * prev + w_ref[1:2, :] * u + w_ref[2:3, :] * nxt)).astype(BF16)


def _conv(p, nb, t, row_off, w):
    assert row_off % t == 0
    ob = row_off // t
    nblk = w.shape[1] // LANE

    def col(cb):
        return pl.BlockSpec((t, LANE), lambda b, j: (ob + b, cb + j))

    return pl.pallas_call(
        functools.partial(_conv_kernel, t),
        grid=(nb, nblk),
        in_specs=[col(COL_C_B), col(COL_C_C), col(COL_C_X),
                  pl.BlockSpec((w.shape[0], LANE), lambda b, j: (0, j))],
        out_specs=pl.BlockSpec((t, LANE), lambda b, j: (b, j)),
        out_shape=jax.ShapeDtypeStruct((nb * t, w.shape[1]), BF16),
        compiler_params=_params(("parallel", "parallel"), 2 * 4 * t * LANE * 4),
        name="conv",
    )(p, p, p, w)


GATE_SRC = 4 * MIX_HEADS * MIX_DIM
N_GATES = 4 * MIX_HEADS


def _gate_lanes(x):
    g = x.reshape(x.shape[:-1] + (2, 2, MIX_HEADS))
    pad = [(0, 0)] * (x.ndim - 1) + [(0, LANE - 2 * MIX_HEADS)]
    blocks = [jnp.pad(g[..., kind, :].reshape(x.shape[:-1] + (2 * MIX_HEADS,)), pad) for kind in range(2)]
    return jnp.concatenate(blocks, axis=-1)


def _stage_mixer_proj(w_in):
    head = w_in[..., 0:GATE_SRC]
    tail = w_in[..., GATE_SRC + N_GATES:]
    gates = _gate_lanes(w_in[..., GATE_SRC:GATE_SRC + N_GATES])
    assert head.shape[-1] + tail.shape[-1] == COL_GATES * LANE
    assert head.shape[-1] + tail.shape[-1] + gates.shape[-1] == PROJ_P
    return jnp.concatenate([head, tail, gates], axis=-1).astype(BF16)


def kernel(x_prompt, x_sample, cache_attn_k, cache_attn_v, state_mlstm_C, state_mlstm_n, state_mlstm_m,
           state_ret_S, c, c_ctx, w_mod, b_mod, norm_g, ffn1_w_in, ffn1_w_out, ffn2_w_in, ffn2_w_out,
           w_in, mlstm_gate_b, mlstm_norm_g, attn_sink, conv_w, ret_log_decay, ret_norm_g,
           w_branch, w_gate, w_out):
    bp, tp, d = x_prompt.shape
    bs, ts, _ = x_sample.shape
    depth = w_mod.shape[0]
    past = cache_attn_k.shape[2]
    kvw = ATT_KV_HEADS * ATT_HEAD_DIM
    fits = [t for t in (1024, 512, 256) if ts % t == 0 and (bp * tp) % t == 0]
    rows = _Rows(bs, ts, bp, tp, fits[0])
    rows_ffn = _Rows(bs, ts, bp, tp, fits[min(1, len(fits) - 1)])

    nbp = -(-(1 + bs) // 8) * 8
    cpad = jnp.concatenate([c_ctx[None], c, jnp.zeros((nbp - 1 - bs, d), F32)], axis=0)
    mod = _modulation(cpad, w_mod, b_mod).reshape(depth, nbp, 3, 3, d)

    w_p = _stage_mixer_proj(w_in)
    ffn1_in, ffn1_out = ffn1_w_in.astype(BF16), ffn1_w_out.astype(BF16)
    ffn2_in, ffn2_out = ffn2_w_in.astype(BF16), ffn2_w_out.astype(BF16)
    wg_b, wb_b, wo_b = w_gate.astype(BF16), w_branch.astype(BF16), w_out.astype(BF16)

    gate_bias = _gate_lanes(mlstm_gate_b.reshape(depth, 1, N_GATES))
    tabs_att = _rope_tables(ts, ATT_HEAD_DIM)
    tabs_ret = _rope_tables(ts, MIX_DIM)

    x = jnp.concatenate([x_sample.reshape(bs * ts, d), x_prompt.reshape(bp * tp, d)], axis=0)
    ms = bs * ts
    kc_all = cache_attn_k.reshape(bs, depth, past, kvw)
    vc_all = cache_attn_v.reshape(bs, depth, past, kvw)
    ks_, vs_, cs_, ns_, mms_, ss_ = [], [], [], [], [], []
    for l in range(depth):
        g = norm_g[l]
        x = _ffn(rows_ffn, x, mod[l, :, 0], g[0:2], ffn1_in, ffn1_out, l)

        p, h = _inproj(rows, x, mod[l, :, 1], g[2:4], w_p, l)
        a_gain = mlstm_norm_g[l][None]
        d_gain = ret_norm_g[l][None]
        lat_a = _mlstm(p, bs, ts, 0, gate_bias[l], a_gain, (state_mlstm_C, state_mlstm_n, state_mlstm_m, l))[0]
        lat_b = _attn_lat(p, bs, ts, kc_all, vc_all, l, tabs_att, attn_sink[l])
        lat_c = _conv(p, bs, ts, 0, conv_w[l])
        lat_d = _retention(p, bs, ts, 0, ret_log_decay[l], d_gain, (tabs_ret, state_ret_S, l))[0]
        ctx_a, c_fin, n_fin, m_fin = _mlstm(p, bp, tp, ms, gate_bias[l], a_gain, None)
        ctx_b, k_new, v_new = _attn_ctx(p, bp, tp, ms, attn_sink[l])
        ctx_c = _conv(p, bp, tp, ms, conv_w[l])
        ctx_d, s_fin = _retention(p, bp, tp, ms, ret_log_decay[l], d_gain, None)
        y = _merge(rows, h, (lat_a, lat_b, lat_c, lat_d), (ctx_a, ctx_b, ctx_c, ctx_d), wg_b, wb_b, l)
        x = _outproj(rows_ffn, x, mod[l, :, 1], g[2:4], y, wo_b, l)

        x = _ffn(rows_ffn, x, mod[l, :, 2], g[4:6], ffn2_in, ffn2_out, l)

        ks_.append(k_new.reshape(bp, tp, ATT_KV_HEADS, ATT_HEAD_DIM))
        vs_.append(v_new.reshape(bp, tp, ATT_KV_HEADS, ATT_HEAD_DIM))
        cs_.append(c_fin)
        ns_.append(n_fin)
        mms_.append(m_fin[:, :, 0].reshape(bp, 2, MIX_HEADS))
        ss_.append(s_fin)

    y_sample = x[:ms].reshape(bs, ts, d)
    y_prompt = x[ms:].reshape(bp, tp, d)
    return (y_prompt, y_sample, jnp.stack(ks_, axis=1), jnp.stack(vs_, axis=1), jnp.stack(cs_, axis=1),
            jnp.stack(ns_, axis=1), jnp.stack(mms_, axis=1), jnp.stack(ss_, axis=1))
```

```python
import functools

import numpy as np
import jax
import jax.numpy as jnp
from jax import lax
from jax.experimental import pallas as pl
from jax.experimental.pallas import tpu as pltpu

F32 = jnp.float32
BF16 = jnp.bfloat16

N_MOD = 9
N_BRANCH = 4
BRANCH_W = 512
MIX_HEADS = 4
MIX_DIM = 128
ATT_HEADS = 8
ATT_KV_HEADS = 2
ATT_HEAD_DIM = 64
ATT_GROUPS = ATT_HEADS // ATT_KV_HEADS
QBLOCK = 128
CHUNK = 128
GRID_W = 64
ROPE_BASE = 10000.0
EPS = 1e-6
NEG = -1e30

LANE = 128
COL_A_Q, COL_A_K, COL_A_V, COL_A_O = 0, 4, 8, 12
COL_B_Q, COL_B_K, COL_B_V = 16, 20, 21
COL_C_B, COL_C_C, COL_C_X = 22, 26, 30
COL_D_Q, COL_D_K, COL_D_V, COL_D_G = 34, 38, 42, 46
COL_GATES = 50
PROJ_BLOCKS = 52
PROJ_P = PROJ_BLOCKS * LANE

MIB = 1024 * 1024
VMEM_CAP = 44 * MIB
VMEM_TEMPS = 12 * MIB
SINGLE = pl.Buffered(1)


def _params(sem, buffers):
    return pltpu.CompilerParams(dimension_semantics=sem,
                                vmem_limit_bytes=int(min(buffers + VMEM_TEMPS, VMEM_CAP)))


def _pick_tile(n, pref):
    if n <= pref:
        return n
    t = (pref // LANE) * LANE
    while t >= LANE:
        if n % t == 0:
            return t
        t -= LANE
    raise ValueError(f"no lane-aligned tile divides {n}")


def _dot(a, b):
    return jnp.dot(a, b, preferred_element_type=F32)


def _dot_nt(a, b):
    return lax.dot_general(a, b, (((1,), (1,)), ((), ())), preferred_element_type=F32)


def _dot_tn(a, b):
    return lax.dot_general(a, b, (((0,), (0,)), ((), ())), preferred_element_type=F32)


def _sigmoid(x):
    return 1.0 / (1.0 + jnp.exp(-x))


def _rms(xf, g):
    ms = jnp.mean(xf * xf, axis=-1, keepdims=True)
    return xf * lax.rsqrt(ms + EPS) * g


def _modulated(x_ref, mod_ref, g_ref):
    return _rms(x_ref[...], g_ref[0:1, :]) * (1.0 + mod_ref[1:2, :]) + mod_ref[0:1, :]


def _head_norm(x, g):
    mu = jnp.mean(x, axis=-1, keepdims=True)
    xc = x - mu
    var = jnp.mean(xc * xc, axis=-1, keepdims=True)
    return xc * lax.rsqrt(var + EPS) * g


def _chunk_rows(c, n=CHUNK):
    return pl.ds(pl.multiple_of(c * n, n), n)


def _mod_kernel(c_ref, w_ref, b_ref, o_ref):
    c = c_ref[...]
    s = (c * _sigmoid(c)).astype(BF16)
    o_ref[...] = _dot(s, w_ref[...].astype(BF16)) + b_ref[...]


def _modulation(cpad, w_mod, b_mod):
    depth, d, n = w_mod.shape
    nbp = cpad.shape[0]
    tn = _pick_tile(n, 1024)
    return pl.pallas_call(
        _mod_kernel,
        grid=(depth, n // tn),
        in_specs=[
            pl.BlockSpec((nbp, d), lambda l, j: (0, 0)),
            pl.BlockSpec((None, d, tn), lambda l, j: (l, 0, j)),
            pl.BlockSpec((None, 1, tn), lambda l, j: (l, 0, j)),
        ],
        out_specs=pl.BlockSpec((None, nbp, tn), lambda l, j: (l, 0, j)),
        out_shape=jax.ShapeDtypeStruct((depth, nbp, n), F32),
        compiler_params=_params(("parallel", "parallel"), 2 * d * tn * 4 + MIB),
        name="modulation",
    )(cpad, w_mod, b_mod.reshape(depth, 1, n))


class _Rows:
    def __init__(self, bs, ts, bp, tp, tm):
        self.bs, self.ts, self.bp, self.tp = bs, ts, bp, tp
        self.ms, self.mp = bs * ts, bp * tp
        self.m = self.ms + self.mp
        assert ts % tm == 0 and self.mp % tm == 0 and self.ms % tp == 0
        self.tm = tm
        self.lat_tiles = self.ms // tm
        self.tiles_per_seq = ts // tm

    def mod_index(self, i):
        return jnp.where(i < self.lat_tiles, 1 + i // self.tiles_per_seq, 0)

    def x_spec(self, d, mode=None):
        return pl.BlockSpec((self.tm, d), lambda i, *_: (i, 0), pipeline_mode=mode)

    def mod_spec(self, d):
        return pl.BlockSpec((None, 3, d), lambda i, *_: (self.mod_index(i), 0, 0))


def _layer_spec(block, index_map, l, mode=None):
    return pl.BlockSpec((None,) + tuple(block), lambda *ids: (l,) + tuple(index_map(*ids)), pipeline_mode=mode)


def _ffn_kernel(nf, x_ref, mod_ref, g_ref, wg_ref, wu_ref, wo_ref, o_ref, h_scr, acc_scr):
    f = pl.program_id(1)

    def hidden_slab(h):
        gate = _dot(h, wg_ref[...])
        up = _dot(h, wu_ref[...])
        a = (gate * _sigmoid(gate)) * up
        return _dot(a.astype(BF16), wo_ref[...])

    def pre():
        h = _modulated(x_ref, mod_ref, g_ref).astype(BF16)
        h_scr[...] = h
        return h

    def post(z):
        o_ref[...] = x_ref[...] + (0.5 * mod_ref[2:3, :]) * _rms(z, g_ref[1:2, :])

    if nf == 1:
        post(hidden_slab(pre()))
        return

    @pl.when(f == 0)
    def _():
        acc_scr[...] = hidden_slab(pre())

    @pl.when((f > 0) & (f < nf - 1))
    def _():
        acc_scr[...] += hidden_slab(h_scr[...])

    @pl.when(f == nf - 1)
    def _():
        post(acc_scr[...] + hidden_slab(h_scr[...]))


def _ffn(rows, x, mod_s, g2, w_in, w_out, l):
    m, d = x.shape
    dff = w_out.shape[1]
    tm = rows.tm
    tf = w_in.shape[3]
    nf = dff // tf
    assert w_in.shape[1] == 2 * nf
    buffers = tm * d * (2 * 4 + 2 * 4 + 4 + 2) + 3 * 2 * d * tf * 2
    return pl.pallas_call(
        functools.partial(_ffn_kernel, nf),
        grid=(m // tm, nf),
        in_specs=[
            rows.x_spec(d),
            rows.mod_spec(d),
            pl.BlockSpec((2, d), lambda i, f: (0, 0)),
            _layer_spec((None, d, tf), lambda i, f: (f, 0, 0), l),
            _layer_spec((None, d, tf), lambda i, f: (f + nf, 0, 0), l),
            _layer_spec((tf, d), lambda i, f: (f, 0), l),
        ],
        out_specs=pl.BlockSpec((tm, d), lambda i, f: (i, 0)),
        out_shape=jax.ShapeDtypeStruct((m, d), F32),
        scratch_shapes=[pltpu.VMEM((tm, d), BF16), pltpu.VMEM((tm, d), F32)],
        compiler_params=_params(("parallel", "arbitrary"), buffers),
        name="ffn",
    )(x, mod_s, g2, w_in, w_in, w_out)


def _inproj_kernel(x_ref, mod_ref, g_ref, w_ref, p_ref, h_ref):
    @pl.when(pl.program_id(1) == 0)
    def _():
        h = _modulated(x_ref, mod_ref, g_ref).astype(BF16)
        h_ref[...] = h
        p_ref[...] = _dot(h, w_ref[...])

    @pl.when(pl.program_id(1) > 0)
    def _():
        p_ref[...] = _dot(h_ref[...], w_ref[...])


def _inproj(rows, x, mod_s, g2, w_p, l):
    m, d = x.shape
    nt, _, tn = w_p.shape[1:]
    n = nt * tn
    tm = rows.tm
    return pl.pallas_call(
        _inproj_kernel,
        grid=(m // tm, nt),
        in_specs=[
            rows.x_spec(d),
            rows.mod_spec(d),
            pl.BlockSpec((2, d), lambda i, j: (0, 0)),
            _layer_spec((None, d, tn), lambda i, j: (j, 0, 0), l),
        ],
        out_specs=[pl.BlockSpec((tm, tn), lambda i, j: (i, j)),
                   pl.BlockSpec((tm, d), lambda i, j: (i, 0))],
        out_shape=[jax.ShapeDtypeStruct((m, n), F32), jax.ShapeDtypeStruct((m, d), BF16)],
        compiler_params=_params(("parallel", "arbitrary"),
                                tm * d * (2 * 4 + 2 * 2) + 2 * d * tn * 2 + 2 * tm * tn * 4),
        name="inproj",
    )(x, mod_s, g2, w_p)


def _merge_kernel(lat_tiles, h_ref, *refs):
    lat_refs, ctx_refs = refs[0:N_BRANCH], refs[N_BRANCH:2 * N_BRANCH]
    wg_ref, wb_ref, y_ref = refs[2 * N_BRANCH:]
    h = h_ref[...]
    is_lat = pl.program_id(0) < lat_tiles
    y = None
    for i in range(N_BRANCH):
        o = jnp.where(is_lat, lat_refs[i][...], ctx_refs[i][...])
        term = _sigmoid(_dot(h, wg_ref[i])) * _dot(o, wb_ref[i])
        y = term if y is None else y + term
    y_ref[...] = y.astype(BF16)


def _merge(rows, h, lat, ctx, w_gate, w_branch, l):
    m, d = h.shape
    tm = rows.tm
    nt, _, bw, tn = w_branch.shape[1:]
    assert w_gate.shape[1:] == (nt, N_BRANCH, d, tn)
    last_lat = rows.lat_tiles - 1
    lat_spec = pl.BlockSpec((tm, bw), lambda i, j: (jnp.minimum(i, last_lat), 0))
    ctx_spec = pl.BlockSpec((tm, bw), lambda i, j: (jnp.maximum(i - rows.lat_tiles, 0), 0))
    return pl.pallas_call(
        functools.partial(_merge_kernel, rows.lat_tiles),
        grid=(m // tm, nt),
        in_specs=[
            rows.x_spec(d),
            *([lat_spec] * N_BRANCH), *([ctx_spec] * N_BRANCH),
            _layer_spec((None, N_BRANCH, d, tn), lambda i, j: (j, 0, 0, 0), l),
            _layer_spec((None, N_BRANCH, bw, tn), lambda i, j: (j, 0, 0, 0), l),
        ],
        out_specs=pl.BlockSpec((tm, tn), lambda i, j: (i, j)),
        out_shape=jax.ShapeDtypeStruct((m, d), BF16),
        compiler_params=_params(("parallel", "arbitrary"),
                                2 * tm * d * 2 + 2 * 2 * N_BRANCH * tm * bw * 2
                                + 2 * N_BRANCH * (d + bw) * tn * 2 + 2 * tm * tn * 2),
        name="merge",
    )(h, *lat, *ctx, w_gate, w_branch)


def _outproj_kernel(x_ref, mod_ref, g_ref, y_ref, w_ref, o_ref):
    z = _dot(y_ref[...], w_ref[...])
    o_ref[...] = x_ref[...] + mod_ref[2:3, :] * _rms(z, g_ref[1:2, :])


def _outproj(rows, x, mod_s, g2, y, w_out, l):
    m, d = x.shape
    tm = rows.tm
    return pl.pallas_call(
        _outproj_kernel,
        grid=(m // tm,),
        in_specs=[
            rows.x_spec(d),
            rows.mod_spec(d),
            pl.BlockSpec((2, d), lambda i: (0, 0)),
            rows.x_spec(d),
            _layer_spec((d, d), lambda i: (0, 0), l, SINGLE),
        ],
        out_specs=pl.BlockSpec((tm, d), lambda i: (i, 0)),
        out_shape=jax.ShapeDtypeStruct((m, d), F32),
        compiler_params=_params(("parallel",), tm * d * (2 * 4 + 2 * 4 + 4 + 2 * 2) + d * d * 2),
        name="outproj",
    )(x, mod_s, g2, y, w_out)


def _rope_tables(t, d):
    rows = t // GRID_W
    row = jnp.repeat(jnp.arange(rows), GRID_W).astype(F32)
    col = jnp.tile(jnp.arange(GRID_W), rows).astype(F32)
    quarter = d // 4
    inv = jnp.power(ROPE_BASE, -jnp.arange(quarter, dtype=F32) / quarter)
    ar = row[:, None] * inv
    ac = col[:, None] * inv
    ang = jnp.concatenate([ar, ar, ac, ac], axis=-1)
    cos, sin = jnp.cos(ang), jnp.sin(ang)
    first = (jnp.arange(d) % (2 * quarter)) < quarter
    sa = jnp.where(first, -sin, 0.0)
    sb = jnp.where(first, 0.0, sin)
    reps = LANE // d
    return tuple(jnp.tile(a, (1, reps)) for a in (cos, sa, sb))


def _rope(x, cos, sa, sb, quarter):
    n = x.shape[-1]
    ahead = pltpu.roll(x, n - quarter, 1)
    behind = pltpu.roll(x, quarter, 1)
    return x * cos + ahead * sa + behind * sb


def _log_sigmoid(x):
    return jnp.minimum(x, 0.0) - jnp.log1p(jnp.exp(-jnp.abs(x)))


def _split_bf16(x):
    hi = x.astype(BF16)
    return hi, (x - hi.astype(F32)).astype(BF16)


def _mlstm_kernel(t, layer, emit_state, *refs):
    has_state = layer is not None
    it = iter(refs)
    q_ref, k_ref, v_ref, og_ref, gt_ref, bias_ref, gain_ref = (next(it) for _ in range(7))
    if has_state:
        c0_ref, n0_ref, m0_ref = next(it), next(it), next(it)
    o_ref = next(it)
    if emit_state:
        cout_ref, nout_ref, mout_ref = next(it), next(it), next(it)
    h_scr, s_scr = next(it), next(it)

    L, D, H = CHUNK, MIX_DIM, MIX_HEADS
    nc = t // L
    r_i = lax.broadcasted_iota(jnp.int32, (L, L), 0)
    c_i = lax.broadcasted_iota(jnp.int32, (L, L), 1)
    masks = (r_i >= c_i, r_i <= c_i)
    cum = tuple(mk.astype(BF16) for mk in masks)
    ones = jnp.ones((L, D), BF16)
    sel = [(r_i == g).astype(BF16) for g in range(2 * H)]
    bias = bias_ref[...]
    scale = D ** -0.5

    ms0 = []
    for d in range(2):
        for h in range(H):
            if has_state:
                s_scr[d, h, :, 0:D] = c0_ref[d, h]
                s_scr[d, h, :, D:2 * D] = jnp.broadcast_to(n0_ref[d, h:h + 1, :], (D, D)).T
                ms0.append(jnp.full((1, D), m0_ref[pl.program_id(0), layer, d, h], F32))
            else:
                s_scr[d, h] = jnp.zeros((D, 2 * D), F32)
                ms0.append(jnp.zeros((1, D), F32))

    def zero_rows(c, _):
        h_scr[_chunk_rows(c), :] = jnp.zeros((L, H * D), F32)
        return 0

    lax.fori_loop(0, nc, zero_rows, 0)

    def chunk_dir(c, d, ms):
        rows = _chunk_rows(c)
        g = gt_ref[rows, :] + bias
        lf_hi, lf_lo = _split_bf16(_log_sigmoid(g[:, LANE:2 * LANE]))
        f_all = _dot(cum[d], lf_hi) + _dot(cum[d], lf_lo)
        b_all = g[:, 0:LANE] - f_all
        b_t = b_all.T
        f_hi, f_lo = _split_bf16(f_all)
        b_hi, b_lo = _split_bf16(b_all)
        out = []
        for h in range(H):
            j = d * H + h
            cols = slice(h * D, (h + 1) * D)
            q = q_ref[rows, cols]
            k = k_ref[rows, cols] * scale
            qb = q.astype(BF16)
            v_aug = jnp.concatenate([v_ref[rows, cols].astype(BF16), ones], axis=1)
            f_rep = _dot(f_hi, sel[j]) + _dot(f_lo, sel[j])
            b_rep = _dot(b_hi, sel[j]) + _dot(b_lo, sel[j])
            brow = b_t[j:j + 1, :]
            f_last = f_rep[L - 1:L, :] if d == 0 else f_rep[0:1, :]
            m = ms[j]
            lw = jnp.where(masks[d], f_rep + brow, NEG)
            inter = f_rep + m
            m_tok = jnp.maximum(jnp.max(lw, axis=-1, keepdims=True), inter)
            a = _dot_nt(qb, k.astype(BF16)) * jnp.exp(lw - m_tok)
            w_inter = jnp.exp(inter - m_tok)
            s_old = s_scr[d, h]
            pv = _dot(a.astype(BF16), v_aug)
            qs = _dot(qb, s_old.astype(BF16))
            num = pv[:, 0:D] + w_inter * qs[:, 0:D]
            den = pv[:, D:2 * D] + w_inter * qs[:, D:2 * D]
            hh = num * (1.0 / jnp.maximum(jnp.abs(den), jnp.exp(-m_tok)))
            h_scr[rows, cols] += hh
            bmax = jnp.max(b_rep, axis=0, keepdims=True)
            a_end = f_last + bmax
            kw = k * jnp.exp(b_rep - bmax)
            u_aug = _dot_tn(kw.astype(BF16), v_aug)
            m_new = jnp.maximum(f_last + m, a_end)
            d_old = jnp.exp(f_last + m - m_new)
            d_new = jnp.exp(a_end - m_new)
            for half in (slice(0, D), slice(D, 2 * D)):
                s_scr[d, h, :, half] = d_old * s_old[:, half] + d_new * u_aug[:, half]
            out.append(m_new)
        return out

    def body(c, ms):
        mf = chunk_dir(c, 0, ms)
        mb = chunk_dir(nc - 1 - c, 1, ms)
        return tuple(mf + mb)

    ms = lax.fori_loop(0, nc, body, tuple(ms0))

    gain = gain_ref[...]

    def finish(c, _):
        rows = _chunk_rows(c)
        for h in range(H):
            cols = slice(h * D, (h + 1) * D)
            y = _head_norm(h_scr[rows, cols], gain[:, cols])
            o_ref[rows, cols] = (_sigmoid(og_ref[rows, cols]) * y).astype(BF16)
        return 0

    lax.fori_loop(0, nc, finish, 0)

    if emit_state:
        for d in range(2):
            for h in range(H):
                cout_ref[d, h] = s_scr[d, h, :, 0:D]
                nout_ref[d, h:h + 1, :] = s_scr[d, h, :, D:2 * D].T[0:1, :]
                mout_ref[d * H + h:d * H + h + 1, :] = ms[d * H + h]


def _mlstm(p, nb, t, row_off, bias, gain, state):
    assert row_off % t == 0
    ob = row_off // t
    has_state = state is not None
    emit_state = not has_state
    hw = MIX_HEADS * MIX_DIM
    st_c = pl.BlockSpec((None, 2, MIX_HEADS, MIX_DIM, MIX_DIM), lambda b: (b, 0, 0, 0, 0))
    st_n = pl.BlockSpec((None, 2, MIX_HEADS, MIX_DIM), lambda b: (b, 0, 0, 0))

    def col(cb):
        return pl.BlockSpec((t, hw), lambda b: (ob + b, cb * LANE // hw), pipeline_mode=mode)

    mode = SINGLE if t * hw * 4 >= 2 * MIB else None
    nbuf = 1 if mode is SINGLE else 2
    buffers = (nbuf * (4 * t * hw + t * 2 * LANE) * 4 + 2 * t * hw * 2 + t * hw * 4
               + 6 * 2 * MIX_HEADS * MIX_DIM * MIX_DIM * 4)
    in_specs = [col(COL_A_Q), col(COL_A_K), col(COL_A_V), col(COL_A_O),
                pl.BlockSpec((t, 2 * LANE), lambda b: (ob + b, COL_GATES // 2), pipeline_mode=mode),
                pl.BlockSpec((1, 2 * LANE), lambda b: (0, 0)),
                pl.BlockSpec((1, hw), lambda b: (0, 0))]
    args = [p, p, p, p, p, bias, gain]
    layer = None
    if has_state:
        layer = state[3]
        in_specs += [pl.BlockSpec((None, None, 2, MIX_HEADS, MIX_DIM, MIX_DIM), lambda b: (b, layer, 0, 0, 0, 0)),
                     pl.BlockSpec((None, None, 2, MIX_HEADS, MIX_DIM), lambda b: (b, layer, 0, 0, 0)),
                     pl.BlockSpec(memory_space=pltpu.SMEM)]
        args += list(state[0:3])
    out_shape = [jax.ShapeDtypeStruct((nb * t, hw), BF16)]
    out_specs = [pl.BlockSpec((t, hw), lambda b: (b, 0))]
    if emit_state:
        out_shape += [jax.ShapeDtypeStruct((nb, 2, MIX_HEADS, MIX_DIM, MIX_DIM), F32),
                      jax.ShapeDtypeStruct((nb, 2, MIX_HEADS, MIX_DIM), F32),
                      jax.ShapeDtypeStruct((nb, 2 * MIX_HEADS, MIX_DIM), F32)]
        out_specs += [st_c, st_n, pl.BlockSpec((None, 2 * MIX_HEADS, MIX_DIM), lambda b: (b, 0, 0))]
    return pl.pallas_call(
        functools.partial(_mlstm_kernel, t, layer, emit_state),
        grid=(nb,),
        in_specs=in_specs,
        out_specs=out_specs,
        out_shape=out_shape,
        scratch_shapes=[pltpu.VMEM((t, hw), F32),
                        pltpu.VMEM((2, MIX_HEADS, MIX_DIM, 2 * MIX_DIM), F32)],
        compiler_params=_params(("parallel",), buffers),
        name="mlstm_lat" if has_state else "mlstm_ctx",
    )(*args)


def _ret_kernel(t, has_state, emit_state, *refs):
    it = iter(refs)
    q_ref, k_ref, v_ref, dg_ref, lg_ref, gain_ref = (next(it) for _ in range(6))
    if has_state:
        cos_ref, sa_ref, sb_ref, s0_ref = (next(it) for _ in range(4))
    o_ref = next(it)
    if emit_state:
        sout_ref = next(it)
    q_scr, u_scr, o_scr = (next(it) for _ in range(3))

    L, D = CHUNK, MIX_DIM
    nc = t // L
    hi = pl.program_id(1)
    lgs = (lg_ref[0, hi], lg_ref[1, hi])
    diff = (lax.broadcasted_iota(jnp.int32, (L, L), 0) - lax.broadcasted_iota(jnp.int32, (L, L), 1)).astype(F32)
    tcol = lax.broadcasted_iota(jnp.int32, (L, 1), 0).astype(F32)
    dm = (jnp.where(diff >= 0.0, jnp.exp(jnp.maximum(diff, 0.0) * lgs[0]), 0.0)
          + jnp.where(diff <= 0.0, jnp.exp(jnp.maximum(-diff, 0.0) * lgs[1]), 0.0))
    w_end = (jnp.exp((L - 1.0 - tcol) * lgs[0]), jnp.exp(tcol * lgs[1]))
    w_cross = (jnp.exp((tcol + 1.0) * lgs[0]), jnp.exp((L - tcol) * lgs[1]))
    decay = tuple(jnp.exp(jnp.full((1, 1), float(L), F32) * lgs[d]) for d in range(2))
    scale = D ** -0.5

    def local(c, _):
        rows = _chunk_rows(c)
        q = q_ref[rows, :]
        k = k_ref[rows, :] * scale
        if has_state:
            cos, sa, sb = cos_ref[rows, :], sa_ref[rows, :], sb_ref[rows, :]
            q = _rope(q, cos, sa, sb, D // 4)
            k = _rope(k, cos, sa, sb, D // 4)
        qb = q.astype(BF16)
        vb = v_ref[rows, :].astype(BF16)
        q_scr[rows, :] = qb
        o_scr[rows, :] = _dot((_dot_nt(qb, k.astype(BF16)) * dm).astype(BF16), vb)
        kw = jnp.concatenate([(k * w_end[0]).astype(BF16), (k * w_end[1]).astype(BF16)], axis=1)
        u_scr[c] = _dot_tn(kw, vb)
        return 0

    lax.fori_loop(0, nc, local, 0, unroll=2)

    def scan(c, carry):
        sf, sb = carry
        rows = _chunk_rows(c)
        o_scr[rows, :] += _dot(q_scr[rows, :], sf.astype(BF16)) * w_cross[0]
        sf = decay[0] * sf + u_scr[c, 0:D, :]
        cb = nc - 1 - c
        rows_b = _chunk_rows(cb)
        o_scr[rows_b, :] += _dot(q_scr[rows_b, :], sb.astype(BF16)) * w_cross[1]
        sb = decay[1] * sb + u_scr[cb, D:2 * D, :]
        return sf, sb

    if has_state:
        init = (s0_ref[0], s0_ref[1])
    else:
        init = (jnp.zeros((D, D), F32), jnp.zeros((D, D), F32))
    sf, sb = lax.fori_loop(0, nc, scan, init, unroll=2)

    gain = gain_ref[...]

    def finish(c, _):
        rows = _chunk_rows(c)
        y = _head_norm(o_scr[rows, :], gain)
        dg = dg_ref[rows, :]
        o_ref[rows, :] = ((dg * _sigmoid(dg)) * y).astype(BF16)
        return 0

    lax.fori_loop(0, nc, finish, 0, unroll=2)

    if emit_state:
        sout_ref[0] = sf
        sout_ref[1] = sb


def _retention(p, nb, t, row_off, log_decay, gain, state):
    assert row_off % t == 0
    ob = row_off // t
    has_state = state is not None
    emit_state = not has_state

    def col(cb):
        return pl.BlockSpec((t, LANE), lambda b, h: (ob + b, cb + h))

    in_specs = [col(COL_D_Q), col(COL_D_K), col(COL_D_V), col(COL_D_G),
                pl.BlockSpec(memory_space=pltpu.SMEM),
                pl.BlockSpec((1, LANE), lambda b, h: (0, h))]
    args = [p, p, p, p, log_decay, gain]
    if has_state:
        (cos, sa, sb), s0, layer = state
        tab = pl.BlockSpec((t, LANE), lambda b, h: (0, 0))
        in_specs += [tab, tab, tab,
                     pl.BlockSpec((None, None, 2, None, MIX_DIM, MIX_DIM), lambda b, h: (b, layer, 0, h, 0, 0))]
        args += [cos, sa, sb, s0]
    out_shape = [jax.ShapeDtypeStruct((nb * t, BRANCH_W), BF16)]
    out_specs = [pl.BlockSpec((t, LANE), lambda b, h: (b, h))]
    if emit_state:
        out_shape.append(jax.ShapeDtypeStruct((nb, 2, MIX_HEADS, MIX_DIM, MIX_DIM), F32))
        out_specs.append(pl.BlockSpec((None, 2, None, MIX_DIM, MIX_DIM), lambda b, h: (b, 0, h, 0, 0)))
    return pl.pallas_call(
        functools.partial(_ret_kernel, t, has_state, emit_state),
        grid=(nb, MIX_HEADS),
        in_specs=in_specs,
        out_specs=out_specs,
        out_shape=out_shape,
        scratch_shapes=[pltpu.VMEM((t, MIX_DIM), BF16),
                        pltpu.VMEM((t // CHUNK, 2 * MIX_DIM, MIX_DIM), F32),
                        pltpu.VMEM((t, MIX_DIM), F32)],
        compiler_params=_params(("parallel", "parallel"), 2 * 7 * t * LANE * 4 + 3 * t * LANE * 4 + 2 * MIB),
        name="ret_lat" if has_state else "ret_ctx",
    )(*args)


def _softmax_sink_pv(s_parts, v_parts, sink):
    m = sink
    for s in s_parts:
        m = jnp.maximum(m, jnp.max(s, axis=-1, keepdims=True))
    es = [jnp.exp(s - m) for s in s_parts]
    den = jnp.exp(sink - m)
    for e in es:
        den = den + jnp.sum(e, axis=-1, keepdims=True)
    out = None
    for e, v in zip(es, v_parts):
        o = _dot(e.astype(BF16), v)
        out = o if out is None else out + o
    return out * (1.0 / den)


def _attn_ctx_kernel(q_ref, k_ref, v_ref, sink_ref, o_ref, ko_ref, vo_ref):
    k = k_ref[...]
    v = v_ref[...]
    ko_ref[...] = k
    vo_ref[...] = v
    hd = ATT_HEAD_DIM
    scale = hd ** -0.5
    for j in range(ATT_KV_HEADS):
        kj = k[:, j * hd:(j + 1) * hd].astype(BF16)
        vj = v[:, j * hd:(j + 1) * hd].astype(BF16)
        for g in range(ATT_GROUPS):
            hx = j * ATT_GROUPS + g
            qh = q_ref[:, hx * hd:(hx + 1) * hd].astype(BF16)
            s = _dot_nt(qh, kj) * scale
            sink = jnp.full((1, 1), sink_ref[hx], F32)
            o_ref[:, hx * hd:(hx + 1) * hd] = _softmax_sink_pv([s], [vj], sink).astype(BF16)


def _attn_ctx(p, nb, t, row_off, sink):
    assert row_off % t == 0
    ob = row_off // t
    qw = ATT_HEADS * ATT_HEAD_DIM
    kvw = ATT_KV_HEADS * ATT_HEAD_DIM
    return pl.pallas_call(
        _attn_ctx_kernel,
        grid=(nb,),
        in_specs=[
            pl.BlockSpec((t, qw), lambda b: (ob + b, COL_B_Q * LANE // qw)),
            pl.BlockSpec((t, kvw), lambda b: (ob + b, COL_B_K)),
            pl.BlockSpec((t, kvw), lambda b: (ob + b, COL_B_V)),
            pl.BlockSpec(memory_space=pltpu.SMEM),
        ],
        out_specs=[pl.BlockSpec((t, qw), lambda b: (b, 0)),
                   pl.BlockSpec((None, t, kvw), lambda b: (b, 0, 0)),
                   pl.BlockSpec((None, t, kvw), lambda b: (b, 0, 0))],
        out_shape=[jax.ShapeDtypeStruct((nb * t, qw), BF16),
                   jax.ShapeDtypeStruct((nb, t, kvw), F32),
                   jax.ShapeDtypeStruct((nb, t, kvw), F32)],
        compiler_params=_params(("parallel",), 2 * t * (qw * 6 + 4 * kvw * 4)),
        name="attn_ctx",
    )(p, p, p, sink)


def _band_bias():
    Q = QBLOCK
    r = np.arange(Q)[:, None]
    c = np.arange(3 * Q)[None, :]
    band = (c >= r) & (c <= r + 2 * Q)
    out = []
    for kind in range(3):
        ok = band.copy()
        if kind == 0:
            ok &= c >= Q
        if kind == 2:
            ok &= c < 2 * Q
        out.append(np.where(ok, 0.0, NEG).astype(np.float32))
    return jnp.asarray(np.stack(out))


def _half_variants(x):
    low = lax.broadcasted_iota(jnp.int32, x.shape, 1) < ATT_HEAD_DIM
    sw = pltpu.roll(x, ATT_HEAD_DIM, 1)
    zero = jnp.zeros_like(x)
    return [jnp.where(low, a, b).astype(BF16) for a, b in ((x, zero), (zero, sw), (sw, zero), (zero, x))]


def _attn_lat_kernel(t, q_ref, k_ref, v_ref, kc_ref, vc_ref, cq_ref, saq_ref, sbq_ref,
                     ck_ref, sak_ref, sbk_ref, bias_ref, sink_ref, o_ref, kp_scr, vp_scr, kc_scr, vc_scr):
    n = pl.program_id(1)
    Q = QBLOCK
    hd = ATT_HEAD_DIM
    quarter = hd // 4
    nvar = 2 * ATT_KV_HEADS

    @pl.when(n == 0)
    def _():
        zero = jnp.zeros((Q, LANE), BF16)
        for vi in range(nvar):
            kp_scr[vi, 0:Q, :] = zero
            vp_scr[vi, 0:Q, :] = zero
            kp_scr[vi, Q + t:2 * Q + t, :] = zero
            vp_scr[vi, Q + t:2 * Q + t, :] = zero

        def fill(c, _):
            rows = _chunk_rows(c, Q)
            dst = pl.ds(pl.multiple_of(c * Q + Q, Q), Q)
            kk = _rope(k_ref[rows, :], ck_ref[rows, :], sak_ref[rows, :], sbk_ref[rows, :], quarter)
            for vi, (kv, vv) in enumerate(zip(_half_variants(kk), _half_variants(v_ref[rows, :]))):
                kp_scr[vi, dst, :] = kv
                vp_scr[vi, dst, :] = vv
            return 0

        lax.fori_loop(0, t // Q, fill, 0)
        for vi, (kv, vv) in enumerate(zip(_half_variants(kc_ref[...]), _half_variants(vc_ref[...]))):
            kc_scr[vi] = kv
            vc_scr[vi] = vv

    band = pl.ds(pl.multiple_of(n * Q, Q), 3 * Q)
    cq, saq, sbq = cq_ref[...], saq_ref[...], sbq_ref[...]
    bias = bias_ref[...]
    scale = hd ** -0.5
    heads_per_chunk = LANE // hd
    for c in range(ATT_HEADS // heads_per_chunk):
        sl = slice(c * LANE, (c + 1) * LANE)
        qc = (_rope(q_ref[:, sl], cq, saq, sbq, quarter) * scale).astype(BF16)
        acc = None
        for half in range(heads_per_chunk):
            hx = c * heads_per_chunk + half
            vi = 2 * (hx // ATT_GROUPS) + half
            s_loc = _dot_nt(qc, kp_scr[vi, band, :]) + bias
            s_ctx = _dot_nt(qc, kc_scr[vi])
            sink = jnp.full((1, 1), sink_ref[hx], F32)
            o = _softmax_sink_pv([s_loc, s_ctx], [vp_scr[vi, band, :], vc_scr[vi]], sink)
            acc = o if acc is None else acc + o
        o_ref[:, sl] = acc.astype(BF16)


def _attn_lat(p, nb, t, kc, vc, layer, tabs, sink):
    assert (ATT_HEAD_DIM ** -0.5) == 2.0 ** round(np.log2(ATT_HEAD_DIM ** -0.5))
    qw = ATT_HEADS * ATT_HEAD_DIM
    kvw = ATT_KV_HEADS * ATT_HEAD_DIM
    assert kvw == LANE
    nq = t // QBLOCK
    past = kc.shape[2]
    nvar = 2 * ATT_KV_HEADS
    cos, sa, sb = tabs
    bias = _band_bias()
    tq = pl.BlockSpec((QBLOCK, LANE), lambda b, n: (n, 0))
    tk = pl.BlockSpec((t, LANE), lambda b, n: (0, 0))
    return pl.pallas_call(
        functools.partial(_attn_lat_kernel, t),
        grid=(nb, nq),
        in_specs=[
            pl.BlockSpec((QBLOCK, qw), lambda b, n: (b * nq + n, COL_B_Q * LANE // qw)),
            pl.BlockSpec((t, kvw), lambda b, n: (b, COL_B_K)),
            pl.BlockSpec((t, kvw), lambda b, n: (b, COL_B_V)),
            pl.BlockSpec((None, None, past, kvw), lambda b, n: (b, layer, 0, 0)),
            pl.BlockSpec((None, None, past, kvw), lambda b, n: (b, layer, 0, 0)),
            tq, tq, tq, tk, tk, tk,
            pl.BlockSpec((None,) + bias.shape[1:],
                         lambda b, n: (jnp.where(n == 0, 0, jnp.where(n == nq - 1, 2, 1)), 0, 0)),
            pl.BlockSpec(memory_space=pltpu.SMEM),
        ],
        out_specs=pl.BlockSpec((QBLOCK, qw), lambda b, n: (b * nq + n, 0)),
        out_shape=jax.ShapeDtypeStruct((nb * t, qw), BF16),
        scratch_shapes=[pltpu.VMEM((nvar, t + 2 * QBLOCK, LANE), BF16), pltpu.VMEM((nvar, t + 2 * QBLOCK, LANE), BF16),
                        pltpu.VMEM((nvar, past, LANE), BF16), pltpu.VMEM((nvar, past, LANE), BF16)],
        compiler_params=_params(("parallel", "arbitrary"),
                                2 * 5 * t * LANE * 4 + 2 * nvar * (t + 2 * QBLOCK + past) * LANE * 2 + 2 * MIB),
        name="attn_lat",
    )(p, p, p, kc, vc, cos, sa, sb, cos, sa, sb, bias, sink)


def _conv_kernel(t, cb_ref, cc_ref, cx_ref, w_ref, o_ref):
    u = cc_ref[...] * cx_ref[...]
    tok = lax.broadcasted_iota(jnp.int32, u.shape, 0)
    prev = jnp.where(tok == 0, 0.0, pltpu.roll(u, 1, 0))
    nxt = jnp.where(tok == t - 1, 0.0, pltpu.roll(u, t - 1, 0))
    o_ref[...] = (cb_ref[...] * (w_ref[0:1, :] * prev + w_ref[1:2, :] * u + w_ref[2:3, :] * nxt)).astype(BF16)


def _conv(p, nb, t, row_off, w):
    assert row_off % t == 0
    ob = row_off // t
    nblk = w.shape[1] // LANE

    def col(cb):
        return pl.BlockSpec((t, LANE), lambda b, j: (ob + b, cb + j))

    return pl.pallas_call(
        functools.partial(_conv_kernel, t),
        grid=(nb, nblk),
        in_specs=[col(COL_C_B), col(COL_C_C), col(COL_C_X),
                  pl.BlockSpec((w.shape[0], LANE), lambda b, j: (0, j))],
        out_specs=pl.BlockSpec((t, LANE), lambda b, j: (b, j)),
        out_shape=jax.ShapeDtypeStruct((nb * t, w.shape[1]), BF16),
        compiler_params=_params(("parallel", "parallel"), 2 * 4 * t * LANE * 4),
        name="conv",
    )(p, p, p, w)


GATE_SRC = 4 * MIX_HEADS * MIX_DIM
N_GATES = 4 * MIX_HEADS


def _gate_lanes(x):
    g = x.reshape(x.shape[:-1] + (2, 2, MIX_HEADS))
    pad = [(0, 0)] * (x.ndim - 1) + [(0, LANE - 2 * MIX_HEADS)]
    blocks = [jnp.pad(g[..., kind, :].reshape(x.shape[:-1] + (2 * MIX_HEADS,)), pad) for kind in range(2)]
    return jnp.concatenate(blocks, axis=-1)


def _column_slabs(w, tile):
    k, n = w.shape[-2:]
    return jnp.swapaxes(w.reshape(w.shape[:-1] + (n // tile, tile)), -3, -2)


def _stage_mixer_proj(w_in):
    head = w_in[..., 0:GATE_SRC]
    tail = w_in[..., GATE_SRC + N_GATES:]
    gates = _gate_lanes(w_in[..., GATE_SRC:GATE_SRC + N_GATES])
    assert head.shape[-1] + tail.shape[-1] == COL_GATES * LANE
    assert head.shape[-1] + tail.shape[-1] + gates.shape[-1] == PROJ_P
    return jnp.concatenate([head, tail, gates], axis=-1).astype(BF16)


def kernel(x_prompt, x_sample, cache_attn_k, cache_attn_v, state_mlstm_C, state_mlstm_n, state_mlstm_m,
           state_ret_S, c, c_ctx, w_mod, b_mod, norm_g, ffn1_w_in, ffn1_w_out, ffn2_w_in, ffn2_w_out,
           w_in, mlstm_gate_b, mlstm_norm_g, attn_sink, conv_w, ret_log_decay, ret_norm_g,
           w_branch, w_gate, w_out):
    bp, tp, d = x_prompt.shape
    bs, ts, _ = x_sample.shape
    depth = w_mod.shape[0]
    past = cache_attn_k.shape[2]
    kvw = ATT_KV_HEADS * ATT_HEAD_DIM
    fits = [t for t in (1024, 512, 256) if ts % t == 0 and (bp * tp) % t == 0]
    rows = _Rows(bs, ts, bp, tp, fits[0])
    rows_ffn = _Rows(bs, ts, bp, tp, fits[min(1, len(fits) - 1)])

    nbp = -(-(1 + bs) // 8) * 8
    cpad = jnp.concatenate([c_ctx[None], c, jnp.zeros((nbp - 1 - bs, d), F32)], axis=0)
    mod = _modulation(cpad, w_mod, b_mod).reshape(depth, nbp, 3, 3, d)

    tf = _pick_tile(ffn1_w_out.shape[1], 512)
    w_p = _column_slabs(_stage_mixer_proj(w_in), _pick_tile(PROJ_P, 512))
    ffn1_in, ffn1_out = _column_slabs(ffn1_w_in.astype(BF16), tf), ffn1_w_out.astype(BF16)
    ffn2_in, ffn2_out = _column_slabs(ffn2_w_in.astype(BF16), tf), ffn2_w_out.astype(BF16)
    tn = _pick_tile(d, 256)
    wg_b = jnp.swapaxes(_column_slabs(w_gate.astype(BF16), tn), 1, 2)
    wb_b = jnp.swapaxes(_column_slabs(w_branch.astype(BF16), tn), 1, 2)
    wo_b = w_out.astype(BF16)

    gate_bias = _gate_lanes(mlstm_gate_b.reshape(depth, 1, N_GATES))
    tabs_att = _rope_tables(ts, ATT_HEAD_DIM)
    tabs_ret = _rope_tables(ts, MIX_DIM)

    x = jnp.concatenate([x_sample.reshape(bs * ts, d), x_prompt.reshape(bp * tp, d)], axis=0)
    ms = bs * ts
    kc_all = cache_attn_k.reshape(bs, depth, past, kvw)
    vc_all = cache_attn_v.reshape(bs, depth, past, kvw)
    ks_, vs_, cs_, ns_, mms_, ss_ = [], [], [], [], [], []
    for l in range(depth):
        g = norm_g[l]
        x = _ffn(rows_ffn, x, mod[l, :, 0], g[0:2], ffn1_in, ffn1_out, l)

        p, h = _inproj(rows, x, mod[l, :, 1], g[2:4], w_p, l)
        a_gain = mlstm_norm_g[l][None]
        d_gain = ret_norm_g[l][None]
        lat_a = _mlstm(p, bs, ts, 0, gate_bias[l], a_gain, (state_mlstm_C, state_mlstm_n, state_mlstm_m, l))[0]
        lat_b = _attn_lat(p, bs, ts, kc_all, vc_all, l, tabs_att, attn_sink[l])
        lat_c = _conv(p, bs, ts, 0, conv_w[l])
        lat_d = _retention(p, bs, ts, 0, ret_log_decay[l], d_gain, (tabs_ret, state_ret_S, l))[0]
        ctx_a, c_fin, n_fin, m_fin = _mlstm(p, bp, tp, ms, gate_bias[l], a_gain, None)
        ctx_b, k_new, v_new = _attn_ctx(p, bp, tp, ms, attn_sink[l])
        ctx_c = _conv(p, bp, tp, ms, conv_w[l])
        ctx_d, s_fin = _retention(p, bp, tp, ms, ret_log_decay[l], d_gain, None)
        y = _merge(rows, h, (lat_a, lat_b, lat_c, lat_d), (ctx_a, ctx_b, ctx_c, ctx_d), wg_b, wb_b, l)
        x = _outproj(rows_ffn, x, mod[l, :, 1], g[2:4], y, wo_b, l)

        x = _ffn(rows_ffn, x, mod[l, :, 2], g[4:6], ffn2_in, ffn2_out, l)

        ks_.append(k_new.reshape(bp, tp, ATT_KV_HEADS, ATT_HEAD_DIM))
        vs_.append(v_new.reshape(bp, tp, ATT_KV_HEADS, ATT_HEAD_DIM))
        cs_.append(c_fin)
        ns_.append(n_fin)
        mms_.append(m_fin[:, :, 0].reshape(bp, 2, MIX_HEADS))
        ss_.append(s_fin)

    y_sample = x[:ms].reshape(bs, ts, d)
    y_prompt = x[ms:].reshape(bp, tp, d)
    return (y_prompt, y_sample, jnp.stack(ks_, axis=1), jnp.stack(vs_, axis=1), jnp.stack(cs_, axis=1),
            jnp.stack(ns_, axis=1), jnp.stack(mms_, axis=1), jnp.stack(ss_, axis=1))
```

```python
import functools

import numpy as np
import jax
import jax.numpy as jnp
from jax import lax
from jax.experimental import pallas as pl
from jax.experimental.pallas import tpu as pltpu

F32 = jnp.float32
BF16 = jnp.bfloat16

N_MOD = 9
N_BRANCH = 4
BRANCH_W = 512
MIX_HEADS = 4
MIX_DIM = 128
RET_GROUP = 2
ATT_HEADS = 8
ATT_KV_HEADS = 2
ATT_HEAD_DIM = 64
ATT_GROUPS = ATT_HEADS // ATT_KV_HEADS
QBLOCK = 128
CHUNK = 128
GRID_W = 64
ROPE_BASE = 10000.0
EPS = 1e-6
NEG = -1e30

LANE = 128
COL_A_Q, COL_A_K, COL_A_V, COL_A_O = 0, 4, 8, 12
COL_B_Q, COL_B_K, COL_B_V = 16, 20, 21
COL_C_B, COL_C_C, COL_C_X = 22, 26, 30
COL_D_Q, COL_D_K, COL_D_V, COL_D_G = 34, 38, 42, 46
COL_GATES = 50
PROJ_BLOCKS = 52
PROJ_P = PROJ_BLOCKS * LANE

MIB = 1024 * 1024
VMEM_CAP = 44 * MIB
VMEM_TEMPS = 12 * MIB
SINGLE = pl.Buffered(1)


def _params(sem, buffers):
    return pltpu.CompilerParams(dimension_semantics=sem,
                                vmem_limit_bytes=int(min(buffers + VMEM_TEMPS, VMEM_CAP)))


def _pick_tile(n, pref):
    if n <= pref:
        return n
    t = (pref // LANE) * LANE
    while t >= LANE:
        if n % t == 0:
            return t
        t -= LANE
    raise ValueError(f"no lane-aligned tile divides {n}")


def _dot(a, b):
    return jnp.dot(a, b, preferred_element_type=F32)


def _dot_nt(a, b):
    return lax.dot_general(a, b, (((1,), (1,)), ((), ())), preferred_element_type=F32)


def _dot_tn(a, b):
    return lax.dot_general(a, b, (((0,), (0,)), ((), ())), preferred_element_type=F32)


def _sigmoid(x):
    return 1.0 / (1.0 + jnp.exp(-x))


def _rms(xf, g):
    ms = jnp.mean(xf * xf, axis=-1, keepdims=True)
    return xf * lax.rsqrt(ms + EPS) * g


def _modulated(x_ref, mod_ref, g_ref):
    return _rms(x_ref[...], g_ref[0:1, :]) * (1.0 + mod_ref[1:2, :]) + mod_ref[0:1, :]


def _head_norm(x, g):
    mu = jnp.mean(x, axis=-1, keepdims=True)
    xc = x - mu
    var = jnp.mean(xc * xc, axis=-1, keepdims=True)
    return xc * lax.rsqrt(var + EPS) * g


def _chunk_rows(c, n=CHUNK):
    return pl.ds(pl.multiple_of(c * n, n), n)


def _mod_kernel(c_ref, w_ref, b_ref, o_ref):
    c = c_ref[...]
    s = (c * _sigmoid(c)).astype(BF16)
    o_ref[...] = _dot(s, w_ref[...].astype(BF16)) + b_ref[...]


def _modulation(cpad, w_mod, b_mod):
    depth, d, n = w_mod.shape
    nbp = cpad.shape[0]
    tn = _pick_tile(n, 1024)
    return pl.pallas_call(
        _mod_kernel,
        grid=(depth, n // tn),
        in_specs=[
            pl.BlockSpec((nbp, d), lambda l, j: (0, 0)),
            pl.BlockSpec((None, d, tn), lambda l, j: (l, 0, j)),
            pl.BlockSpec((None, 1, tn), lambda l, j: (l, 0, j)),
        ],
        out_specs=pl.BlockSpec((None, nbp, tn), lambda l, j: (l, 0, j)),
        out_shape=jax.ShapeDtypeStruct((depth, nbp, n), F32),
        compiler_params=_params(("parallel", "parallel"), 2 * d * tn * 4 + MIB),
        name="modulation",
    )(cpad, w_mod, b_mod.reshape(depth, 1, n))


class _Rows:
    def __init__(self, bs, ts, bp, tp, tm):
        self.bs, self.ts, self.bp, self.tp = bs, ts, bp, tp
        self.ms, self.mp = bs * ts, bp * tp
        self.m = self.ms + self.mp
        assert ts % tm == 0 and self.mp % tm == 0 and self.ms % tp == 0
        self.tm = tm
        self.lat_tiles = self.ms // tm
        self.tiles_per_seq = ts // tm

    def mod_index(self, i):
        return jnp.where(i < self.lat_tiles, 1 + i // self.tiles_per_seq, 0)

    def x_spec(self, d, mode=None):
        return pl.BlockSpec((self.tm, d), lambda i, *_: (i, 0), pipeline_mode=mode)

    def mod_spec(self, d):
        return pl.BlockSpec((None, 3, d), lambda i, *_: (self.mod_index(i), 0, 0))


def _layer_spec(block, index_map, l, mode=None):
    return pl.BlockSpec((None,) + tuple(block), lambda *ids: (l,) + tuple(index_map(*ids)), pipeline_mode=mode)


def _ffn_kernel(nf, x_ref, mod_ref, g_ref, wg_ref, wu_ref, wo_ref, o_ref, h_scr, acc_scr):
    f = pl.program_id(1)

    def hidden_slab(h):
        gate = _dot(h, wg_ref[...])
        up = _dot(h, wu_ref[...])
        a = (gate * _sigmoid(gate)) * up
        return _dot(a.astype(BF16), wo_ref[...])

    def pre():
        h = _modulated(x_ref, mod_ref, g_ref).astype(BF16)
        h_scr[...] = h
        return h

    def post(z):
        o_ref[...] = x_ref[...] + (0.5 * mod_ref[2:3, :]) * _rms(z, g_ref[1:2, :])

    if nf == 1:
        post(hidden_slab(pre()))
        return

    @pl.when(f == 0)
    def _():
        acc_scr[...] = hidden_slab(pre())

    @pl.when((f > 0) & (f < nf - 1))
    def _():
        acc_scr[...] += hidden_slab(h_scr[...])

    @pl.when(f == nf - 1)
    def _():
        post(acc_scr[...] + hidden_slab(h_scr[...]))


def _ffn(rows, x, mod_s, g2, w_in, w_out, l):
    m, d = x.shape
    dff = w_out.shape[1]
    tm = rows.tm
    tf = _pick_tile(dff, 512)
    nf = dff // tf
    buffers = tm * d * (2 * 4 + 2 * 4 + 4 + 2) + 3 * 2 * d * tf * 2
    return pl.pallas_call(
        functools.partial(_ffn_kernel, nf),
        grid=(m // tm, nf),
        in_specs=[
            rows.x_spec(d),
            rows.mod_spec(d),
            pl.BlockSpec((2, d), lambda i, f: (0, 0)),
            _layer_spec((d, tf), lambda i, f: (0, f), l),
            _layer_spec((d, tf), lambda i, f: (0, f + nf), l),
            _layer_spec((tf, d), lambda i, f: (f, 0), l),
        ],
        out_specs=pl.BlockSpec((tm, d), lambda i, f: (i, 0)),
        out_shape=jax.ShapeDtypeStruct((m, d), F32),
        scratch_shapes=[pltpu.VMEM((tm, d), BF16), pltpu.VMEM((tm, d), F32)],
        compiler_params=_params(("parallel", "arbitrary"), buffers),
        name="ffn",
    )(x, mod_s, g2, w_in, w_in, w_out)


def _inproj_kernel(x_ref, mod_ref, g_ref, w_ref, p_ref, h_ref):
    @pl.when(pl.program_id(1) == 0)
    def _():
        h = _modulated(x_ref, mod_ref, g_ref).astype(BF16)
        h_ref[...] = h
        p_ref[...] = _dot(h, w_ref[...])

    @pl.when(pl.program_id(1) > 0)
    def _():
        p_ref[...] = _dot(h_ref[...], w_ref[...])


def _inproj(rows, x, mod_s, g2, w_p, l):
    m, d = x.shape
    n = w_p.shape[2]
    tn = _pick_tile(n, 512)
    nt = n // tn
    tm = rows.tm
    return pl.pallas_call(
        _inproj_kernel,
        grid=(m // tm, nt),
        in_specs=[
            rows.x_spec(d),
            rows.mod_spec(d),
            pl.BlockSpec((2, d), lambda i, j: (0, 0)),
            _layer_spec((d, tn), lambda i, j: (0, j), l),
        ],
        out_specs=[pl.BlockSpec((tm, tn), lambda i, j: (i, j)),
                   pl.BlockSpec((tm, d), lambda i, j: (i, 0))],
        out_shape=[jax.ShapeDtypeStruct((m, n), F32), jax.ShapeDtypeStruct((m, d), BF16)],
        compiler_params=_params(("parallel", "arbitrary"),
                                tm * d * (2 * 4 + 2 * 2) + 2 * d * tn * 2 + 2 * tm * tn * 4),
        name="inproj",
    )(x, mod_s, g2, w_p)


def _merge_kernel(lat_tiles, h_ref, *refs):
    lat_refs, ctx_refs = refs[0:N_BRANCH], refs[N_BRANCH:2 * N_BRANCH]
    wg_ref, wb_ref, y_ref = refs[2 * N_BRANCH:]
    h = h_ref[...]
    is_lat = pl.program_id(0) < lat_tiles
    y = None
    for i in range(N_BRANCH):
        o = jnp.where(is_lat, lat_refs[i][...], ctx_refs[i][...])
        term = _sigmoid(_dot(h, wg_ref[i])) * _dot(o, wb_ref[i])
        y = term if y is None else y + term
    y_ref[...] = y.astype(BF16)


def _merge(rows, h, lat, ctx, w_gate, w_branch, l):
    m, d = h.shape
    tm = rows.tm
    nt, _, bw, tn = w_branch.shape[1:]
    assert w_gate.shape[1:] == (nt, N_BRANCH, d, tn)
    last_lat = rows.lat_tiles - 1
    lat_spec = pl.BlockSpec((tm, bw), lambda i, j: (jnp.minimum(i, last_lat), 0))
    ctx_spec = pl.BlockSpec((tm, bw), lambda i, j: (jnp.maximum(i - rows.lat_tiles, 0), 0))
    return pl.pallas_call(
        functools.partial(_merge_kernel, rows.lat_tiles),
        grid=(m // tm, nt),
        in_specs=[
            rows.x_spec(d),
            *([lat_spec] * N_BRANCH), *([ctx_spec] * N_BRANCH),
            _layer_spec((None, N_BRANCH, d, tn), lambda i, j: (j, 0, 0, 0), l),
            _layer_spec((None, N_BRANCH, bw, tn), lambda i, j: (j, 0, 0, 0), l),
        ],
        out_specs=pl.BlockSpec((tm, tn), lambda i, j: (i, j)),
        out_shape=jax.ShapeDtypeStruct((m, d), BF16),
        compiler_params=_params(("parallel", "arbitrary"),
                                2 * tm * d * 2 + 2 * 2 * N_BRANCH * tm * bw * 2
                                + 2 * N_BRANCH * (d + bw) * tn * 2 + 2 * tm * tn * 2),
        name="merge",
    )(h, *lat, *ctx, w_gate, w_branch)


def _outproj_kernel(x_ref, mod_ref, g_ref, y_ref, w_ref, o_ref):
    z = _dot(y_ref[...], w_ref[...])
    o_ref[...] = x_ref[...] + mod_ref[2:3, :] * _rms(z, g_ref[1:2, :])


def _outproj(rows, x, mod_s, g2, y, w_out, l):
    m, d = x.shape
    tm = rows.tm
    return pl.pallas_call(
        _outproj_kernel,
        grid=(m // tm,),
        in_specs=[
            rows.x_spec(d),
            rows.mod_spec(d),
            pl.BlockSpec((2, d), lambda i: (0, 0)),
            rows.x_spec(d),
            _layer_spec((d, d), lambda i: (0, 0), l, SINGLE),
        ],
        out_specs=pl.BlockSpec((tm, d), lambda i: (i, 0)),
        out_shape=jax.ShapeDtypeStruct((m, d), F32),
        compiler_params=_params(("parallel",), tm * d * (2 * 4 + 2 * 4 + 4 + 2 * 2) + d * d * 2),
        name="outproj",
    )(x, mod_s, g2, y, w_out)


def _rope_tables(t, d):
    rows = t // GRID_W
    row = jnp.repeat(jnp.arange(rows), GRID_W).astype(F32)
    col = jnp.tile(jnp.arange(GRID_W), rows).astype(F32)
    quarter = d // 4
    inv = jnp.power(ROPE_BASE, -jnp.arange(quarter, dtype=F32) / quarter)
    ar = row[:, None] * inv
    ac = col[:, None] * inv
    ang = jnp.concatenate([ar, ar, ac, ac], axis=-1)
    cos, sin = jnp.cos(ang), jnp.sin(ang)
    first = (jnp.arange(d) % (2 * quarter)) < quarter
    sa = jnp.where(first, -sin, 0.0)
    sb = jnp.where(first, 0.0, sin)
    reps = LANE // d
    return tuple(jnp.tile(a, (1, reps)) for a in (cos, sa, sb))


def _rope(x, cos, sa, sb, quarter):
    n = x.shape[-1]
    ahead = pltpu.roll(x, n - quarter, 1)
    behind = pltpu.roll(x, quarter, 1)
    return x * cos + ahead * sa + behind * sb


def _log_sigmoid(x):
    return jnp.minimum(x, 0.0) - jnp.log1p(jnp.exp(-jnp.abs(x)))


def _split_bf16(x):
    hi = x.astype(BF16)
    return hi, (x - hi.astype(F32)).astype(BF16)


def _mlstm_kernel(t, layer, emit_state, *refs):
    has_state = layer is not None
    it = iter(refs)
    q_ref, k_ref, v_ref, og_ref, gt_ref, bias_ref, gain_ref = (next(it) for _ in range(7))
    if has_state:
        c0_ref, n0_ref, m0_ref = next(it), next(it), next(it)
    o_ref = next(it)
    if emit_state:
        cout_ref, nout_ref, mout_ref = next(it), next(it), next(it)
    h_scr, s_scr = next(it), next(it)

    L, D, H = CHUNK, MIX_DIM, MIX_HEADS
    nc = t // L
    r_i = lax.broadcasted_iota(jnp.int32, (L, L), 0)
    c_i = lax.broadcasted_iota(jnp.int32, (L, L), 1)
    masks = (r_i >= c_i, r_i <= c_i)
    cum = tuple(mk.astype(BF16) for mk in masks)
    ones = jnp.ones((L, D), BF16)
    sel = [(r_i == g).astype(BF16) for g in range(2 * H)]
    bias = bias_ref[...]
    scale = D ** -0.5

    ms0 = []
    for d in range(2):
        for h in range(H):
            if has_state:
                s_scr[d, h, :, 0:D] = c0_ref[d, h]
                s_scr[d, h, :, D:2 * D] = jnp.broadcast_to(n0_ref[d, h:h + 1, :], (D, D)).T
                ms0.append(jnp.full((1, D), m0_ref[pl.program_id(0), layer, d, h], F32))
            else:
                s_scr[d, h] = jnp.zeros((D, 2 * D), F32)
                ms0.append(jnp.zeros((1, D), F32))

    def zero_rows(c, _):
        h_scr[_chunk_rows(c), :] = jnp.zeros((L, H * D), F32)
        return 0

    lax.fori_loop(0, nc, zero_rows, 0)

    def chunk_dir(c, d, ms):
        rows = _chunk_rows(c)
        g = gt_ref[rows, :] + bias
        lf_hi, lf_lo = _split_bf16(_log_sigmoid(g[:, LANE:2 * LANE]))
        f_all = _dot(cum[d], lf_hi) + _dot(cum[d], lf_lo)
        b_all = g[:, 0:LANE] - f_all
        b_t = b_all.T
        f_hi, f_lo = _split_bf16(f_all)
        b_hi, b_lo = _split_bf16(b_all)
        out = []
        for h in range(H):
            j = d * H + h
            cols = slice(h * D, (h + 1) * D)
            q = q_ref[rows, cols]
            k = k_ref[rows, cols] * scale
            qb = q.astype(BF16)
            v_aug = jnp.concatenate([v_ref[rows, cols].astype(BF16), ones], axis=1)
            f_rep = _dot(f_hi, sel[j]) + _dot(f_lo, sel[j])
            b_rep = _dot(b_hi, sel[j]) + _dot(b_lo, sel[j])
            brow = b_t[j:j + 1, :]
            f_last = f_rep[L - 1:L, :] if d == 0 else f_rep[0:1, :]
            m = ms[j]
            lw = jnp.where(masks[d], f_rep + brow, NEG)
            inter = f_rep + m
            m_tok = jnp.maximum(jnp.max(lw, axis=-1, keepdims=True), inter)
            a = _dot_nt(qb, k.astype(BF16)) * jnp.exp(lw - m_tok)
            w_inter = jnp.exp(inter - m_tok)
            s_old = s_scr[d, h]
            pv = _dot(a.astype(BF16), v_aug)
            qs = _dot(qb, s_old.astype(BF16))
            num = pv[:, 0:D] + w_inter * qs[:, 0:D]
            den = pv[:, D:2 * D] + w_inter * qs[:, D:2 * D]
            hh = num * (1.0 / jnp.maximum(jnp.abs(den), jnp.exp(-m_tok)))
            h_scr[rows, cols] += hh
            bmax = jnp.max(b_rep, axis=0, keepdims=True)
            a_end = f_last + bmax
            kw = k * jnp.exp(b_rep - bmax)
            u_aug = _dot_tn(kw.astype(BF16), v_aug)
            m_new = jnp.maximum(f_last + m, a_end)
            d_old = jnp.exp(f_last + m - m_new)
            d_new = jnp.exp(a_end - m_new)
            for half in (slice(0, D), slice(D, 2 * D)):
                s_scr[d, h, :, half] = d_old * s_old[:, half] + d_new * u_aug[:, half]
            out.append(m_new)
        return out

    def body(c, ms):
        mf = chunk_dir(c, 0, ms)
        mb = chunk_dir(nc - 1 - c, 1, ms)
        return tuple(mf + mb)

    ms = lax.fori_loop(0, nc, body, tuple(ms0))

    gain = gain_ref[...]

    def finish(c, _):
        rows = _chunk_rows(c)
        for h in range(H):
            cols = slice(h * D, (h + 1) * D)
            y = _head_norm(h_scr[rows, cols], gain[:, cols])
            o_ref[rows, cols] = (_sigmoid(og_ref[rows, cols]) * y).astype(BF16)
        return 0

    lax.fori_loop(0, nc, finish, 0)

    if emit_state:
        for d in range(2):
            for h in range(H):
                cout_ref[d, h] = s_scr[d, h, :, 0:D]
                nout_ref[d, h:h + 1, :] = s_scr[d, h, :, D:2 * D].T[0:1, :]
                mout_ref[d * H + h:d * H + h + 1, :] = ms[d * H + h]


def _mlstm(p, nb, t, row_off, bias, gain, state):
    assert row_off % t == 0
    ob = row_off // t
    has_state = state is not None
    emit_state = not has_state
    hw = MIX_HEADS * MIX_DIM
    st_c = pl.BlockSpec((None, 2, MIX_HEADS, MIX_DIM, MIX_DIM), lambda b: (b, 0, 0, 0, 0))
    st_n = pl.BlockSpec((None, 2, MIX_HEADS, MIX_DIM), lambda b: (b, 0, 0, 0))

    def col(cb):
        return pl.BlockSpec((t, hw), lambda b: (ob + b, cb * LANE // hw), pipeline_mode=mode)

    mode = SINGLE if t * hw * 4 >= 2 * MIB else None
    nbuf = 1 if mode is SINGLE else 2
    buffers = (nbuf * (4 * t * hw + t * 2 * LANE) * 4 + 2 * t * hw * 2 + t * hw * 4
               + 6 * 2 * MIX_HEADS * MIX_DIM * MIX_DIM * 4)
    in_specs = [col(COL_A_Q), col(COL_A_K), col(COL_A_V), col(COL_A_O),
                pl.BlockSpec((t, 2 * LANE), lambda b: (ob + b, COL_GATES // 2), pipeline_mode=mode),
                pl.BlockSpec((1, 2 * LANE), lambda b: (0, 0)),
                pl.BlockSpec((1, hw), lambda b: (0, 0))]
    args = [p, p, p, p, p, bias, gain]
    layer = None
    if has_state:
        layer = state[3]
        in_specs += [pl.BlockSpec((None, None, 2, MIX_HEADS, MIX_DIM, MIX_DIM), lambda b: (b, layer, 0, 0, 0, 0)),
                     pl.BlockSpec((None, None, 2, MIX_HEADS, MIX_DIM), lambda b: (b, layer, 0, 0, 0)),
                     pl.BlockSpec(memory_space=pltpu.SMEM)]
        args += list(state[0:3])
    out_shape = [jax.ShapeDtypeStruct((nb * t, hw), BF16)]
    out_specs = [pl.BlockSpec((t, hw), lambda b: (b, 0))]
    if emit_state:
        out_shape += [jax.ShapeDtypeStruct((nb, 2, MIX_HEADS, MIX_DIM, MIX_DIM), F32),
                      jax.ShapeDtypeStruct((nb, 2, MIX_HEADS, MIX_DIM), F32),
                      jax.ShapeDtypeStruct((nb, 2 * MIX_HEADS, MIX_DIM), F32)]
        out_specs += [st_c, st_n, pl.BlockSpec((None, 2 * MIX_HEADS, MIX_DIM), lambda b: (b, 0, 0))]
    return pl.pallas_call(
        functools.partial(_mlstm_kernel, t, layer, emit_state),
        grid=(nb,),
        in_specs=in_specs,
        out_specs=out_specs,
        out_shape=out_shape,
        scratch_shapes=[pltpu.VMEM((t, hw), F32),
                        pltpu.VMEM((2, MIX_HEADS, MIX_DIM, 2 * MIX_DIM), F32)],
        compiler_params=_params(("parallel",), buffers),
        name="mlstm_lat" if has_state else "mlstm_ctx",
    )(*args)


def _ret_kernel(t, has_state, emit_state, *refs):
    it = iter(refs)
    q_ref, k_ref, v_ref, dg_ref, lg_ref, gain_ref = (next(it) for _ in range(6))
    if has_state:
        cos_ref, sa_ref, sb_ref, s0_ref = (next(it) for _ in range(4))
    o_ref = next(it)
    if emit_state:
        sout_ref = next(it)
    q_scr, u_scr, o_scr, s_scr = (next(it) for _ in range(4))

    L, D, G = CHUNK, MIX_DIM, RET_GROUP
    nc = t // L
    diff = (lax.broadcasted_iota(jnp.int32, (L, L), 0) - lax.broadcasted_iota(jnp.int32, (L, L), 1)).astype(F32)
    tcol = lax.broadcasted_iota(jnp.int32, (L, 1), 0).astype(F32)
    scale = D ** -0.5
    dm, w_end, w_cross, decay = [], [], [], []
    for j in range(G):
        hx = pl.program_id(1) * G + j
        lgs = (lg_ref[0, hx], lg_ref[1, hx])
        dm.append(jnp.where(diff >= 0.0, jnp.exp(jnp.maximum(diff, 0.0) * lgs[0]), 0.0)
                  + jnp.where(diff <= 0.0, jnp.exp(jnp.maximum(-diff, 0.0) * lgs[1]), 0.0))
        w_end.append((jnp.exp((L - 1.0 - tcol) * lgs[0]), jnp.exp(tcol * lgs[1])))
        w_cross.append((jnp.exp((tcol + 1.0) * lgs[0]), jnp.exp((L - tcol) * lgs[1])))
        decay.append(tuple(jnp.exp(jnp.full((1, 1), float(L), F32) * lgs[d]) for d in range(2)))
        for d in range(2):
            s_scr[d, j] = s0_ref[d, j] if has_state else jnp.zeros((D, D), F32)

    def local(c, _):
        rows = _chunk_rows(c)
        if has_state:
            cos, sa, sb = cos_ref[rows, :], sa_ref[rows, :], sb_ref[rows, :]
        for j in range(G):
            cols = slice(j * D, (j + 1) * D)
            q = q_ref[rows, cols]
            k = k_ref[rows, cols] * scale
            if has_state:
                q = _rope(q, cos, sa, sb, D // 4)
                k = _rope(k, cos, sa, sb, D // 4)
            qb = q.astype(BF16)
            vb = v_ref[rows, cols].astype(BF16)
            q_scr[rows, cols] = qb
            o_scr[rows, cols] = _dot((_dot_nt(qb, k.astype(BF16)) * dm[j]).astype(BF16), vb)
            kw = jnp.concatenate([(k * w_end[j][0]).astype(BF16), (k * w_end[j][1]).astype(BF16)], axis=1)
            u_scr[c, j] = _dot_tn(kw, vb)
        return 0

    lax.fori_loop(0, nc, local, 0, unroll=2)

    def scan(c, _):
        for d, cc in ((0, c), (1, nc - 1 - c)):
            rows = _chunk_rows(cc)
            for j in range(G):
                cols = slice(j * D, (j + 1) * D)
                s_old = s_scr[d, j]
                o_scr[rows, cols] += _dot(q_scr[rows, cols], s_old.astype(BF16)) * w_cross[j][d]
                s_scr[d, j] = decay[j][d] * s_old + u_scr[cc, j, d * D:(d + 1) * D, :]
        return 0

    lax.fori_loop(0, nc, scan, 0, unroll=2)

    gain = gain_ref[...]

    def finish(c, _):
        rows = _chunk_rows(c)
        for j in range(G):
            cols = slice(j * D, (j + 1) * D)
            y = _head_norm(o_scr[rows, cols], gain[:, cols])
            dg = dg_ref[rows, cols]
            o_ref[rows, cols] = ((dg * _sigmoid(dg)) * y).astype(BF16)
        return 0

    lax.fori_loop(0, nc, finish, 0, unroll=2)

    if emit_state:
        for d in range(2):
            for j in range(G):
                sout_ref[d, j] = s_scr[d, j]


def _retention(p, nb, t, row_off, log_decay, gain, state):
    assert row_off % t == 0 and MIX_HEADS % RET_GROUP == 0 and COL_D_Q % RET_GROUP == 0
    ob = row_off // t
    has_state = state is not None
    emit_state = not has_state
    G = RET_GROUP
    gw = G * MIX_DIM

    def col(cb):
        return pl.BlockSpec((t, gw), lambda b, h: (ob + b, cb // G + h))

    in_specs = [col(COL_D_Q), col(COL_D_K), col(COL_D_V), col(COL_D_G),
                pl.BlockSpec(memory_space=pltpu.SMEM),
                pl.BlockSpec((1, gw), lambda b, h: (0, h))]
    args = [p, p, p, p, log_decay, gain]
    if has_state:
        (cos, sa, sb), s0, layer = state
        tab = pl.BlockSpec((t, LANE), lambda b, h: (0, 0))
        in_specs += [tab, tab, tab,
                     pl.BlockSpec((None, None, 2, G, MIX_DIM, MIX_DIM), lambda b, h: (b, layer, 0, h, 0, 0))]
        args += [cos, sa, sb, s0]
    out_shape = [jax.ShapeDtypeStruct((nb * t, BRANCH_W), BF16)]
    out_specs = [pl.BlockSpec((t, gw), lambda b, h: (b, h))]
    if emit_state:
        out_shape.append(jax.ShapeDtypeStruct((nb, 2, MIX_HEADS, MIX_DIM, MIX_DIM), F32))
        out_specs.append(pl.BlockSpec((None, 2, G, MIX_DIM, MIX_DIM), lambda b, h: (b, 0, h, 0, 0)))
    buffers = (2 * 4 * t * gw * 4 + 2 * 3 * t * LANE * 4 + 2 * t * gw * 2
               + t * gw * (2 + 4) + (t // CHUNK) * G * 2 * MIX_DIM * MIX_DIM * 4 + 6 * G * MIX_DIM * MIX_DIM * 4)
    return pl.pallas_call(
        functools.partial(_ret_kernel, t, has_state, emit_state),
        grid=(nb, MIX_HEADS // G),
        in_specs=in_specs,
        out_specs=out_specs,
        out_shape=out_shape,
        scratch_shapes=[pltpu.VMEM((t, gw), BF16),
                        pltpu.VMEM((t // CHUNK, G, 2 * MIX_DIM, MIX_DIM), F32),
                        pltpu.VMEM((t, gw), F32),
                        pltpu.VMEM((2, G, MIX_DIM, MIX_DIM), F32)],
        compiler_params=_params(("parallel", "parallel"), buffers),
        name="ret_lat" if has_state else "ret_ctx",
    )(*args)


def _softmax_sink_pv(s_parts, v_parts, sink):
    m = sink
    for s in s_parts:
        m = jnp.maximum(m, jnp.max(s, axis=-1, keepdims=True))
    es = [jnp.exp(s - m) for s in s_parts]
    den = jnp.exp(sink - m)
    for e in es:
        den = den + jnp.sum(e, axis=-1, keepdims=True)
    out = None
    for e, v in zip(es, v_parts):
        o = _dot(e.astype(BF16), v)
        out = o if out is None else out + o
    return out * (1.0 / den)


def _attn_ctx_kernel(q_ref, k_ref, v_ref, sink_ref, o_ref, ko_ref, vo_ref):
    k = k_ref[...]
    v = v_ref[...]
    ko_ref[...] = k
    vo_ref[...] = v
    hd = ATT_HEAD_DIM
    scale = hd ** -0.5
    for j in range(ATT_KV_HEADS):
        kj = k[:, j * hd:(j + 1) * hd].astype(BF16)
        vj = v[:, j * hd:(j + 1) * hd].astype(BF16)
        for g in range(ATT_GROUPS):
            hx = j * ATT_GROUPS + g
            qh = q_ref[:, hx * hd:(hx + 1) * hd].astype(BF16)
            s = _dot_nt(qh, kj) * scale
            sink = jnp.full((1, 1), sink_ref[hx], F32)
            o_ref[:, hx * hd:(hx + 1) * hd] = _softmax_sink_pv([s], [vj], sink).astype(BF16)


def _attn_ctx(p, nb, t, row_off, sink):
    assert row_off % t == 0
    ob = row_off // t
    qw = ATT_HEADS * ATT_HEAD_DIM
    kvw = ATT_KV_HEADS * ATT_HEAD_DIM
    return pl.pallas_call(
        _attn_ctx_kernel,
        grid=(nb,),
        in_specs=[
            pl.BlockSpec((t, qw), lambda b: (ob + b, COL_B_Q * LANE // qw)),
            pl.BlockSpec((t, kvw), lambda b: (ob + b, COL_B_K)),
            pl.BlockSpec((t, kvw), lambda b: (ob + b, COL_B_V)),
            pl.BlockSpec(memory_space=pltpu.SMEM),
        ],
        out_specs=[pl.BlockSpec((t, qw), lambda b: (b, 0)),
                   pl.BlockSpec((None, t, kvw), lambda b: (b, 0, 0)),
                   pl.BlockSpec((None, t, kvw), lambda b: (b, 0, 0))],
        out_shape=[jax.ShapeDtypeStruct((nb * t, qw), BF16),
                   jax.ShapeDtypeStruct((nb, t, kvw), F32),
                   jax.ShapeDtypeStruct((nb, t, kvw), F32)],
        compiler_params=_params(("parallel",), 2 * t * (qw * 6 + 4 * kvw * 4)),
        name="attn_ctx",
    )(p, p, p, sink)


def _band_bias():
    Q = QBLOCK
    r = np.arange(Q)[:, None]
    c = np.arange(3 * Q)[None, :]
    band = (c >= r) & (c <= r + 2 * Q)
    out = []
    for kind in range(3):
        ok = band.copy()
        if kind == 0:
            ok &= c >= Q
        if kind == 2:
            ok &= c < 2 * Q
        out.append(np.where(ok, 0.0, NEG).astype(np.float32))
    return jnp.asarray(np.stack(out))


def _half_variants(x):
    low = lax.broadcasted_iota(jnp.int32, x.shape, 1) < ATT_HEAD_DIM
    sw = pltpu.roll(x, ATT_HEAD_DIM, 1)
    zero = jnp.zeros_like(x)
    return [jnp.where(low, a, b).astype(BF16) for a, b in ((x, zero), (zero, sw), (sw, zero), (zero, x))]


def _attn_lat_kernel(t, q_ref, k_ref, v_ref, kc_ref, vc_ref, cq_ref, saq_ref, sbq_ref,
                     ck_ref, sak_ref, sbk_ref, bias_ref, sink_ref, o_ref, kp_scr, vp_scr, kc_scr, vc_scr):
    n = pl.program_id(1)
    Q = QBLOCK
    hd = ATT_HEAD_DIM
    quarter = hd // 4
    nvar = 2 * ATT_KV_HEADS

    @pl.when(n == 0)
    def _():
        zero = jnp.zeros((Q, LANE), BF16)
        for vi in range(nvar):
            kp_scr[vi, 0:Q, :] = zero
            vp_scr[vi, 0:Q, :] = zero
            kp_scr[vi, Q + t:2 * Q + t, :] = zero
            vp_scr[vi, Q + t:2 * Q + t, :] = zero

        def fill(c, _):
            rows = _chunk_rows(c, Q)
            dst = pl.ds(pl.multiple_of(c * Q + Q, Q), Q)
            kk = _rope(k_ref[rows, :], ck_ref[rows, :], sak_ref[rows, :], sbk_ref[rows, :], quarter)
            for vi, (kv, vv) in enumerate(zip(_half_variants(kk), _half_variants(v_ref[rows, :]))):
                kp_scr[vi, dst, :] = kv
                vp_scr[vi, dst, :] = vv
            return 0

        lax.fori_loop(0, t // Q, fill, 0)
        for vi, (kv, vv) in enumerate(zip(_half_variants(kc_ref[...]), _half_variants(vc_ref[...]))):
            kc_scr[vi] = kv
            vc_scr[vi] = vv

    band = pl.ds(pl.multiple_of(n * Q, Q), 3 * Q)
    cq, saq, sbq = cq_ref[...], saq_ref[...], sbq_ref[...]
    bias = bias_ref[...]
    scale = hd ** -0.5
    heads_per_chunk = LANE // hd
    for c in range(ATT_HEADS // heads_per_chunk):
        sl = slice(c * LANE, (c + 1) * LANE)
        qc = (_rope(q_ref[:, sl], cq, saq, sbq, quarter) * scale).astype(BF16)
        acc = None
        for half in range(heads_per_chunk):
            hx = c * heads_per_chunk + half
            vi = 2 * (hx // ATT_GROUPS) + half
            s_loc = _dot_nt(qc, kp_scr[vi, band, :]) + bias
            s_ctx = _dot_nt(qc, kc_scr[vi])
            sink = jnp.full((1, 1), sink_ref[hx], F32)
            o = _softmax_sink_pv([s_loc, s_ctx], [vp_scr[vi, band, :], vc_scr[vi]], sink)
            acc = o if acc is None else acc + o
        o_ref[:, sl] = acc.astype(BF16)


def _attn_lat(p, nb, t, kc, vc, layer, tabs, sink):
    assert (ATT_HEAD_DIM ** -0.5) == 2.0 ** round(np.log2(ATT_HEAD_DIM ** -0.5))
    qw = ATT_HEADS * ATT_HEAD_DIM
    kvw = ATT_KV_HEADS * ATT_HEAD_DIM
    assert kvw == LANE
    nq = t // QBLOCK
    past = kc.shape[2]
    nvar = 2 * ATT_KV_HEADS
    cos, sa, sb = tabs
    bias = _band_bias()
    tq = pl.BlockSpec((QBLOCK, LANE), lambda b, n: (n, 0))
    tk = pl.BlockSpec((t, LANE), lambda b, n: (0, 0))
    return pl.pallas_call(
        functools.partial(_attn_lat_kernel, t),
        grid=(nb, nq),
        in_specs=[
            pl.BlockSpec((QBLOCK, qw), lambda b, n: (b * nq + n, COL_B_Q * LANE // qw)),
            pl.BlockSpec((t, kvw), lambda b, n: (b, COL_B_K)),
            pl.BlockSpec((t, kvw), lambda b, n: (b, COL_B_V)),
            pl.BlockSpec((None, None, past, kvw), lambda b, n: (b, layer, 0, 0)),
            pl.BlockSpec((None, None, past, kvw), lambda b, n: (b, layer, 0, 0)),
            tq, tq, tq, tk, tk, tk,
            pl.BlockSpec((None,) + bias.shape[1:],
                         lambda b, n: (jnp.where(n == 0, 0, jnp.where(n == nq - 1, 2, 1)), 0, 0)),
            pl.BlockSpec(memory_space=pltpu.SMEM),
        ],
        out_specs=pl.BlockSpec((QBLOCK, qw), lambda b, n: (b * nq + n, 0)),
        out_shape=jax.ShapeDtypeStruct((nb * t, qw), BF16),
        scratch_shapes=[pltpu.VMEM((nvar, t + 2 * QBLOCK, LANE), BF16), pltpu.VMEM((nvar, t + 2 * QBLOCK, LANE), BF16),
                        pltpu.VMEM((nvar, past, LANE), BF16), pltpu.VMEM((nvar, past, LANE), BF16)],
        compiler_params=_params(("parallel", "arbitrary"),
                                2 * 5 * t * LANE * 4 + 2 * nvar * (t + 2 * QBLOCK + past) * LANE * 2 + 2 * MIB),
        name="attn_lat",
    )(p, p, p, kc, vc, cos, sa, sb, cos, sa, sb, bias, sink)


def _conv_kernel(t, cb_ref, cc_ref, cx_ref, w_ref, o_ref):
    u = cc_ref[...] * cx_ref[...]
    tok = lax.broadcasted_iota(jnp.int32, u.shape, 0)
    prev = jnp.where(tok == 0, 0.0, pltpu.roll(u, 1, 0))
    nxt = jnp.where(tok == t - 1, 0.0, pltpu.roll(u, t - 1, 0))
    o_ref[...] = (cb_ref[...] * (w_ref[0:1, :] * prev + w_ref[1:2, :] * u + w_ref[2:3, :] * nxt)).astype(BF16)


def _conv(p, nb, t, row_off, w):
    assert row_off % t == 0
    ob = row_off // t
    nblk = w.shape[1] // LANE

    def col(cb):
        return pl.BlockSpec((t, LANE), lambda b, j: (ob + b, cb + j))

    return pl.pallas_call(
        functools.partial(_conv_kernel, t),
        grid=(nb, nblk),
        in_specs=[col(COL_C_B), col(COL_C_C), col(COL_C_X),
                  pl.BlockSpec((w.shape[0], LANE), lambda b, j: (0, j))],
        out_specs=pl.BlockSpec((t, LANE), lambda b, j: (b, j)),
        out_shape=jax.ShapeDtypeStruct((nb * t, w.shape[1]), BF16),
        compiler_params=_params(("parallel", "parallel"), 2 * 4 * t * LANE * 4),
        name="conv",
    )(p, p, p, w)


GATE_SRC = 4 * MIX_HEADS * MIX_DIM
N_GATES = 4 * MIX_HEADS


def _gate_lanes(x):
    g = x.reshape(x.shape[:-1] + (2, 2, MIX_HEADS))
    pad = [(0, 0)] * (x.ndim - 1) + [(0, LANE - 2 * MIX_HEADS)]
    blocks = [jnp.pad(g[..., kind, :].reshape(x.shape[:-1] + (2 * MIX_HEADS,)), pad) for kind in range(2)]
    return jnp.concatenate(blocks, axis=-1)


def _column_slabs(w, tile):
    k, n = w.shape[-2:]
    return jnp.swapaxes(w.reshape(w.shape[:-1] + (n // tile, tile)), -3, -2)


def _stage_mixer_proj(w_in):
    head = w_in[..., 0:GATE_SRC]
    tail = w_in[..., GATE_SRC + N_GATES:]
    gates = _gate_lanes(w_in[..., GATE_SRC:GATE_SRC + N_GATES])
    assert head.shape[-1] + tail.shape[-1] == COL_GATES * LANE
    assert head.shape[-1] + tail.shape[-1] + gates.shape[-1] == PROJ_P
    return jnp.concatenate([head, tail, gates], axis=-1).astype(BF16)


def kernel(x_prompt, x_sample, cache_attn_k, cache_attn_v, state_mlstm_C, state_mlstm_n, state_mlstm_m,
           state_ret_S, c, c_ctx, w_mod, b_mod, norm_g, ffn1_w_in, ffn1_w_out, ffn2_w_in, ffn2_w_out,
           w_in, mlstm_gate_b, mlstm_norm_g, attn_sink, conv_w, ret_log_decay, ret_norm_g,
           w_branch, w_gate, w_out):
    bp, tp, d = x_prompt.shape
    bs, ts, _ = x_sample.shape
    depth = w_mod.shape[0]
    past = cache_attn_k.shape[2]
    kvw = ATT_KV_HEADS * ATT_HEAD_DIM
    fits = [t for t in (1024, 512, 256) if ts % t == 0 and (bp * tp) % t == 0]
    rows = _Rows(bs, ts, bp, tp, fits[0])
    rows_ffn = _Rows(bs, ts, bp, tp, fits[min(1, len(fits) - 1)])

    nbp = -(-(1 + bs) // 8) * 8
    cpad = jnp.concatenate([c_ctx[None], c, jnp.zeros((nbp - 1 - bs, d), F32)], axis=0)
    mod = _modulation(cpad, w_mod, b_mod).reshape(depth, nbp, 3, 3, d)

    w_p = _stage_mixer_proj(w_in)
    ffn1_in, ffn1_out = ffn1_w_in.astype(BF16), ffn1_w_out.astype(BF16)
    ffn2_in, ffn2_out = ffn2_w_in.astype(BF16), ffn2_w_out.astype(BF16)
    tn = _pick_tile(d, 256)
    wg_b = jnp.swapaxes(_column_slabs(w_gate.astype(BF16), tn), 1, 2)
    wb_b = jnp.swapaxes(_column_slabs(w_branch.astype(BF16), tn), 1, 2)
    wo_b = w_out.astype(BF16)

    gate_bias = _gate_lanes(mlstm_gate_b.reshape(depth, 1, N_GATES))
    tabs_att = _rope_tables(ts, ATT_HEAD_DIM)
    tabs_ret = _rope_tables(ts, MIX_DIM)

    x = jnp.concatenate([x_sample.reshape(bs * ts, d), x_prompt.reshape(bp * tp, d)], axis=0)
    ms = bs * ts
    kc_all = cache_attn_k.reshape(bs, depth, past, kvw)
    vc_all = cache_attn_v.reshape(bs, depth, past, kvw)
    ks_, vs_, cs_, ns_, mms_, ss_ = [], [], [], [], [], []
    for l in range(depth):
        g = norm_g[l]
        x = _ffn(rows_ffn, x, mod[l, :, 0], g[0:2], ffn1_in, ffn1_out, l)

        p, h = _inproj(rows, x, mod[l, :, 1], g[2:4], w_p, l)
        a_gain = mlstm_norm_g[l][None]
        d_gain = ret_norm_g[l][None]
        lat_a = _mlstm(p, bs, ts, 0, gate_bias[l], a_gain, (state_mlstm_C, state_mlstm_n, state_mlstm_m, l))[0]
        lat_b = _attn_lat(p, bs, ts, kc_all, vc_all, l, tabs_att, attn_sink[l])
        lat_c = _conv(p, bs, ts, 0, conv_w[l])
        lat_d = _retention(p, bs, ts, 0, ret_log_decay[l], d_gain, (tabs_ret, state_ret_S, l))[0]
        ctx_a, c_fin, n_fin, m_fin = _mlstm(p, bp, tp, ms, gate_bias[l], a_gain, None)
        ctx_b, k_new, v_new = _attn_ctx(p, bp, tp, ms, attn_sink[l])
        ctx_c = _conv(p, bp, tp, ms, conv_w[l])
        ctx_d, s_fin = _retention(p, bp, tp, ms, ret_log_decay[l], d_gain, None)
        y = _merge(rows, h, (lat_a, lat_b, lat_c, lat_d), (ctx_a, ctx_b, ctx_c, ctx_d), wg_b, wb_b, l)
        x = _outproj(rows_ffn, x, mod[l, :, 1], g[2:4], y, wo_b, l)

        x = _ffn(rows_ffn, x, mod[l, :, 2], g[4:6], ffn2_in, ffn2_out, l)

        ks_.append(k_new.reshape(bp, tp, ATT_KV_HEADS, ATT_HEAD_DIM))
        vs_.append(v_new.reshape(bp, tp, ATT_KV_HEADS, ATT_HEAD_DIM))
        cs_.append(c_fin)
        ns_.append(n_fin)
        mms_.append(m_fin[:, :, 0].reshape(bp, 2, MIX_HEADS))
        ss_.append(s_fin)

    y_sample = x[:ms].reshape(bs, ts, d)
    y_prompt = x[ms:].reshape(bp, tp, d)
    return (y_prompt, y_sample, jnp.stack(ks_, axis=1), jnp.stack(vs_, axis=1), jnp.stack(cs_, axis=1),
            jnp.stack(ns_, axis=1), jnp.stack(mms_, axis=1), jnp.stack(ss_, axis=1))
```

```python
import functools

import numpy as np
import jax
import jax.numpy as jnp
from jax import lax
from jax.experimental import pallas as pl
from jax.experimental.pallas import tpu as pltpu

F32 = jnp.float32
BF16 = jnp.bfloat16

N_MOD = 9
N_BRANCH = 4
BRANCH_W = 512
MIX_HEADS = 4
MIX_DIM = 128
RET_GROUP = 2
ATT_HEADS = 8
ATT_KV_HEADS = 2
ATT_HEAD_DIM = 64
ATT_GROUPS = ATT_HEADS // ATT_KV_HEADS
QBLOCK = 128
CHUNK = 128
GRID_W = 64
ROPE_BASE = 10000.0
EPS = 1e-6
NEG = -1e30

LANE = 128
COL_A_Q, COL_A_K, COL_A_V, COL_A_O = 0, 4, 8, 12
COL_B_Q, COL_B_K, COL_B_V = 16, 20, 21
COL_C_B, COL_C_C, COL_C_X = 22, 26, 30
COL_D_Q, COL_D_K, COL_D_V, COL_D_G = 34, 38, 42, 46
COL_GATES = 50
PROJ_BLOCKS = 52
PROJ_P = PROJ_BLOCKS * LANE

MIB = 1024 * 1024
VMEM_CAP = 44 * MIB
VMEM_TEMPS = 12 * MIB
SINGLE = pl.Buffered(1)


def _params(sem, buffers):
    return pltpu.CompilerParams(dimension_semantics=sem,
                                vmem_limit_bytes=int(min(buffers + VMEM_TEMPS, VMEM_CAP)))


def _pick_tile(n, pref):
    if n <= pref:
        return n
    t = (pref // LANE) * LANE
    while t >= LANE:
        if n % t == 0:
            return t
        t -= LANE
    raise ValueError(f"no lane-aligned tile divides {n}")


def _dot(a, b):
    return jnp.dot(a, b, preferred_element_type=F32)


def _dot_nt(a, b):
    return lax.dot_general(a, b, (((1,), (1,)), ((), ())), preferred_element_type=F32)


def _dot_tn(a, b):
    return lax.dot_general(a, b, (((0,), (0,)), ((), ())), preferred_element_type=F32)


def _sigmoid(x):
    return 1.0 / (1.0 + jnp.exp(-x))


def _rms(xf, g):
    ms = jnp.mean(xf * xf, axis=-1, keepdims=True)
    return xf * lax.rsqrt(ms + EPS) * g


def _modulated(x_ref, mod_ref, g_ref):
    return _rms(x_ref[...], g_ref[0:1, :]) * (1.0 + mod_ref[1:2, :]) + mod_ref[0:1, :]


def _head_norm(x, g):
    mu = jnp.mean(x, axis=-1, keepdims=True)
    xc = x - mu
    var = jnp.mean(xc * xc, axis=-1, keepdims=True)
    return xc * lax.rsqrt(var + EPS) * g


def _chunk_rows(c, n=CHUNK):
    return pl.ds(pl.multiple_of(c * n, n), n)


def _mod_kernel(c_ref, w_ref, b_ref, o_ref):
    c = c_ref[...]
    s = (c * _sigmoid(c)).astype(BF16)
    o_ref[...] = _dot(s, w_ref[...].astype(BF16)) + b_ref[...]


def _modulation(cpad, w_mod, b_mod):
    depth, d, n = w_mod.shape
    nbp = cpad.shape[0]
    tn = _pick_tile(n, 1024)
    return pl.pallas_call(
        _mod_kernel,
        grid=(depth, n // tn),
        in_specs=[
            pl.BlockSpec((nbp, d), lambda l, j: (0, 0)),
            pl.BlockSpec((None, d, tn), lambda l, j: (l, 0, j)),
            pl.BlockSpec((None, 1, tn), lambda l, j: (l, 0, j)),
        ],
        out_specs=pl.BlockSpec((None, nbp, tn), lambda l, j: (l, 0, j)),
        out_shape=jax.ShapeDtypeStruct((depth, nbp, n), F32),
        compiler_params=_params(("parallel", "parallel"), 2 * d * tn * 4 + MIB),
        name="modulation",
    )(cpad, w_mod, b_mod.reshape(depth, 1, n))


class _Rows:
    def __init__(self, bs, ts, bp, tp, tm):
        self.bs, self.ts, self.bp, self.tp = bs, ts, bp, tp
        self.ms, self.mp = bs * ts, bp * tp
        self.m = self.ms + self.mp
        assert ts % tm == 0 and self.mp % tm == 0 and self.ms % tp == 0
        self.tm = tm
        self.lat_tiles = self.ms // tm
        self.tiles_per_seq = ts // tm

    def mod_index(self, i):
        return jnp.where(i < self.lat_tiles, 1 + i // self.tiles_per_seq, 0)

    def x_spec(self, d, mode=None):
        return pl.BlockSpec((self.tm, d), lambda i, *_: (i, 0), pipeline_mode=mode)

    def mod_spec(self, d):
        return pl.BlockSpec((None, 3, d), lambda i, *_: (self.mod_index(i), 0, 0))


def _layer_spec(block, index_map, l, mode=None):
    return pl.BlockSpec((None,) + tuple(block), lambda *ids: (l,) + tuple(index_map(*ids)), pipeline_mode=mode)


def _ffn_kernel(nf, x_ref, mod_ref, g_ref, wg_ref, wu_ref, wo_ref, o_ref, h_scr, acc_scr):
    f = pl.program_id(1)

    def hidden_slab(h):
        gate = _dot(h, wg_ref[...])
        up = _dot(h, wu_ref[...])
        a = (gate * _sigmoid(gate)) * up
        return _dot(a.astype(BF16), wo_ref[...])

    def pre():
        h = _modulated(x_ref, mod_ref, g_ref).astype(BF16)
        h_scr[...] = h
        return h

    def post(z):
        o_ref[...] = x_ref[...] + (0.5 * mod_ref[2:3, :]) * _rms(z, g_ref[1:2, :])

    if nf == 1:
        post(hidden_slab(pre()))
        return

    @pl.when(f == 0)
    def _():
        acc_scr[...] = hidden_slab(pre())

    @pl.when((f > 0) & (f < nf - 1))
    def _():
        acc_scr[...] += hidden_slab(h_scr[...])

    @pl.when(f == nf - 1)
    def _():
        post(acc_scr[...] + hidden_slab(h_scr[...]))


def _ffn(rows, x, mod_s, g2, w_in, w_out, l):
    m, d = x.shape
    dff = w_out.shape[1]
    tm = rows.tm
    tf = _pick_tile(dff, 512)
    nf = dff // tf
    buffers = tm * d * (2 * 4 + 2 * 4 + 4 + 2) + 3 * 2 * d * tf * 2
    return pl.pallas_call(
        functools.partial(_ffn_kernel, nf),
        grid=(m // tm, nf),
        in_specs=[
            rows.x_spec(d),
            rows.mod_spec(d),
            pl.BlockSpec((2, d), lambda i, f: (0, 0)),
            _layer_spec((d, tf), lambda i, f: (0, f), l),
            _layer_spec((d, tf), lambda i, f: (0, f + nf), l),
            _layer_spec((tf, d), lambda i, f: (f, 0), l),
        ],
        out_specs=pl.BlockSpec((tm, d), lambda i, f: (i, 0)),
        out_shape=jax.ShapeDtypeStruct((m, d), F32),
        scratch_shapes=[pltpu.VMEM((tm, d), BF16), pltpu.VMEM((tm, d), F32)],
        compiler_params=_params(("parallel", "arbitrary"), buffers),
        name="ffn",
    )(x, mod_s, g2, w_in, w_in, w_out)


def _inproj_kernel(x_ref, mod_ref, g_ref, w_ref, p_ref, h_ref):
    @pl.when(pl.program_id(1) == 0)
    def _():
        h = _modulated(x_ref, mod_ref, g_ref).astype(BF16)
        h_ref[...] = h
        p_ref[...] = _dot(h, w_ref[...])

    @pl.when(pl.program_id(1) > 0)
    def _():
        p_ref[...] = _dot(h_ref[...], w_ref[...])


def _inproj(rows, x, mod_s, g2, w_p, l):
    m, d = x.shape
    n = w_p.shape[2]
    tn = _pick_tile(n, 512)
    nt = n // tn
    tm = rows.tm
    return pl.pallas_call(
        _inproj_kernel,
        grid=(m // tm, nt),
        in_specs=[
            rows.x_spec(d),
            rows.mod_spec(d),
            pl.BlockSpec((2, d), lambda i, j: (0, 0)),
            _layer_spec((d, tn), lambda i, j: (0, j), l),
        ],
        out_specs=[pl.BlockSpec((tm, tn), lambda i, j: (i, j)),
                   pl.BlockSpec((tm, d), lambda i, j: (i, 0))],
        out_shape=[jax.ShapeDtypeStruct((m, n), F32), jax.ShapeDtypeStruct((m, d), BF16)],
        compiler_params=_params(("parallel", "arbitrary"),
                                tm * d * (2 * 4 + 2 * 2) + 2 * d * tn * 2 + 2 * tm * tn * 4),
        name="inproj",
    )(x, mod_s, g2, w_p)


def _merge_kernel(lat_tiles, h_ref, *refs):
    lat_refs, ctx_refs = refs[0:N_BRANCH], refs[N_BRANCH:2 * N_BRANCH]
    wg_ref, wb_ref, y_ref = refs[2 * N_BRANCH:]
    h = h_ref[...]
    is_lat = pl.program_id(0) < lat_tiles
    y = None
    for i in range(N_BRANCH):
        o = jnp.where(is_lat, lat_refs[i][...], ctx_refs[i][...])
        term = _sigmoid(_dot(h, wg_ref[i])) * _dot(o, wb_ref[i])
        y = term if y is None else y + term
    y_ref[...] = y.astype(BF16)


def _merge(rows, h, lat, ctx, w_gate, w_branch, l):
    m, d = h.shape
    tm = rows.tm
    nt, _, bw, tn = w_branch.shape[1:]
    assert w_gate.shape[1:] == (nt, N_BRANCH, d, tn)
    last_lat = rows.lat_tiles - 1
    lat_spec = pl.BlockSpec((tm, bw), lambda i, j: (jnp.minimum(i, last_lat), 0))
    ctx_spec = pl.BlockSpec((tm, bw), lambda i, j: (jnp.maximum(i - rows.lat_tiles, 0), 0))
    return pl.pallas_call(
        functools.partial(_merge_kernel, rows.lat_tiles),
        grid=(m // tm, nt),
        in_specs=[
            rows.x_spec(d),
            *([lat_spec] * N_BRANCH), *([ctx_spec] * N_BRANCH),
            _layer_spec((None, N_BRANCH, d, tn), lambda i, j: (j, 0, 0, 0), l),
            _layer_spec((None, N_BRANCH, bw, tn), lambda i, j: (j, 0, 0, 0), l),
        ],
        out_specs=pl.BlockSpec((tm, tn), lambda i, j: (i, j)),
        out_shape=jax.ShapeDtypeStruct((m, d), BF16),
        compiler_params=_params(("parallel", "arbitrary"),
                                2 * tm * d * 2 + 2 * 2 * N_BRANCH * tm * bw * 2
                                + 2 * N_BRANCH * (d + bw) * tn * 2 + 2 * tm * tn * 2),
        name="merge",
    )(h, *lat, *ctx, w_gate, w_branch)


def _outproj_kernel(x_ref, mod_ref, g_ref, y_ref, w_ref, o_ref):
    z = _dot(y_ref[...], w_ref[...])
    o_ref[...] = x_ref[...] + mod_ref[2:3, :] * _rms(z, g_ref[1:2, :])


def _outproj(rows, x, mod_s, g2, y, w_out, l):
    m, d = x.shape
    tm = rows.tm
    return pl.pallas_call(
        _outproj_kernel,
        grid=(m // tm,),
        in_specs=[
            rows.x_spec(d),
            rows.mod_spec(d),
            pl.BlockSpec((2, d), lambda i: (0, 0)),
            rows.x_spec(d),
            _layer_spec((d, d), lambda i: (0, 0), l, SINGLE),
        ],
        out_specs=pl.BlockSpec((tm, d), lambda i: (i, 0)),
        out_shape=jax.ShapeDtypeStruct((m, d), F32),
        compiler_params=_params(("parallel",), tm * d * (2 * 4 + 2 * 4 + 4 + 2 * 2) + d * d * 2),
        name="outproj",
    )(x, mod_s, g2, y, w_out)


def _rope_tables(t, d):
    rows = t // GRID_W
    row = jnp.repeat(jnp.arange(rows), GRID_W).astype(F32)
    col = jnp.tile(jnp.arange(GRID_W), rows).astype(F32)
    quarter = d // 4
    inv = jnp.power(ROPE_BASE, -jnp.arange(quarter, dtype=F32) / quarter)
    ar = row[:, None] * inv
    ac = col[:, None] * inv
    ang = jnp.concatenate([ar, ar, ac, ac], axis=-1)
    cos, sin = jnp.cos(ang), jnp.sin(ang)
    first = (jnp.arange(d) % (2 * quarter)) < quarter
    sa = jnp.where(first, -sin, 0.0)
    sb = jnp.where(first, 0.0, sin)
    reps = LANE // d
    return tuple(jnp.tile(a, (1, reps)) for a in (cos, sa, sb))


def _rope(x, cos, sa, sb, quarter):
    n = x.shape[-1]
    ahead = pltpu.roll(x, n - quarter, 1)
    behind = pltpu.roll(x, quarter, 1)
    return x * cos + ahead * sa + behind * sb


def _log_sigmoid(x):
    return jnp.minimum(x, 0.0) - jnp.log1p(jnp.exp(-jnp.abs(x)))


def _split_bf16(x):
    hi = x.astype(BF16)
    return hi, (x - hi.astype(F32)).astype(BF16)


def _mlstm_kernel(t, layer, emit_state, *refs):
    has_state = layer is not None
    it = iter(refs)
    q_ref, k_ref, v_ref, og_ref, gt_ref, bias_ref, gain_ref = (next(it) for _ in range(7))
    if has_state:
        c0_ref, n0_ref, m0_ref = next(it), next(it), next(it)
    o_ref = next(it)
    if emit_state:
        cout_ref, nout_ref, mout_ref = next(it), next(it), next(it)
    h_scr, s_scr = next(it), next(it)

    L, D, H = CHUNK, MIX_DIM, MIX_HEADS
    nc = t // L
    r_i = lax.broadcasted_iota(jnp.int32, (L, L), 0)
    c_i = lax.broadcasted_iota(jnp.int32, (L, L), 1)
    masks = (r_i >= c_i, r_i <= c_i)
    cum = tuple(mk.astype(BF16) for mk in masks)
    ones = jnp.ones((L, D), BF16)
    sel = [(r_i == g).astype(BF16) for g in range(2 * H)]
    bias = bias_ref[...]
    scale = D ** -0.5

    ms0 = []
    for d in range(2):
        for h in range(H):
            if has_state:
                s_scr[d, h, :, 0:D] = c0_ref[d, h]
                s_scr[d, h, :, D:2 * D] = jnp.broadcast_to(n0_ref[d, h:h + 1, :], (D, D)).T
                ms0.append(jnp.full((1, D), m0_ref[pl.program_id(0), layer, d, h], F32))
            else:
                s_scr[d, h] = jnp.zeros((D, 2 * D), F32)
                ms0.append(jnp.zeros((1, D), F32))

    def zero_rows(c, _):
        h_scr[_chunk_rows(c), :] = jnp.zeros((L, H * D), F32)
        return 0

    lax.fori_loop(0, nc, zero_rows, 0)

    def body(c, ms):
        chains = []
        for d, cc in ((0, c), (1, nc - 1 - c)):
            rows = _chunk_rows(cc)
            g = gt_ref[rows, :] + bias
            lf_hi, lf_lo = _split_bf16(_log_sigmoid(g[:, LANE:2 * LANE]))
            f_all = _dot(cum[d], lf_hi) + _dot(cum[d], lf_lo)
            b_all = g[:, 0:LANE] - f_all
            b_t = b_all.T
            f_hi, f_lo = _split_bf16(f_all)
            b_hi, b_lo = _split_bf16(b_all)
            for h in range(H):
                j = d * H + h
                cols = slice(h * D, (h + 1) * D)
                k = k_ref[rows, cols] * scale
                qb = q_ref[rows, cols].astype(BF16)
                s_old = s_scr[d, h]
                chains.append(dict(
                    d=d, h=h, j=j, rows=rows, cols=cols, k=k, s_old=s_old, brow=b_t[j:j + 1, :],
                    v_aug=jnp.concatenate([v_ref[rows, cols].astype(BF16), ones], axis=1),
                    f_rep=_dot(f_hi, sel[j]) + _dot(f_lo, sel[j]),
                    b_rep=_dot(b_hi, sel[j]) + _dot(b_lo, sel[j]),
                    qk=_dot_nt(qb, k.astype(BF16)),
                    qs=_dot(qb, s_old.astype(BF16))))
        for ch in chains:
            d, f_rep, b_rep, m = ch["d"], ch["f_rep"], ch["b_rep"], ms[ch["j"]]
            ch["f_last"] = f_rep[L - 1:L, :] if d == 0 else f_rep[0:1, :]
            lw = jnp.where(masks[d], f_rep + ch["brow"], NEG)
            inter = f_rep + m
            m_tok = jnp.maximum(jnp.max(lw, axis=-1, keepdims=True), inter)
            ch["a"] = (ch["qk"] * jnp.exp(lw - m_tok)).astype(BF16)
            ch["w_inter"] = jnp.exp(inter - m_tok)
            ch["floor"] = jnp.exp(-m_tok)
            ch["bmax"] = jnp.max(b_rep, axis=0, keepdims=True)
            ch["kw"] = (ch["k"] * jnp.exp(b_rep - ch["bmax"])).astype(BF16)
        for ch in chains:
            ch["pv"] = _dot(ch["a"], ch["v_aug"])
            ch["u_aug"] = _dot_tn(ch["kw"], ch["v_aug"])
        out = list(ms)
        for ch in chains:
            d, h, pv, qs, w_inter, m = ch["d"], ch["h"], ch["pv"], ch["qs"], ch["w_inter"], ms[ch["j"]]
            num = pv[:, 0:D] + w_inter * qs[:, 0:D]
            den = pv[:, D:2 * D] + w_inter * qs[:, D:2 * D]
            h_scr[ch["rows"], ch["cols"]] += num * (1.0 / jnp.maximum(jnp.abs(den), ch["floor"]))
            a_end = ch["f_last"] + ch["bmax"]
            m_new = jnp.maximum(ch["f_last"] + m, a_end)
            d_old = jnp.exp(ch["f_last"] + m - m_new)
            d_new = jnp.exp(a_end - m_new)
            for half in (slice(0, D), slice(D, 2 * D)):
                s_scr[d, h, :, half] = d_old * ch["s_old"][:, half] + d_new * ch["u_aug"][:, half]
            out[ch["j"]] = m_new
        return tuple(out)

    ms = lax.fori_loop(0, nc, body, tuple(ms0))

    gain = gain_ref[...]

    def finish(c, _):
        rows = _chunk_rows(c)
        for h in range(H):
            cols = slice(h * D, (h + 1) * D)
            y = _head_norm(h_scr[rows, cols], gain[:, cols])
            o_ref[rows, cols] = (_sigmoid(og_ref[rows, cols]) * y).astype(BF16)
        return 0

    lax.fori_loop(0, nc, finish, 0)

    if emit_state:
        for d in range(2):
            for h in range(H):
                cout_ref[d, h] = s_scr[d, h, :, 0:D]
                nout_ref[d, h:h + 1, :] = s_scr[d, h, :, D:2 * D].T[0:1, :]
                mout_ref[d * H + h:d * H + h + 1, :] = ms[d * H + h]


def _mlstm(p, nb, t, row_off, bias, gain, state):
    assert row_off % t == 0
    ob = row_off // t
    has_state = state is not None
    emit_state = not has_state
    hw = MIX_HEADS * MIX_DIM
    st_c = pl.BlockSpec((None, 2, MIX_HEADS, MIX_DIM, MIX_DIM), lambda b: (b, 0, 0, 0, 0))
    st_n = pl.BlockSpec((None, 2, MIX_HEADS, MIX_DIM), lambda b: (b, 0, 0, 0))

    def col(cb):
        return pl.BlockSpec((t, hw), lambda b: (ob + b, cb * LANE // hw), pipeline_mode=mode)

    mode = SINGLE if t * hw * 4 >= 2 * MIB else None
    nbuf = 1 if mode is SINGLE else 2
    buffers = (nbuf * (4 * t * hw + t * 2 * LANE) * 4 + 2 * t * hw * 2 + t * hw * 4
               + 6 * 2 * MIX_HEADS * MIX_DIM * MIX_DIM * 4)
    in_specs = [col(COL_A_Q), col(COL_A_K), col(COL_A_V), col(COL_A_O),
                pl.BlockSpec((t, 2 * LANE), lambda b: (ob + b, COL_GATES // 2), pipeline_mode=mode),
                pl.BlockSpec((1, 2 * LANE), lambda b: (0, 0)),
                pl.BlockSpec((1, hw), lambda b: (0, 0))]
    args = [p, p, p, p, p, bias, gain]
    layer = None
    if has_state:
        layer = state[3]
        in_specs += [pl.BlockSpec((None, None, 2, MIX_HEADS, MIX_DIM, MIX_DIM), lambda b: (b, layer, 0, 0, 0, 0)),
                     pl.BlockSpec((None, None, 2, MIX_HEADS, MIX_DIM), lambda b: (b, layer, 0, 0, 0)),
                     pl.BlockSpec(memory_space=pltpu.SMEM)]
        args += list(state[0:3])
    out_shape = [jax.ShapeDtypeStruct((nb * t, hw), BF16)]
    out_specs = [pl.BlockSpec((t, hw), lambda b: (b, 0))]
    if emit_state:
        out_shape += [jax.ShapeDtypeStruct((nb, 2, MIX_HEADS, MIX_DIM, MIX_DIM), F32),
                      jax.ShapeDtypeStruct((nb, 2, MIX_HEADS, MIX_DIM), F32),
                      jax.ShapeDtypeStruct((nb, 2 * MIX_HEADS, MIX_DIM), F32)]
        out_specs += [st_c, st_n, pl.BlockSpec((None, 2 * MIX_HEADS, MIX_DIM), lambda b: (b, 0, 0))]
    return pl.pallas_call(
        functools.partial(_mlstm_kernel, t, layer, emit_state),
        grid=(nb,),
        in_specs=in_specs,
        out_specs=out_specs,
        out_shape=out_shape,
        scratch_shapes=[pltpu.VMEM((t, hw), F32),
                        pltpu.VMEM((2, MIX_HEADS, MIX_DIM, 2 * MIX_DIM), F32)],
        compiler_params=_params(("parallel",), buffers),
        name="mlstm_lat" if has_state else "mlstm_ctx",
    )(*args)


def _ret_kernel(t, has_state, emit_state, *refs):
    it = iter(refs)
    q_ref, k_ref, v_ref, dg_ref, lg_ref, gain_ref = (next(it) for _ in range(6))
    if has_state:
        cos_ref, sa_ref, sb_ref, s0_ref = (next(it) for _ in range(4))
    o_ref = next(it)
    if emit_state:
        sout_ref = next(it)
    q_scr, u_scr, o_scr, s_scr = (next(it) for _ in range(4))

    L, D, G = CHUNK, MIX_DIM, RET_GROUP
    nc = t // L
    diff = (lax.broadcasted_iota(jnp.int32, (L, L), 0) - lax.broadcasted_iota(jnp.int32, (L, L), 1)).astype(F32)
    tcol = lax.broadcasted_iota(jnp.int32, (L, 1), 0).astype(F32)
    scale = D ** -0.5
    dm, w_end, w_cross, decay = [], [], [], []
    for j in range(G):
        hx = pl.program_id(1) * G + j
        lgs = (lg_ref[0, hx], lg_ref[1, hx])
        dm.append(jnp.where(diff >= 0.0, jnp.exp(jnp.maximum(diff, 0.0) * lgs[0]), 0.0)
                  + jnp.where(diff <= 0.0, jnp.exp(jnp.maximum(-diff, 0.0) * lgs[1]), 0.0))
        w_end.append((jnp.exp((L - 1.0 - tcol) * lgs[0]), jnp.exp(tcol * lgs[1])))
        w_cross.append((jnp.exp((tcol + 1.0) * lgs[0]), jnp.exp((L - tcol) * lgs[1])))
        decay.append(tuple(jnp.exp(jnp.full((1, 1), float(L), F32) * lgs[d]) for d in range(2)))
        for d in range(2):
            s_scr[d, j] = s0_ref[d, j] if has_state else jnp.zeros((D, D), F32)

    def local(cp, _):
        chains = []
        for c in (2 * cp, 2 * cp + 1):
            rows = _chunk_rows(c)
            if has_state:
                cos, sa, sb = cos_ref[rows, :], sa_ref[rows, :], sb_ref[rows, :]
            for j in range(G):
                cols = slice(j * D, (j + 1) * D)
                q = q_ref[rows, cols]
                k = k_ref[rows, cols] * scale
                if has_state:
                    q = _rope(q, cos, sa, sb, D // 4)
                    k = _rope(k, cos, sa, sb, D // 4)
                qb = q.astype(BF16)
                vb = v_ref[rows, cols].astype(BF16)
                q_scr[rows, cols] = qb
                kw = jnp.concatenate([(k * w_end[j][0]).astype(BF16), (k * w_end[j][1]).astype(BF16)], axis=1)
                u_scr[c, j] = _dot_tn(kw, vb)
                chains.append((rows, cols, j, vb, _dot_nt(qb, k.astype(BF16))))
        weighted = [(qk * dm[j]).astype(BF16) for _, _, j, _, qk in chains]
        for (rows, cols, _, vb, _), a in zip(chains, weighted):
            o_scr[rows, cols] = _dot(a, vb)
        return 0

    assert nc % 2 == 0
    lax.fori_loop(0, nc // 2, local, 0)

    def scan(c, _):
        chains = []
        for d, cc in ((0, c), (1, nc - 1 - c)):
            rows = _chunk_rows(cc)
            for j in range(G):
                cols = slice(j * D, (j + 1) * D)
                s_old = s_scr[d, j]
                chains.append((d, cc, j, rows, cols, s_old, _dot(q_scr[rows, cols], s_old.astype(BF16))))
        for d, cc, j, rows, cols, s_old, cross in chains:
            o_scr[rows, cols] += cross * w_cross[j][d]
            s_scr[d, j] = decay[j][d] * s_old + u_scr[cc, j, d * D:(d + 1) * D, :]
        return 0

    lax.fori_loop(0, nc, scan, 0, unroll=2)

    gain = gain_ref[...]

    def finish(c, _):
        rows = _chunk_rows(c)
        for j in range(G):
            cols = slice(j * D, (j + 1) * D)
            y = _head_norm(o_scr[rows, cols], gain[:, cols])
            dg = dg_ref[rows, cols]
            o_ref[rows, cols] = ((dg * _sigmoid(dg)) * y).astype(BF16)
        return 0

    lax.fori_loop(0, nc, finish, 0, unroll=2)

    if emit_state:
        for d in range(2):
            for j in range(G):
                sout_ref[d, j] = s_scr[d, j]


def _retention(p, nb, t, row_off, log_decay, gain, state):
    assert row_off % t == 0 and MIX_HEADS % RET_GROUP == 0 and COL_D_Q % RET_GROUP == 0
    ob = row_off // t
    has_state = state is not None
    emit_state = not has_state
    G = RET_GROUP
    gw = G * MIX_DIM

    def col(cb):
        return pl.BlockSpec((t, gw), lambda b, h: (ob + b, cb // G + h))

    in_specs = [col(COL_D_Q), col(COL_D_K), col(COL_D_V), col(COL_D_G),
                pl.BlockSpec(memory_space=pltpu.SMEM),
                pl.BlockSpec((1, gw), lambda b, h: (0, h))]
    args = [p, p, p, p, log_decay, gain]
    if has_state:
        (cos, sa, sb), s0, layer = state
        tab = pl.BlockSpec((t, LANE), lambda b, h: (0, 0))
        in_specs += [tab, tab, tab,
                     pl.BlockSpec((None, None, 2, G, MIX_DIM, MIX_DIM), lambda b, h: (b, layer, 0, h, 0, 0))]
        args += [cos, sa, sb, s0]
    out_shape = [jax.ShapeDtypeStruct((nb * t, BRANCH_W), BF16)]
    out_specs = [pl.BlockSpec((t, gw), lambda b, h: (b, h))]
    if emit_state:
        out_shape.append(jax.ShapeDtypeStruct((nb, 2, MIX_HEADS, MIX_DIM, MIX_DIM), F32))
        out_specs.append(pl.BlockSpec((None, 2, G, MIX_DIM, MIX_DIM), lambda b, h: (b, 0, h, 0, 0)))
    buffers = (2 * 4 * t * gw * 4 + 2 * 3 * t * LANE * 4 + 2 * t * gw * 2
               + t * gw * (2 + 4) + (t // CHUNK) * G * 2 * MIX_DIM * MIX_DIM * 4 + 6 * G * MIX_DIM * MIX_DIM * 4)
    return pl.pallas_call(
        functools.partial(_ret_kernel, t, has_state, emit_state),
        grid=(nb, MIX_HEADS // G),
        in_specs=in_specs,
        out_specs=out_specs,
        out_shape=out_shape,
        scratch_shapes=[pltpu.VMEM((t, gw), BF16),
                        pltpu.VMEM((t // CHUNK, G, 2 * MIX_DIM, MIX_DIM), F32),
                        pltpu.VMEM((t, gw), F32),
                        pltpu.VMEM((2, G, MIX_DIM, MIX_DIM), F32)],
        compiler_params=_params(("parallel", "parallel"), buffers),
        name="ret_lat" if has_state else "ret_ctx",
    )(*args)


def _softmax_sink_pv(s_parts, v_parts, sink):
    m = sink
    for s in s_parts:
        m = jnp.maximum(m, jnp.max(s, axis=-1, keepdims=True))
    es = [jnp.exp(s - m) for s in s_parts]
    den = jnp.exp(sink - m)
    for e in es:
        den = den + jnp.sum(e, axis=-1, keepdims=True)
    out = None
    for e, v in zip(es, v_parts):
        o = _dot(e.astype(BF16), v)
        out = o if out is None else out + o
    return out * (1.0 / den)


def _attn_ctx_kernel(q_ref, k_ref, v_ref, sink_ref, o_ref, ko_ref, vo_ref):
    k = k_ref[...]
    v = v_ref[...]
    ko_ref[...] = k
    vo_ref[...] = v
    hd = ATT_HEAD_DIM
    scale = hd ** -0.5
    kjs = [k[:, j * hd:(j + 1) * hd].astype(BF16) for j in range(ATT_KV_HEADS)]
    vjs = [v[:, j * hd:(j + 1) * hd].astype(BF16) for j in range(ATT_KV_HEADS)]

    def scores(hx):
        return [_dot_nt(q_ref[:, hx * hd:(hx + 1) * hd].astype(BF16), kjs[hx // ATT_GROUPS]) * scale]

    s_next = scores(0)
    for hx in range(ATT_HEADS):
        s_parts = s_next
        if hx + 1 < ATT_HEADS:
            s_next = scores(hx + 1)
        sink = jnp.full((1, 1), sink_ref[hx], F32)
        o_ref[:, hx * hd:(hx + 1) * hd] = _softmax_sink_pv(s_parts, [vjs[hx // ATT_GROUPS]], sink).astype(BF16)


def _attn_ctx(p, nb, t, row_off, sink):
    assert row_off % t == 0
    ob = row_off // t
    qw = ATT_HEADS * ATT_HEAD_DIM
    kvw = ATT_KV_HEADS * ATT_HEAD_DIM
    return pl.pallas_call(
        _attn_ctx_kernel,
        grid=(nb,),
        in_specs=[
            pl.BlockSpec((t, qw), lambda b: (ob + b, COL_B_Q * LANE // qw)),
            pl.BlockSpec((t, kvw), lambda b: (ob + b, COL_B_K)),
            pl.BlockSpec((t, kvw), lambda b: (ob + b, COL_B_V)),
            pl.BlockSpec(memory_space=pltpu.SMEM),
        ],
        out_specs=[pl.BlockSpec((t, qw), lambda b: (b, 0)),
                   pl.BlockSpec((None, t, kvw), lambda b: (b, 0, 0)),
                   pl.BlockSpec((None, t, kvw), lambda b: (b, 0, 0))],
        out_shape=[jax.ShapeDtypeStruct((nb * t, qw), BF16),
                   jax.ShapeDtypeStruct((nb, t, kvw), F32),
                   jax.ShapeDtypeStruct((nb, t, kvw), F32)],
        compiler_params=_params(("parallel",), 2 * t * (qw * 6 + 4 * kvw * 4)),
        name="attn_ctx",
    )(p, p, p, sink)


def _band_bias():
    Q = QBLOCK
    r = np.arange(Q)[:, None]
    c = np.arange(3 * Q)[None, :]
    band = (c >= r) & (c <= r + 2 * Q)
    out = []
    for kind in range(3):
        ok = band.copy()
        if kind == 0:
            ok &= c >= Q
        if kind == 2:
            ok &= c < 2 * Q
        out.append(np.where(ok, 0.0, NEG).astype(np.float32))
    return jnp.asarray(np.stack(out))


def _half_variants(x):
    low = lax.broadcasted_iota(jnp.int32, x.shape, 1) < ATT_HEAD_DIM
    sw = pltpu.roll(x, ATT_HEAD_DIM, 1)
    zero = jnp.zeros_like(x)
    return [jnp.where(low, a, b).astype(BF16) for a, b in ((x, zero), (zero, sw), (sw, zero), (zero, x))]


def _attn_lat_kernel(t, q_ref, k_ref, v_ref, kc_ref, vc_ref, cq_ref, saq_ref, sbq_ref,
                     ck_ref, sak_ref, sbk_ref, bias_ref, sink_ref, o_ref, kp_scr, vp_scr, kc_scr, vc_scr):
    n = pl.program_id(1)
    Q = QBLOCK
    hd = ATT_HEAD_DIM
    quarter = hd // 4
    nvar = 2 * ATT_KV_HEADS

    @pl.when(n == 0)
    def _():
        zero = jnp.zeros((Q, LANE), BF16)
        for vi in range(nvar):
            kp_scr[vi, 0:Q, :] = zero
            vp_scr[vi, 0:Q, :] = zero
            kp_scr[vi, Q + t:2 * Q + t, :] = zero
            vp_scr[vi, Q + t:2 * Q + t, :] = zero

        def fill(c, _):
            rows = _chunk_rows(c, Q)
            dst = pl.ds(pl.multiple_of(c * Q + Q, Q), Q)
            kk = _rope(k_ref[rows, :], ck_ref[rows, :], sak_ref[rows, :], sbk_ref[rows, :], quarter)
            for vi, (kv, vv) in enumerate(zip(_half_variants(kk), _half_variants(v_ref[rows, :]))):
                kp_scr[vi, dst, :] = kv
                vp_scr[vi, dst, :] = vv
            return 0

        lax.fori_loop(0, t // Q, fill, 0)
        for vi, (kv, vv) in enumerate(zip(_half_variants(kc_ref[...]), _half_variants(vc_ref[...]))):
            kc_scr[vi] = kv
            vc_scr[vi] = vv

    band = pl.ds(pl.multiple_of(n * Q, Q), 3 * Q)
    cq, saq, sbq = cq_ref[...], saq_ref[...], sbq_ref[...]
    bias = bias_ref[...]
    scale = hd ** -0.5
    heads_per_chunk = LANE // hd
    qcs = [(_rope(q_ref[:, c * LANE:(c + 1) * LANE], cq, saq, sbq, quarter) * scale).astype(BF16)
           for c in range(ATT_HEADS // heads_per_chunk)]

    def variant(hx):
        return 2 * (hx // ATT_GROUPS) + hx % heads_per_chunk

    def scores(hx):
        qc, vi = qcs[hx // heads_per_chunk], variant(hx)
        return [_dot_nt(qc, kp_scr[vi, band, :]) + bias, _dot_nt(qc, kc_scr[vi])]

    acc = None
    s_next = scores(0)
    for hx in range(ATT_HEADS):
        s_parts = s_next
        if hx + 1 < ATT_HEADS:
            s_next = scores(hx + 1)
        vi = variant(hx)
        sink = jnp.full((1, 1), sink_ref[hx], F32)
        o = _softmax_sink_pv(s_parts, [vp_scr[vi, band, :], vc_scr[vi]], sink)
        acc = o if acc is None else acc + o
        if hx % heads_per_chunk == heads_per_chunk - 1:
            c = hx // heads_per_chunk
            o_ref[:, c * LANE:(c + 1) * LANE] = acc.astype(BF16)
            acc = None


def _attn_lat(p, nb, t, kc, vc, layer, tabs, sink):
    assert (ATT_HEAD_DIM ** -0.5) == 2.0 ** round(np.log2(ATT_HEAD_DIM ** -0.5))
    qw = ATT_HEADS * ATT_HEAD_DIM
    kvw = ATT_KV_HEADS * ATT_HEAD_DIM
    assert kvw == LANE
    nq = t // QBLOCK
    past = kc.shape[2]
    nvar = 2 * ATT_KV_HEADS
    cos, sa, sb = tabs
    bias = _band_bias()
    tq = pl.BlockSpec((QBLOCK, LANE), lambda b, n: (n, 0))
    tk = pl.BlockSpec((t, LANE), lambda b, n: (0, 0))
    return pl.pallas_call(
        functools.partial(_attn_lat_kernel, t),
        grid=(nb, nq),
        in_specs=[
            pl.BlockSpec((QBLOCK, qw), lambda b, n: (b * nq + n, COL_B_Q * LANE // qw)),
            pl.BlockSpec((t, kvw), lambda b, n: (b, COL_B_K)),
            pl.BlockSpec((t, kvw), lambda b, n: (b, COL_B_V)),
            pl.BlockSpec((None, None, past, kvw), lambda b, n: (b, layer, 0, 0)),
            pl.BlockSpec((None, None, past, kvw), lambda b, n: (b, layer, 0, 0)),
            tq, tq, tq, tk, tk, tk,
            pl.BlockSpec((None,) + bias.shape[1:],
                         lambda b, n: (jnp.where(n == 0, 0, jnp.where(n == nq - 1, 2, 1)), 0, 0)),
            pl.BlockSpec(memory_space=pltpu.SMEM),
        ],
        out_specs=pl.BlockSpec((QBLOCK, qw), lambda b, n: (b * nq + n, 0)),
        out_shape=jax.ShapeDtypeStruct((nb * t, qw), BF16),
        scratch_shapes=[pltpu.VMEM((nvar, t + 2 * QBLOCK, LANE), BF16), pltpu.VMEM((nvar, t + 2 * QBLOCK, LANE), BF16),
                        pltpu.VMEM((nvar, past, LANE), BF16), pltpu.VMEM((nvar, past, LANE), BF16)],
        compiler_params=_params(("parallel", "arbitrary"),
                                2 * 5 * t * LANE * 4 + 2 * nvar * (t + 2 * QBLOCK + past) * LANE * 2 + 2 * MIB),
        name="attn_lat",
    )(p, p, p, kc, vc, cos, sa, sb, cos, sa, sb, bias, sink)


def _conv_kernel(t, cb_ref, cc_ref, cx_ref, w_ref, o_ref):
    u = cc_ref[...] * cx_ref[...]
    tok = lax.broadcasted_iota(jnp.int32, u.shape, 0)
    prev = jnp.where(tok == 0, 0.0, pltpu.roll(u, 1, 0))
    nxt = jnp.where(tok == t - 1, 0.0, pltpu.roll(u, t - 1, 0))
    o_ref[...] = (cb_ref[...] * (w_ref[0:1, :] * prev + w_ref[1:2, :] * u + w_ref[2:3, :] * nxt)).astype(BF16)


def _conv(p, nb, t, row_off, w):
    assert row_off % t == 0
    ob = row_off // t
    nblk = w.shape[1] // LANE

    def col(cb):
        return pl.BlockSpec((t, LANE), lambda b, j: (ob + b, cb + j))

    return pl.pallas_call(
        functools.partial(_conv_kernel, t),
        grid=(nb, nblk),
        in_specs=[col(COL_C_B), col(COL_C_C), col(COL_C_X),
                  pl.BlockSpec((w.shape[0], LANE), lambda b, j: (0, j))],
        out_specs=pl.BlockSpec((t, LANE), lambda b, j: (b, j)),
        out_shape=jax.ShapeDtypeStruct((nb * t, w.shape[1]), BF16),
        compiler_params=_params(("parallel", "parallel"), 2 * 4 * t * LANE * 4),
        name="conv",
    )(p, p, p, w)


GATE_SRC = 4 * MIX_HEADS * MIX_DIM
N_GATES = 4 * MIX_HEADS


def _gate_lanes(x):
    g = x.reshape(x.shape[:-1] + (2, 2, MIX_HEADS))
    pad = [(0, 0)] * (x.ndim - 1) + [(0, LANE - 2 * MIX_HEADS)]
    blocks = [jnp.pad(g[..., kind, :].reshape(x.shape[:-1] + (2 * MIX_HEADS,)), pad) for kind in range(2)]
    return jnp.concatenate(blocks, axis=-1)


def _column_slabs(w, tile):
    k, n = w.shape[-2:]
    return jnp.swapaxes(w.reshape(w.shape[:-1] + (n // tile, tile)), -3, -2)


def _stage_mixer_proj(w_in):
    head = w_in[..., 0:GATE_SRC]
    tail = w_in[..., GATE_SRC + N_GATES:]
    gates = _gate_lanes(w_in[..., GATE_SRC:GATE_SRC + N_GATES])
    assert head.shape[-1] + tail.shape[-1] == COL_GATES * LANE
    assert head.shape[-1] + tail.shape[-1] + gates.shape[-1] == PROJ_P
    return jnp.concatenate([head, tail, gates], axis=-1).astype(BF16)


def kernel(x_prompt, x_sample, cache_attn_k, cache_attn_v, state_mlstm_C, state_mlstm_n, state_mlstm_m,
           state_ret_S, c, c_ctx, w_mod, b_mod, norm_g, ffn1_w_in, ffn1_w_out, ffn2_w_in, ffn2_w_out,
           w_in, mlstm_gate_b, mlstm_norm_g, attn_sink, conv_w, ret_log_decay, ret_norm_g,
           w_branch, w_gate, w_out):
    bp, tp, d = x_prompt.shape
    bs, ts, _ = x_sample.shape
    depth = w_mod.shape[0]
    past = cache_attn_k.shape[2]
    kvw = ATT_KV_HEADS * ATT_HEAD_DIM
    fits = [t for t in (1024, 512, 256) if ts % t == 0 and (bp * tp) % t == 0]
    rows = _Rows(bs, ts, bp, tp, fits[0])
    rows_ffn = _Rows(bs, ts, bp, tp, fits[min(1, len(fits) - 1)])

    nbp = -(-(1 + bs) // 8) * 8
    cpad = jnp.concatenate([c_ctx[None], c, jnp.zeros((nbp - 1 - bs, d), F32)], axis=0)
    mod = _modulation(cpad, w_mod, b_mod).reshape(depth, nbp, 3, 3, d)

    w_p = _stage_mixer_proj(w_in)
    ffn1_in, ffn1_out = ffn1_w_in.astype(BF16), ffn1_w_out.astype(BF16)
    ffn2_in, ffn2_out = ffn2_w_in.astype(BF16), ffn2_w_out.astype(BF16)
    tn = _pick_tile(d, 256)
    wg_b = jnp.swapaxes(_column_slabs(w_gate.astype(BF16), tn), 1, 2)
    wb_b = jnp.swapaxes(_column_slabs(w_branch.astype(BF16), tn), 1, 2)
    wo_b = w_out.astype(BF16)

    gate_bias = _gate_lanes(mlstm_gate_b.reshape(depth, 1, N_GATES))
    tabs_att = _rope_tables(ts, ATT_HEAD_DIM)
    tabs_ret = _rope_tables(ts, MIX_DIM)

    x = jnp.concatenate([x_sample.reshape(bs * ts, d), x_prompt.reshape(bp * tp, d)], axis=0)
    ms = bs * ts
    kc_all = cache_attn_k.reshape(bs, depth, past, kvw)
    vc_all = cache_attn_v.reshape(bs, depth, past, kvw)
    ks_, vs_, cs_, ns_, mms_, ss_ = [], [], [], [], [], []
    for l in range(depth):
        g = norm_g[l]
        x = _ffn(rows_ffn, x, mod[l, :, 0], g[0:2], ffn1_in, ffn1_out, l)

        p, h = _inproj(rows, x, mod[l, :, 1], g[2:4], w_p, l)
        a_gain = mlstm_norm_g[l][None]
        d_gain = ret_norm_g[l][None]
        lat_a = _mlstm(p, bs, ts, 0, gate_bias[l], a_gain, (state_mlstm_C, state_mlstm_n, state_mlstm_m, l))[0]
        lat_b = _attn_lat(p, bs, ts, kc_all, vc_all, l, tabs_att, attn_sink[l])
        lat_c = _conv(p, bs, ts, 0, conv_w[l])
        lat_d = _retention(p, bs, ts, 0, ret_log_decay[l], d_gain, (tabs_ret, state_ret_S, l))[0]
        ctx_a, c_fin, n_fin, m_fin = _mlstm(p, bp, tp, ms, gate_bias[l], a_gain, None)
        ctx_b, k_new, v_new = _attn_ctx(p, bp, tp, ms, attn_sink[l])
        ctx_c = _conv(p, bp, tp, ms, conv_w[l])
        ctx_d, s_fin = _retention(p, bp, tp, ms, ret_log_decay[l], d_gain, None)
        y = _merge(rows, h, (lat_a, lat_b, lat_c, lat_d), (ctx_a, ctx_b, ctx_c, ctx_d), wg_b, wb_b, l)
        x = _outproj(rows_ffn, x, mod[l, :, 1], g[2:4], y, wo_b, l)

        x = _ffn(rows_ffn, x, mod[l, :, 2], g[4:6], ffn2_in, ffn2_out, l)

        ks_.append(k_new.reshape(bp, tp, ATT_KV_HEADS, ATT_HEAD_DIM))
        vs_.append(v_new.reshape(bp, tp, ATT_KV_HEADS, ATT_HEAD_DIM))
        cs_.append(c_fin)
        ns_.append(n_fin)
        mms_.append(m_fin[:, :, 0].reshape(bp, 2, MIX_HEADS))
        ss_.append(s_fin)

    y_sample = x[:ms].reshape(bs, ts, d)
    y_prompt = x[ms:].reshape(bp, tp, d)
    return (y_prompt, y_sample, jnp.stack(ks_, axis=1), jnp.stack(vs_, axis=1), jnp.stack(cs_, axis=1),
            jnp.stack(ns_, axis=1), jnp.stack(mms_, axis=1), jnp.stack(ss_, axis=1))
```

```python
import functools

import numpy as np
import jax
import jax.numpy as jnp
from jax import lax
from jax.experimental import pallas as pl
from jax.experimental.pallas import tpu as pltpu

F32 = jnp.float32
BF16 = jnp.bfloat16

N_MOD = 9
N_BRANCH = 4
BRANCH_W = 512
MIX_HEADS = 4
MIX_DIM = 128
RET_GROUP = 2
ATT_HEADS = 8
ATT_KV_HEADS = 2
ATT_HEAD_DIM = 64
ATT_GROUPS = ATT_HEADS // ATT_KV_HEADS
QBLOCK = 128
ATT_AHEAD = 4
CHUNK = 128
GRID_W = 64
ROPE_BASE = 10000.0
EPS = 1e-6
NEG = -1e30

LANE = 128
COL_A_Q, COL_A_K, COL_A_V, COL_A_O = 0, 4, 8, 12
COL_B_Q, COL_B_K, COL_B_V = 16, 20, 21
COL_C_B, COL_C_C, COL_C_X = 22, 26, 30
COL_D_Q, COL_D_K, COL_D_V, COL_D_G = 34, 38, 42, 46
COL_GATES = 50
PROJ_BLOCKS = 52
PROJ_P = PROJ_BLOCKS * LANE

MIB = 1024 * 1024
VMEM_CAP = 44 * MIB
VMEM_TEMPS = 12 * MIB
SINGLE = pl.Buffered(1)


def _params(sem, buffers):
    return pltpu.CompilerParams(dimension_semantics=sem,
                                vmem_limit_bytes=int(min(buffers + VMEM_TEMPS, VMEM_CAP)))


def _pick_tile(n, pref):
    if n <= pref:
        return n
    t = (pref // LANE) * LANE
    while t >= LANE:
        if n % t == 0:
            return t
        t -= LANE
    raise ValueError(f"no lane-aligned tile divides {n}")


def _dot(a, b):
    return jnp.dot(a, b, preferred_element_type=F32)


def _dot_nt(a, b):
    return lax.dot_general(a, b, (((1,), (1,)), ((), ())), preferred_element_type=F32)


def _dot_tn(a, b):
    return lax.dot_general(a, b, (((0,), (0,)), ((), ())), preferred_element_type=F32)


def _sigmoid(x):
    return 1.0 / (1.0 + jnp.exp(-x))


def _rms(xf, g):
    ms = jnp.mean(xf * xf, axis=-1, keepdims=True)
    return xf * lax.rsqrt(ms + EPS) * g


def _modulated(x_ref, mod_ref, g_ref):
    return _rms(x_ref[...], g_ref[0:1, :]) * (1.0 + mod_ref[1:2, :]) + mod_ref[0:1, :]


def _head_norm(x, g):
    mu = jnp.mean(x, axis=-1, keepdims=True)
    xc = x - mu
    var = jnp.mean(xc * xc, axis=-1, keepdims=True)
    return xc * lax.rsqrt(var + EPS) * g


def _chunk_rows(c, n=CHUNK):
    return pl.ds(pl.multiple_of(c * n, n), n)


def _mod_kernel(c_ref, w_ref, b_ref, o_ref):
    c = c_ref[...]
    s = (c * _sigmoid(c)).astype(BF16)
    o_ref[...] = _dot(s, w_ref[...].astype(BF16)) + b_ref[...]


def _modulation(cpad, w_mod, b_mod):
    depth, d, n = w_mod.shape
    nbp = cpad.shape[0]
    tn = _pick_tile(n, 1024)
    return pl.pallas_call(
        _mod_kernel,
        grid=(depth, n // tn),
        in_specs=[
            pl.BlockSpec((nbp, d), lambda l, j: (0, 0)),
            pl.BlockSpec((None, d, tn), lambda l, j: (l, 0, j)),
            pl.BlockSpec((None, 1, tn), lambda l, j: (l, 0, j)),
        ],
        out_specs=pl.BlockSpec((None, nbp, tn), lambda l, j: (l, 0, j)),
        out_shape=jax.ShapeDtypeStruct((depth, nbp, n), F32),
        compiler_params=_params(("parallel", "parallel"), 2 * d * tn * 4 + MIB),
        name="modulation",
    )(cpad, w_mod, b_mod.reshape(depth, 1, n))


class _Rows:
    def __init__(self, bs, ts, bp, tp, tm):
        self.bs, self.ts, self.bp, self.tp = bs, ts, bp, tp
        self.ms, self.mp = bs * ts, bp * tp
        self.m = self.ms + self.mp
        assert ts % tm == 0 and self.mp % tm == 0 and self.ms % tp == 0
        self.tm = tm
        self.lat_tiles = self.ms // tm
        self.tiles_per_seq = ts // tm

    def mod_index(self, i):
        return jnp.where(i < self.lat_tiles, 1 + i // self.tiles_per_seq, 0)

    def x_spec(self, d, mode=None):
        return pl.BlockSpec((self.tm, d), lambda i, *_: (i, 0), pipeline_mode=mode)

    def mod_spec(self, d):
        return pl.BlockSpec((None, 3, d), lambda i, *_: (self.mod_index(i), 0, 0))


def _layer_spec(block, index_map, l, mode=None):
    return pl.BlockSpec((None,) + tuple(block), lambda *ids: (l,) + tuple(index_map(*ids)), pipeline_mode=mode)


def _ffn_kernel(nf, x_ref, mod_ref, g_ref, wg_ref, wu_ref, wo_ref, o_ref, h_scr, acc_scr):
    f = pl.program_id(1)

    def hidden_slab(h):
        tf = wg_ref.shape[1]
        halves = [slice(0, tf // 2), slice(tf // 2, tf)] if tf % (2 * LANE) == 0 else [slice(0, tf)]
        gate_up = [(_dot(h, wg_ref[:, sl]), _dot(h, wu_ref[:, sl])) for sl in halves]
        acts = [((g * _sigmoid(g)) * u).astype(BF16) for g, u in gate_up]
        out = None
        for a, sl in zip(acts, halves):
            o = _dot(a, wo_ref[sl, :])
            out = o if out is None else out + o
        return out

    def pre():
        h = _modulated(x_ref, mod_ref, g_ref).astype(BF16)
        h_scr[...] = h
        return h

    def post(z):
        o_ref[...] = x_ref[...] + (0.5 * mod_ref[2:3, :]) * _rms(z, g_ref[1:2, :])

    if nf == 1:
        post(hidden_slab(pre()))
        return

    @pl.when(f == 0)
    def _():
        acc_scr[...] = hidden_slab(pre())

    @pl.when((f > 0) & (f < nf - 1))
    def _():
        acc_scr[...] += hidden_slab(h_scr[...])

    @pl.when(f == nf - 1)
    def _():
        post(acc_scr[...] + hidden_slab(h_scr[...]))


def _ffn(rows, x, mod_s, g2, w_in, w_out, l):
    m, d = x.shape
    dff = w_out.shape[1]
    tm = rows.tm
    tf = _pick_tile(dff, 512)
    nf = dff // tf
    buffers = tm * d * (2 * 4 + 2 * 4 + 4 + 2) + 3 * 2 * d * tf * 2
    return pl.pallas_call(
        functools.partial(_ffn_kernel, nf),
        grid=(m // tm, nf),
        in_specs=[
            rows.x_spec(d),
            rows.mod_spec(d),
            pl.BlockSpec((2, d), lambda i, f: (0, 0)),
            _layer_spec((d, tf), lambda i, f: (0, f), l),
            _layer_spec((d, tf), lambda i, f: (0, f + nf), l),
            _layer_spec((tf, d), lambda i, f: (f, 0), l),
        ],
        out_specs=pl.BlockSpec((tm, d), lambda i, f: (i, 0)),
        out_shape=jax.ShapeDtypeStruct((m, d), F32),
        scratch_shapes=[pltpu.VMEM((tm, d), BF16), pltpu.VMEM((tm, d), F32)],
        compiler_params=_params(("parallel", "arbitrary"), buffers),
        name="ffn",
    )(x, mod_s, g2, w_in, w_in, w_out)


def _inproj_kernel(x_ref, mod_ref, g_ref, w_ref, p_ref, h_ref):
    @pl.when(pl.program_id(1) == 0)
    def _():
        h = _modulated(x_ref, mod_ref, g_ref).astype(BF16)
        h_ref[...] = h
        p_ref[...] = _dot(h, w_ref[...])

    @pl.when(pl.program_id(1) > 0)
    def _():
        p_ref[...] = _dot(h_ref[...], w_ref[...])


def _inproj(rows, x, mod_s, g2, w_p, l):
    m, d = x.shape
    n = w_p.shape[2]
    tn = _pick_tile(n, 512)
    nt = n // tn
    tm = rows.tm
    return pl.pallas_call(
        _inproj_kernel,
        grid=(m // tm, nt),
        in_specs=[
            rows.x_spec(d),
            rows.mod_spec(d),
            pl.BlockSpec((2, d), lambda i, j: (0, 0)),
            _layer_spec((d, tn), lambda i, j: (0, j), l),
        ],
        out_specs=[pl.BlockSpec((tm, tn), lambda i, j: (i, j)),
                   pl.BlockSpec((tm, d), lambda i, j: (i, 0))],
        out_shape=[jax.ShapeDtypeStruct((m, n), F32), jax.ShapeDtypeStruct((m, d), BF16)],
        compiler_params=_params(("parallel", "arbitrary"),
                                tm * d * (2 * 4 + 2 * 2) + 2 * d * tn * 2 + 2 * tm * tn * 4),
        name="inproj",
    )(x, mod_s, g2, w_p)


def _merge_kernel(lat_tiles, h_ref, *refs):
    lat_refs, ctx_refs = refs[0:N_BRANCH], refs[N_BRANCH:2 * N_BRANCH]
    wg_ref, wb_ref, y_ref = refs[2 * N_BRANCH:]
    h = h_ref[...]
    is_lat = pl.program_id(0) < lat_tiles
    y = None
    for i in range(N_BRANCH):
        o = jnp.where(is_lat, lat_refs[i][...], ctx_refs[i][...])
        term = _sigmoid(_dot(h, wg_ref[i])) * _dot(o, wb_ref[i])
        y = term if y is None else y + term
    y_ref[...] = y.astype(BF16)


def _merge(rows, h, lat, ctx, w_gate, w_branch, l):
    m, d = h.shape
    tm = rows.tm
    nt, _, bw, tn = w_branch.shape[1:]
    assert w_gate.shape[1:] == (nt, N_BRANCH, d, tn)
    last_lat = rows.lat_tiles - 1
    lat_spec = pl.BlockSpec((tm, bw), lambda i, j: (jnp.minimum(i, last_lat), 0))
    ctx_spec = pl.BlockSpec((tm, bw), lambda i, j: (jnp.maximum(i - rows.lat_tiles, 0), 0))
    return pl.pallas_call(
        functools.partial(_merge_kernel, rows.lat_tiles),
        grid=(m // tm, nt),
        in_specs=[
            rows.x_spec(d),
            *([lat_spec] * N_BRANCH), *([ctx_spec] * N_BRANCH),
            _layer_spec((None, N_BRANCH, d, tn), lambda i, j: (j, 0, 0, 0), l),
            _layer_spec((None, N_BRANCH, bw, tn), lambda i, j: (j, 0, 0, 0), l),
        ],
        out_specs=pl.BlockSpec((tm, tn), lambda i, j: (i, j)),
        out_shape=jax.ShapeDtypeStruct((m, d), BF16),
        compiler_params=_params(("parallel", "arbitrary"),
                                2 * tm * d * 2 + 2 * 2 * N_BRANCH * tm * bw * 2
                                + 2 * N_BRANCH * (d + bw) * tn * 2 + 2 * tm * tn * 2),
        name="merge",
    )(h, *lat, *ctx, w_gate, w_branch)


def _outproj_kernel(x_ref, mod_ref, g_ref, y_ref, w_ref, o_ref):
    z = _dot(y_ref[...], w_ref[...])
    o_ref[...] = x_ref[...] + mod_ref[2:3, :] * _rms(z, g_ref[1:2, :])


def _outproj(rows, x, mod_s, g2, y, w_out, l):
    m, d = x.shape
    tm = rows.tm
    return pl.pallas_call(
        _outproj_kernel,
        grid=(m // tm,),
        in_specs=[
            rows.x_spec(d),
            rows.mod_spec(d),
            pl.BlockSpec((2, d), lambda i: (0, 0)),
            rows.x_spec(d),
            _layer_spec((d, d), lambda i: (0, 0), l, SINGLE),
        ],
        out_specs=pl.BlockSpec((tm, d), lambda i: (i, 0)),
        out_shape=jax.ShapeDtypeStruct((m, d), F32),
        compiler_params=_params(("parallel",), tm * d * (2 * 4 + 2 * 4 + 4 + 2 * 2) + d * d * 2),
        name="outproj",
    )(x, mod_s, g2, y, w_out)


def _rope_tables(t, d):
    rows = t // GRID_W
    row = jnp.repeat(jnp.arange(rows), GRID_W).astype(F32)
    col = jnp.tile(jnp.arange(GRID_W), rows).astype(F32)
    quarter = d // 4
    inv = jnp.power(ROPE_BASE, -jnp.arange(quarter, dtype=F32) / quarter)
    ar = row[:, None] * inv
    ac = col[:, None] * inv
    ang = jnp.concatenate([ar, ar, ac, ac], axis=-1)
    cos, sin = jnp.cos(ang), jnp.sin(ang)
    first = (jnp.arange(d) % (2 * quarter)) < quarter
    sa = jnp.where(first, -sin, 0.0)
    sb = jnp.where(first, 0.0, sin)
    reps = LANE // d
    return tuple(jnp.tile(a, (1, reps)) for a in (cos, sa, sb))


def _rope(x, cos, sa, sb, quarter):
    n = x.shape[-1]
    ahead = pltpu.roll(x, n - quarter, 1)
    behind = pltpu.roll(x, quarter, 1)
    return x * cos + ahead * sa + behind * sb


def _log_sigmoid(x):
    return jnp.minimum(x, 0.0) - jnp.log1p(jnp.exp(-jnp.abs(x)))


def _split_bf16(x):
    hi = x.astype(BF16)
    return hi, (x - hi.astype(F32)).astype(BF16)


def _mlstm_kernel(t, layer, emit_state, *refs):
    has_state = layer is not None
    it = iter(refs)
    q_ref, k_ref, v_ref, og_ref, gt_ref, bias_ref, gain_ref = (next(it) for _ in range(7))
    if has_state:
        c0_ref, n0_ref, m0_ref = next(it), next(it), next(it)
    o_ref = next(it)
    if emit_state:
        cout_ref, nout_ref, mout_ref = next(it), next(it), next(it)
    h_scr, s_scr = next(it), next(it)

    L, D, H = CHUNK, MIX_DIM, MIX_HEADS
    nc = t // L
    r_i = lax.broadcasted_iota(jnp.int32, (L, L), 0)
    c_i = lax.broadcasted_iota(jnp.int32, (L, L), 1)
    masks = (r_i >= c_i, r_i <= c_i)
    cum = tuple(mk.astype(BF16) for mk in masks)
    ones = jnp.ones((L, D), BF16)
    sel = [(r_i == g).astype(BF16) for g in range(2 * H)]
    bias = bias_ref[...]
    scale = D ** -0.5

    ms0 = []
    for d in range(2):
        for h in range(H):
            if has_state:
                s_scr[d, h, :, 0:D] = c0_ref[d, h]
                s_scr[d, h, :, D:2 * D] = jnp.broadcast_to(n0_ref[d, h:h + 1, :], (D, D)).T
                ms0.append(jnp.full((1, D), m0_ref[pl.program_id(0), layer, d, h], F32))
            else:
                s_scr[d, h] = jnp.zeros((D, 2 * D), F32)
                ms0.append(jnp.zeros((1, D), F32))

    def zero_rows(c, _):
        h_scr[_chunk_rows(c), :] = jnp.zeros((L, H * D), F32)
        return 0

    lax.fori_loop(0, nc, zero_rows, 0)

    def body(c, ms):
        chains = []
        for d, cc in ((0, c), (1, nc - 1 - c)):
            rows = _chunk_rows(cc)
            g = gt_ref[rows, :] + bias
            lf_hi, lf_lo = _split_bf16(_log_sigmoid(g[:, LANE:2 * LANE]))
            f_all = _dot(cum[d], lf_hi) + _dot(cum[d], lf_lo)
            b_all = g[:, 0:LANE] - f_all
            b_t = b_all.T
            f_hi, f_lo = _split_bf16(f_all)
            b_hi, b_lo = _split_bf16(b_all)
            for h in range(H):
                j = d * H + h
                cols = slice(h * D, (h + 1) * D)
                k = k_ref[rows, cols] * scale
                qb = q_ref[rows, cols].astype(BF16)
                s_old = s_scr[d, h]
                chains.append(dict(
                    d=d, h=h, j=j, rows=rows, cols=cols, k=k, s_old=s_old, brow=b_t[j:j + 1, :],
                    v_aug=jnp.concatenate([v_ref[rows, cols].astype(BF16), ones], axis=1),
                    f_rep=_dot(f_hi, sel[j]) + _dot(f_lo, sel[j]),
                    b_rep=_dot(b_hi, sel[j]) + _dot(b_lo, sel[j]),
                    qk=_dot_nt(qb, k.astype(BF16)),
                    qs=_dot(qb, s_old.astype(BF16))))
        for ch in chains:
            d, f_rep, b_rep, m = ch["d"], ch["f_rep"], ch["b_rep"], ms[ch["j"]]
            ch["f_last"] = f_rep[L - 1:L, :] if d == 0 else f_rep[0:1, :]
            lw = jnp.where(masks[d], f_rep + ch["brow"], NEG)
            inter = f_rep + m
            m_tok = jnp.maximum(jnp.max(lw, axis=-1, keepdims=True), inter)
            ch["a"] = (ch["qk"] * jnp.exp(lw - m_tok)).astype(BF16)
            ch["w_inter"] = jnp.exp(inter - m_tok)
            ch["floor"] = jnp.exp(-m_tok)
            ch["bmax"] = jnp.max(b_rep, axis=0, keepdims=True)
            ch["kw"] = (ch["k"] * jnp.exp(b_rep - ch["bmax"])).astype(BF16)
        for ch in chains:
            ch["pv"] = _dot(ch["a"], ch["v_aug"])
            ch["u_aug"] = _dot_tn(ch["kw"], ch["v_aug"])
        out = list(ms)
        for ch in chains:
            d, h, pv, qs, w_inter, m = ch["d"], ch["h"], ch["pv"], ch["qs"], ch["w_inter"], ms[ch["j"]]
            num = pv[:, 0:D] + w_inter * qs[:, 0:D]
            den = pv[:, D:2 * D] + w_inter * qs[:, D:2 * D]
            h_scr[ch["rows"], ch["cols"]] += num * (1.0 / jnp.maximum(jnp.abs(den), ch["floor"]))
            a_end = ch["f_last"] + ch["bmax"]
            m_new = jnp.maximum(ch["f_last"] + m, a_end)
            d_old = jnp.exp(ch["f_last"] + m - m_new)
            d_new = jnp.exp(a_end - m_new)
            for half in (slice(0, D), slice(D, 2 * D)):
                s_scr[d, h, :, half] = d_old * ch["s_old"][:, half] + d_new * ch["u_aug"][:, half]
            out[ch["j"]] = m_new
        return tuple(out)

    ms = lax.fori_loop(0, nc, body, tuple(ms0))

    gain = gain_ref[...]

    def finish(c, _):
        rows = _chunk_rows(c)
        for h in range(H):
            cols = slice(h * D, (h + 1) * D)
            y = _head_norm(h_scr[rows, cols], gain[:, cols])
            o_ref[rows, cols] = (_sigmoid(og_ref[rows, cols]) * y).astype(BF16)
        return 0

    lax.fori_loop(0, nc, finish, 0)

    if emit_state:
        for d in range(2):
            for h in range(H):
                cout_ref[d, h] = s_scr[d, h, :, 0:D]
                nout_ref[d, h:h + 1, :] = s_scr[d, h, :, D:2 * D].T[0:1, :]
                mout_ref[d * H + h:d * H + h + 1, :] = ms[d * H + h]


def _mlstm(p, nb, t, row_off, bias, gain, state):
    assert row_off % t == 0
    ob = row_off // t
    has_state = state is not None
    emit_state = not has_state
    hw = MIX_HEADS * MIX_DIM
    st_c = pl.BlockSpec((None, 2, MIX_HEADS, MIX_DIM, MIX_DIM), lambda b: (b, 0, 0, 0, 0))
    st_n = pl.BlockSpec((None, 2, MIX_HEADS, MIX_DIM), lambda b: (b, 0, 0, 0))

    def col(cb):
        return pl.BlockSpec((t, hw), lambda b: (ob + b, cb * LANE // hw), pipeline_mode=mode)

    mode = SINGLE if t * hw * 4 >= 2 * MIB else None
    nbuf = 1 if mode is SINGLE else 2
    buffers = (nbuf * (4 * t * hw + t * 2 * LANE) * 4 + 2 * t * hw * 2 + t * hw * 4
               + 6 * 2 * MIX_HEADS * MIX_DIM * MIX_DIM * 4)
    in_specs = [col(COL_A_Q), col(COL_A_K), col(COL_A_V), col(COL_A_O),
                pl.BlockSpec((t, 2 * LANE), lambda b: (ob + b, COL_GATES // 2), pipeline_mode=mode),
                pl.BlockSpec((1, 2 * LANE), lambda b: (0, 0)),
                pl.BlockSpec((1, hw), lambda b: (0, 0))]
    args = [p, p, p, p, p, bias, gain]
    layer = None
    if has_state:
        layer = state[3]
        in_specs += [pl.BlockSpec((None, None, 2, MIX_HEADS, MIX_DIM, MIX_DIM), lambda b: (b, layer, 0, 0, 0, 0)),
                     pl.BlockSpec((None, None, 2, MIX_HEADS, MIX_DIM), lambda b: (b, layer, 0, 0, 0)),
                     pl.BlockSpec(memory_space=pltpu.SMEM)]
        args += list(state[0:3])
    out_shape = [jax.ShapeDtypeStruct((nb * t, hw), BF16)]
    out_specs = [pl.BlockSpec((t, hw), lambda b: (b, 0))]
    if emit_state:
        out_shape += [jax.ShapeDtypeStruct((nb, 2, MIX_HEADS, MIX_DIM, MIX_DIM), F32),
                      jax.ShapeDtypeStruct((nb, 2, MIX_HEADS, MIX_DIM), F32),
                      jax.ShapeDtypeStruct((nb, 2 * MIX_HEADS, MIX_DIM), F32)]
        out_specs += [st_c, st_n, pl.BlockSpec((None, 2 * MIX_HEADS, MIX_DIM), lambda b: (b, 0, 0))]
    return pl.pallas_call(
        functools.partial(_mlstm_kernel, t, layer, emit_state),
        grid=(nb,),
        in_specs=in_specs,
        out_specs=out_specs,
        out_shape=out_shape,
        scratch_shapes=[pltpu.VMEM((t, hw), F32),
                        pltpu.VMEM((2, MIX_HEADS, MIX_DIM, 2 * MIX_DIM), F32)],
        compiler_params=_params(("parallel",), buffers),
        name="mlstm_lat" if has_state else "mlstm_ctx",
    )(*args)


def _ret_kernel(t, has_state, emit_state, *refs):
    it = iter(refs)
    q_ref, k_ref, v_ref, dg_ref, lg_ref, gain_ref = (next(it) for _ in range(6))
    if has_state:
        cos_ref, sa_ref, sb_ref, s0_ref = (next(it) for _ in range(4))
    o_ref = next(it)
    if emit_state:
        sout_ref = next(it)
    q_scr, u_scr, o_scr, s_scr = (next(it) for _ in range(4))

    L, D, G = CHUNK, MIX_DIM, RET_GROUP
    nc = t // L
    diff = (lax.broadcasted_iota(jnp.int32, (L, L), 0) - lax.broadcasted_iota(jnp.int32, (L, L), 1)).astype(F32)
    tcol = lax.broadcasted_iota(jnp.int32, (L, 1), 0).astype(F32)
    scale = D ** -0.5
    dm, w_end, w_cross, decay = [], [], [], []
    for j in range(G):
        hx = pl.program_id(1) * G + j
        lgs = (lg_ref[0, hx], lg_ref[1, hx])
        dm.append(jnp.where(diff >= 0.0, jnp.exp(jnp.maximum(diff, 0.0) * lgs[0]), 0.0)
                  + jnp.where(diff <= 0.0, jnp.exp(jnp.maximum(-diff, 0.0) * lgs[1]), 0.0))
        w_end.append((jnp.exp((L - 1.0 - tcol) * lgs[0]), jnp.exp(tcol * lgs[1])))
        w_cross.append((jnp.exp((tcol + 1.0) * lgs[0]), jnp.exp((L - tcol) * lgs[1])))
        decay.append(tuple(jnp.exp(jnp.full((1, 1), float(L), F32) * lgs[d]) for d in range(2)))
        for d in range(2):
            s_scr[d, j] = s0_ref[d, j] if has_state else jnp.zeros((D, D), F32)

    def local(cp, _):
        chains = []
        for c in (2 * cp, 2 * cp + 1):
            rows = _chunk_rows(c)
            if has_state:
                cos, sa, sb = cos_ref[rows, :], sa_ref[rows, :], sb_ref[rows, :]
            for j in range(G):
                cols = slice(j * D, (j + 1) * D)
                q = q_ref[rows, cols]
                k = k_ref[rows, cols] * scale
                if has_state:
                    q = _rope(q, cos, sa, sb, D // 4)
                    k = _rope(k, cos, sa, sb, D // 4)
                qb = q.astype(BF16)
                vb = v_ref[rows, cols].astype(BF16)
                q_scr[rows, cols] = qb
                kw = jnp.concatenate([(k * w_end[j][0]).astype(BF16), (k * w_end[j][1]).astype(BF16)], axis=1)
                u_scr[c, j] = _dot_tn(kw, vb)
                chains.append((rows, cols, j, vb, _dot_nt(qb, k.astype(BF16))))
        weighted = [(qk * dm[j]).astype(BF16) for _, _, j, _, qk in chains]
        for (rows, cols, _, vb, _), a in zip(chains, weighted):
            o_scr[rows, cols] = _dot(a, vb)
        return 0

    assert nc % 2 == 0
    lax.fori_loop(0, nc // 2, local, 0)

    def scan(c, _):
        chains = []
        for d, cc in ((0, c), (1, nc - 1 - c)):
            rows = _chunk_rows(cc)
            for j in range(G):
                cols = slice(j * D, (j + 1) * D)
                s_old = s_scr[d, j]
                chains.append((d, cc, j, rows, cols, s_old, _dot(q_scr[rows, cols], s_old.astype(BF16))))
        for d, cc, j, rows, cols, s_old, cross in chains:
            o_scr[rows, cols] += cross * w_cross[j][d]
            s_scr[d, j] = decay[j][d] * s_old + u_scr[cc, j, d * D:(d + 1) * D, :]
        return 0

    lax.fori_loop(0, nc, scan, 0, unroll=2)

    gain = gain_ref[...]

    def finish(c, _):
        rows = _chunk_rows(c)
        for j in range(G):
            cols = slice(j * D, (j + 1) * D)
            y = _head_norm(o_scr[rows, cols], gain[:, cols])
            dg = dg_ref[rows, cols]
            o_ref[rows, cols] = ((dg * _sigmoid(dg)) * y).astype(BF16)
        return 0

    lax.fori_loop(0, nc, finish, 0, unroll=2)

    if emit_state:
        for d in range(2):
            for j in range(G):
                sout_ref[d, j] = s_scr[d, j]


def _retention(p, nb, t, row_off, log_decay, gain, state):
    assert row_off % t == 0 and MIX_HEADS % RET_GROUP == 0 and COL_D_Q % RET_GROUP == 0
    ob = row_off // t
    has_state = state is not None
    emit_state = not has_state
    G = RET_GROUP
    gw = G * MIX_DIM

    def col(cb):
        return pl.BlockSpec((t, gw), lambda b, h: (ob + b, cb // G + h))

    in_specs = [col(COL_D_Q), col(COL_D_K), col(COL_D_V), col(COL_D_G),
                pl.BlockSpec(memory_space=pltpu.SMEM),
                pl.BlockSpec((1, gw), lambda b, h: (0, h))]
    args = [p, p, p, p, log_decay, gain]
    if has_state:
        (cos, sa, sb), s0, layer = state
        tab = pl.BlockSpec((t, LANE), lambda b, h: (0, 0))
        in_specs += [tab, tab, tab,
                     pl.BlockSpec((None, None, 2, G, MIX_DIM, MIX_DIM), lambda b, h: (b, layer, 0, h, 0, 0))]
        args += [cos, sa, sb, s0]
    out_shape = [jax.ShapeDtypeStruct((nb * t, BRANCH_W), BF16)]
    out_specs = [pl.BlockSpec((t, gw), lambda b, h: (b, h))]
    if emit_state:
        out_shape.append(jax.ShapeDtypeStruct((nb, 2, MIX_HEADS, MIX_DIM, MIX_DIM), F32))
        out_specs.append(pl.BlockSpec((None, 2, G, MIX_DIM, MIX_DIM), lambda b, h: (b, 0, h, 0, 0)))
    buffers = (2 * 4 * t * gw * 4 + 2 * 3 * t * LANE * 4 + 2 * t * gw * 2
               + t * gw * (2 + 4) + (t // CHUNK) * G * 2 * MIX_DIM * MIX_DIM * 4 + 6 * G * MIX_DIM * MIX_DIM * 4)
    return pl.pallas_call(
        functools.partial(_ret_kernel, t, has_state, emit_state),
        grid=(nb, MIX_HEADS // G),
        in_specs=in_specs,
        out_specs=out_specs,
        out_shape=out_shape,
        scratch_shapes=[pltpu.VMEM((t, gw), BF16),
                        pltpu.VMEM((t // CHUNK, G, 2 * MIX_DIM, MIX_DIM), F32),
                        pltpu.VMEM((t, gw), F32),
                        pltpu.VMEM((2, G, MIX_DIM, MIX_DIM), F32)],
        compiler_params=_params(("parallel", "parallel"), buffers),
        name="ret_lat" if has_state else "ret_ctx",
    )(*args)


def _softmax_sink_pv(s_parts, v_parts, sink):
    m = sink
    for s in s_parts:
        m = jnp.maximum(m, jnp.max(s, axis=-1, keepdims=True))
    es = [jnp.exp(s - m) for s in s_parts]
    den = jnp.exp(sink - m)
    for e in es:
        den = den + jnp.sum(e, axis=-1, keepdims=True)
    out = None
    for e, v in zip(es, v_parts):
        o = _dot(e.astype(BF16), v)
        out = o if out is None else out + o
    return out * (1.0 / den)


def _attn_ctx_kernel(q_ref, k_ref, v_ref, sink_ref, o_ref, ko_ref, vo_ref):
    k = k_ref[...]
    v = v_ref[...]
    ko_ref[...] = k
    vo_ref[...] = v
    hd = ATT_HEAD_DIM
    scale = hd ** -0.5
    kjs = [k[:, j * hd:(j + 1) * hd].astype(BF16) for j in range(ATT_KV_HEADS)]
    vjs = [v[:, j * hd:(j + 1) * hd].astype(BF16) for j in range(ATT_KV_HEADS)]

    def scores(hx):
        return [_dot_nt(q_ref[:, hx * hd:(hx + 1) * hd].astype(BF16), kjs[hx // ATT_GROUPS]) * scale]

    s_next = scores(0)
    for hx in range(ATT_HEADS):
        s_parts = s_next
        if hx + 1 < ATT_HEADS:
            s_next = scores(hx + 1)
        sink = jnp.full((1, 1), sink_ref[hx], F32)
        o_ref[:, hx * hd:(hx + 1) * hd] = _softmax_sink_pv(s_parts, [vjs[hx // ATT_GROUPS]], sink).astype(BF16)


def _attn_ctx(p, nb, t, row_off, sink):
    assert row_off % t == 0
    ob = row_off // t
    qw = ATT_HEADS * ATT_HEAD_DIM
    kvw = ATT_KV_HEADS * ATT_HEAD_DIM
    return pl.pallas_call(
        _attn_ctx_kernel,
        grid=(nb,),
        in_specs=[
            pl.BlockSpec((t, qw), lambda b: (ob + b, COL_B_Q * LANE // qw)),
            pl.BlockSpec((t, kvw), lambda b: (ob + b, COL_B_K)),
            pl.BlockSpec((t, kvw), lambda b: (ob + b, COL_B_V)),
            pl.BlockSpec(memory_space=pltpu.SMEM),
        ],
        out_specs=[pl.BlockSpec((t, qw), lambda b: (b, 0)),
                   pl.BlockSpec((None, t, kvw), lambda b: (b, 0, 0)),
                   pl.BlockSpec((None, t, kvw), lambda b: (b, 0, 0))],
        out_shape=[jax.ShapeDtypeStruct((nb * t, qw), BF16),
                   jax.ShapeDtypeStruct((nb, t, kvw), F32),
                   jax.ShapeDtypeStruct((nb, t, kvw), F32)],
        compiler_params=_params(("parallel",), 2 * t * (qw * 6 + 4 * kvw * 4)),
        name="attn_ctx",
    )(p, p, p, sink)


def _band_bias():
    Q = QBLOCK
    r = np.arange(Q)[:, None]
    c = np.arange(3 * Q)[None, :]
    band = (c >= r) & (c <= r + 2 * Q)
    out = []
    for kind in range(3):
        ok = band.copy()
        if kind == 0:
            ok &= c >= Q
        if kind == 2:
            ok &= c < 2 * Q
        out.append(np.where(ok, 0.0, NEG).astype(np.float32))
    return jnp.asarray(np.stack(out))


def _half_variants(x):
    low = lax.broadcasted_iota(jnp.int32, x.shape, 1) < ATT_HEAD_DIM
    sw = pltpu.roll(x, ATT_HEAD_DIM, 1)
    zero = jnp.zeros_like(x)
    return [jnp.where(low, a, b).astype(BF16) for a, b in ((x, zero), (zero, sw), (sw, zero), (zero, x))]


def _attn_lat_kernel(t, q_ref, k_ref, v_ref, kc_ref, vc_ref, cq_ref, saq_ref, sbq_ref,
                     ck_ref, sak_ref, sbk_ref, bias_ref, sink_ref, o_ref, kp_scr, vp_scr, kc_scr, vc_scr):
    n = pl.program_id(1)
    Q = QBLOCK
    hd = ATT_HEAD_DIM
    quarter = hd // 4
    nvar = 2 * ATT_KV_HEADS

    @pl.when(n == 0)
    def _():
        zero = jnp.zeros((Q, LANE), BF16)
        for vi in range(nvar):
            kp_scr[vi, 0:Q, :] = zero
            vp_scr[vi, 0:Q, :] = zero
            kp_scr[vi, Q + t:2 * Q + t, :] = zero
            vp_scr[vi, Q + t:2 * Q + t, :] = zero

        def fill(c, _):
            rows = _chunk_rows(c, Q)
            dst = pl.ds(pl.multiple_of(c * Q + Q, Q), Q)
            kk = _rope(k_ref[rows, :], ck_ref[rows, :], sak_ref[rows, :], sbk_ref[rows, :], quarter)
            for vi, (kv, vv) in enumerate(zip(_half_variants(kk), _half_variants(v_ref[rows, :]))):
                kp_scr[vi, dst, :] = kv
                vp_scr[vi, dst, :] = vv
            return 0

        lax.fori_loop(0, t // Q, fill, 0)
        for vi, (kv, vv) in enumerate(zip(_half_variants(kc_ref[...]), _half_variants(vc_ref[...]))):
            kc_scr[vi] = kv
            vc_scr[vi] = vv

    band = pl.ds(pl.multiple_of(n * Q, Q), 3 * Q)
    cq, saq, sbq = cq_ref[...], saq_ref[...], sbq_ref[...]
    bias = bias_ref[...]
    scale = hd ** -0.5
    heads_per_chunk = LANE // hd
    qcs = [(_rope(q_ref[:, c * LANE:(c + 1) * LANE], cq, saq, sbq, quarter) * scale).astype(BF16)
           for c in range(ATT_HEADS // heads_per_chunk)]

    def variant(hx):
        return 2 * (hx // ATT_GROUPS) + hx % heads_per_chunk

    def scores(hx):
        qc, vi = qcs[hx // heads_per_chunk], variant(hx)
        return [_dot_nt(qc, kp_scr[vi, band, :]) + bias, _dot_nt(qc, kc_scr[vi])]

    acc = None
    pending = [scores(hx) for hx in range(ATT_AHEAD)]
    for hx in range(ATT_HEADS):
        s_parts = pending.pop(0)
        if hx + ATT_AHEAD < ATT_HEADS:
            pending.append(scores(hx + ATT_AHEAD))
        vi = variant(hx)
        sink = jnp.full((1, 1), sink_ref[hx], F32)
        o = _softmax_sink_pv(s_parts, [vp_scr[vi, band, :], vc_scr[vi]], sink)
        acc = o if acc is None else acc + o
        if hx % heads_per_chunk == heads_per_chunk - 1:
            c = hx // heads_per_chunk
            o_ref[:, c * LANE:(c + 1) * LANE] = acc.astype(BF16)
            acc = None


def _attn_lat(p, nb, t, kc, vc, layer, tabs, sink):
    assert (ATT_HEAD_DIM ** -0.5) == 2.0 ** round(np.log2(ATT_HEAD_DIM ** -0.5))
    qw = ATT_HEADS * ATT_HEAD_DIM
    kvw = ATT_KV_HEADS * ATT_HEAD_DIM
    assert kvw == LANE
    nq = t // QBLOCK
    past = kc.shape[2]
    nvar = 2 * ATT_KV_HEADS
    cos, sa, sb = tabs
    bias = _band_bias()
    tq = pl.BlockSpec((QBLOCK, LANE), lambda b, n: (n, 0))
    tk = pl.BlockSpec((t, LANE), lambda b, n: (0, 0))
    return pl.pallas_call(
        functools.partial(_attn_lat_kernel, t),
        grid=(nb, nq),
        in_specs=[
            pl.BlockSpec((QBLOCK, qw), lambda b, n: (b * nq + n, COL_B_Q * LANE // qw)),
            pl.BlockSpec((t, kvw), lambda b, n: (b, COL_B_K)),
            pl.BlockSpec((t, kvw), lambda b, n: (b, COL_B_V)),
            pl.BlockSpec((None, None, past, kvw), lambda b, n: (b, layer, 0, 0)),
            pl.BlockSpec((None, None, past, kvw), lambda b, n: (b, layer, 0, 0)),
            tq, tq, tq, tk, tk, tk,
            pl.BlockSpec((None,) + bias.shape[1:],
                         lambda b, n: (jnp.where(n == 0, 0, jnp.where(n == nq - 1, 2, 1)), 0, 0)),
            pl.BlockSpec(memory_space=pltpu.SMEM),
        ],
        out_specs=pl.BlockSpec((QBLOCK, qw), lambda b, n: (b * nq + n, 0)),
        out_shape=jax.ShapeDtypeStruct((nb * t, qw), BF16),
        scratch_shapes=[pltpu.VMEM((nvar, t + 2 * QBLOCK, LANE), BF16), pltpu.VMEM((nvar, t + 2 * QBLOCK, LANE), BF16),
                        pltpu.VMEM((nvar, past, LANE), BF16), pltpu.VMEM((nvar, past, LANE), BF16)],
        compiler_params=_params(("parallel", "arbitrary"),
                                2 * 5 * t * LANE * 4 + 2 * nvar * (t + 2 * QBLOCK + past) * LANE * 2 + 2 * MIB),
        name="attn_lat",
    )(p, p, p, kc, vc, cos, sa, sb, cos, sa, sb, bias, sink)


def _conv_kernel(t, cb_ref, cc_ref, cx_ref, w_ref, o_ref):
    u = cc_ref[...] * cx_ref[...]
    tok = lax.broadcasted_iota(jnp.int32, u.shape, 0)
    prev = jnp.where(tok == 0, 0.0, pltpu.roll(u, 1, 0))
    nxt = jnp.where(tok == t - 1, 0.0, pltpu.roll(u, t - 1, 0))
    o_ref[...] = (cb_ref[...] * (w_ref[0:1, :] * prev + w_ref[1:2, :] * u + w_ref[2:3, :] * nxt)).astype(BF16)


def _conv(p, nb, t, row_off, w):
    assert row_off % t == 0
    ob = row_off // t
    nblk = w.shape[1] // LANE

    def col(cb):
        return pl.BlockSpec((t, LANE), lambda b, j: (ob + b, cb + j))

    return pl.pallas_call(
        functools.partial(_conv_kernel, t),
        grid=(nb, nblk),
        in_specs=[col(COL_C_B), col(COL_C_C), col(COL_C_X),
                  pl.BlockSpec((w.shape[0], LANE), lambda b, j: (0, j))],
        out_specs=pl.BlockSpec((t, LANE), lambda b, j: (b, j)),
        out_shape=jax.ShapeDtypeStruct((nb * t, w.shape[1]), BF16),
        compiler_params=_params(("parallel", "parallel"), 2 * 4 * t * LANE * 4),
        name="conv",
    )(p, p, p, w)


GATE_SRC = 4 * MIX_HEADS * MIX_DIM
N_GATES = 4 * MIX_HEADS


def _gate_lanes(x):
    g = x.reshape(x.shape[:-1] + (2, 2, MIX_HEADS))
    pad = [(0, 0)] * (x.ndim - 1) + [(0, LANE - 2 * MIX_HEADS)]
    blocks = [jnp.pad(g[..., kind, :].reshape(x.shape[:-1] + (2 * MIX_HEADS,)), pad) for kind in range(2)]
    return jnp.concatenate(blocks, axis=-1)


def _column_slabs(w, tile):
    k, n = w.shape[-2:]
    return jnp.swapaxes(w.reshape(w.shape[:-1] + (n // tile, tile)), -3, -2)


def _stage_mixer_proj(w_in):
    head = w_in[..., 0:GATE_SRC]
    tail = w_in[..., GATE_SRC + N_GATES:]
    gates = _gate_lanes(w_in[..., GATE_SRC:GATE_SRC + N_GATES])
    assert head.shape[-1] + tail.shape[-1] == COL_GATES * LANE
    assert head.shape[-1] + tail.shape[-1] + gates.shape[-1] == PROJ_P
    return jnp.concatenate([head, tail, gates], axis=-1).astype(BF16)


def kernel(x_prompt, x_sample, cache_attn_k, cache_attn_v, state_mlstm_C, state_mlstm_n, state_mlstm_m,
           state_ret_S, c, c_ctx, w_mod, b_mod, norm_g, ffn1_w_in, ffn1_w_out, ffn2_w_in, ffn2_w_out,
           w_in, mlstm_gate_b, mlstm_norm_g, attn_sink, conv_w, ret_log_decay, ret_norm_g,
           w_branch, w_gate, w_out):
    bp, tp, d = x_prompt.shape
    bs, ts, _ = x_sample.shape
    depth = w_mod.shape[0]
    past = cache_attn_k.shape[2]
    kvw = ATT_KV_HEADS * ATT_HEAD_DIM
    fits = [t for t in (1024, 512, 256) if ts % t == 0 and (bp * tp) % t == 0]
    rows = _Rows(bs, ts, bp, tp, fits[0])
    rows_ffn = _Rows(bs, ts, bp, tp, fits[min(1, len(fits) - 1)])

    nbp = -(-(1 + bs) // 8) * 8
    cpad = jnp.concatenate([c_ctx[None], c, jnp.zeros((nbp - 1 - bs, d), F32)], axis=0)
    mod = _modulation(cpad, w_mod, b_mod).reshape(depth, nbp, 3, 3, d)

    w_p = _stage_mixer_proj(w_in)
    ffn1_in, ffn1_out = ffn1_w_in.astype(BF16), ffn1_w_out.astype(BF16)
    ffn2_in, ffn2_out = ffn2_w_in.astype(BF16), ffn2_w_out.astype(BF16)
    tn = _pick_tile(d, 256)
    wg_b = jnp.swapaxes(_column_slabs(w_gate.astype(BF16), tn), 1, 2)
    wb_b = jnp.swapaxes(_column_slabs(w_branch.astype(BF16), tn), 1, 2)
    wo_b = w_out.astype(BF16)

    gate_bias = _gate_lanes(mlstm_gate_b.reshape(depth, 1, N_GATES))
    tabs_att = _rope_tables(ts, ATT_HEAD_DIM)
    tabs_ret = _rope_tables(ts, MIX_DIM)

    x = jnp.concatenate([x_sample.reshape(bs * ts, d), x_prompt.reshape(bp * tp, d)], axis=0)
    ms = bs * ts
    kc_all = cache_attn_k.reshape(bs, depth, past, kvw)
    vc_all = cache_attn_v.reshape(bs, depth, past, kvw)
    ks_, vs_, cs_, ns_, mms_, ss_ = [], [], [], [], [], []
    for l in range(depth):
        g = norm_g[l]
        x = _ffn(rows_ffn, x, mod[l, :, 0], g[0:2], ffn1_in, ffn1_out, l)

        p, h = _inproj(rows, x, mod[l, :, 1], g[2:4], w_p, l)
        a_gain = mlstm_norm_g[l][None]
        d_gain = ret_norm_g[l][None]
        lat_a = _mlstm(p, bs, ts, 0, gate_bias[l], a_gain, (state_mlstm_C, state_mlstm_n, state_mlstm_m, l))[0]
        lat_b = _attn_lat(p, bs, ts, kc_all, vc_all, l, tabs_att, attn_sink[l])
        lat_c = _conv(p, bs, ts, 0, conv_w[l])
        lat_d = _retention(p, bs, ts, 0, ret_log_decay[l], d_gain, (tabs_ret, state_ret_S, l))[0]
        ctx_a, c_fin, n_fin, m_fin = _mlstm(p, bp, tp, ms, gate_bias[l], a_gain, None)
        ctx_b, k_new, v_new = _attn_ctx(p, bp, tp, ms, attn_sink[l])
        ctx_c = _conv(p, bp, tp, ms, conv_w[l])
        ctx_d, s_fin = _retention(p, bp, tp, ms, ret_log_decay[l], d_gain, None)
        y = _merge(rows, h, (lat_a, lat_b, lat_c, lat_d), (ctx_a, ctx_b, ctx_c, ctx_d), wg_b, wb_b, l)
        x = _outproj(rows_ffn, x, mod[l, :, 1], g[2:4], y, wo_b, l)

        x = _ffn(rows_ffn, x, mod[l, :, 2], g[4:6], ffn2_in, ffn2_out, l)

        ks_.append(k_new.reshape(bp, tp, ATT_KV_HEADS, ATT_HEAD_DIM))
        vs_.append(v_new.reshape(bp, tp, ATT_KV_HEADS, ATT_HEAD_DIM))
        cs_.append(c_fin)
        ns_.append(n_fin)
        mms_.append(m_fin[:, :, 0].reshape(bp, 2, MIX_HEADS))
        ss_.append(s_fin)

    y_sample = x[:ms].reshape(bs, ts, d)
    y_prompt = x[ms:].reshape(bp, tp, d)
    return (y_prompt, y_sample, jnp.stack(ks_, axis=1), jnp.stack(vs_, axis=1), jnp.stack(cs_, axis=1),
            jnp.stack(ns_, axis=1), jnp.stack(mms_, axis=1), jnp.stack(ss_, axis=1))
```

```python
import functools

import numpy as np
import jax
import jax.numpy as jnp
from jax import lax
from jax.experimental import pallas as pl
from jax.experimental.pallas import tpu as pltpu

F32 = jnp.float32
BF16 = jnp.bfloat16

N_MOD = 9
N_BRANCH = 4
BRANCH_W = 512
MIX_HEADS = 4
MIX_DIM = 128
RET_GROUP = 2
ATT_HEADS = 8
ATT_KV_HEADS = 2
ATT_HEAD_DIM = 64
ATT_GROUPS = ATT_HEADS // ATT_KV_HEADS
QBLOCK = 128
ATT_AHEAD = 4
CHUNK = 128
GRID_W = 64
ROPE_BASE = 10000.0
EPS = 1e-6
NEG = -1e30

LANE = 128
COL_A_Q, COL_A_K, COL_A_V, COL_A_O = 0, 4, 8, 12
COL_B_Q, COL_B_K, COL_B_V = 16, 20, 21
COL_C_B, COL_C_C, COL_C_X = 22, 26, 30
COL_D_Q, COL_D_K, COL_D_V, COL_D_G = 34, 38, 42, 46
COL_GATES = 50
PROJ_BLOCKS = 52
PROJ_P = PROJ_BLOCKS * LANE

MIB = 1024 * 1024
VMEM_CAP = 44 * MIB
VMEM_TEMPS = 12 * MIB
SINGLE = pl.Buffered(1)


def _params(sem, buffers):
    return pltpu.CompilerParams(dimension_semantics=sem,
                                vmem_limit_bytes=int(min(buffers + VMEM_TEMPS, VMEM_CAP)))


def _pick_tile(n, pref):
    if n <= pref:
        return n
    t = (pref // LANE) * LANE
    while t >= LANE:
        if n % t == 0:
            return t
        t -= LANE
    raise ValueError(f"no lane-aligned tile divides {n}")


def _dot(a, b):
    return jnp.dot(a, b, preferred_element_type=F32)


def _dot_nt(a, b):
    return lax.dot_general(a, b, (((1,), (1,)), ((), ())), preferred_element_type=F32)


def _dot_tn(a, b):
    return lax.dot_general(a, b, (((0,), (0,)), ((), ())), preferred_element_type=F32)


def _sigmoid(x):
    return 1.0 / (1.0 + jnp.exp(-x))


def _rms(xf, g):
    ms = jnp.mean(xf * xf, axis=-1, keepdims=True)
    return xf * lax.rsqrt(ms + EPS) * g


def _modulated(x_ref, mod_ref, g_ref):
    return _rms(x_ref[...], g_ref[0:1, :]) * (1.0 + mod_ref[1:2, :]) + mod_ref[0:1, :]


def _head_norm(x, g):
    mu = jnp.mean(x, axis=-1, keepdims=True)
    xc = x - mu
    var = jnp.mean(xc * xc, axis=-1, keepdims=True)
    return xc * lax.rsqrt(var + EPS) * g


def _chunk_rows(c, n=CHUNK):
    return pl.ds(pl.multiple_of(c * n, n), n)


def _mod_kernel(c_ref, w_ref, b_ref, o_ref):
    c = c_ref[...]
    s = (c * _sigmoid(c)).astype(BF16)
    o_ref[...] = _dot(s, w_ref[...].astype(BF16)) + b_ref[...]


def _modulation(cpad, w_mod, b_mod):
    depth, d, n = w_mod.shape
    nbp = cpad.shape[0]
    tn = _pick_tile(n, 1024)
    return pl.pallas_call(
        _mod_kernel,
        grid=(depth, n // tn),
        in_specs=[
            pl.BlockSpec((nbp, d), lambda l, j: (0, 0)),
            pl.BlockSpec((None, d, tn), lambda l, j: (l, 0, j)),
            pl.BlockSpec((None, 1, tn), lambda l, j: (l, 0, j)),
        ],
        out_specs=pl.BlockSpec((None, nbp, tn), lambda l, j: (l, 0, j)),
        out_shape=jax.ShapeDtypeStruct((depth, nbp, n), F32),
        compiler_params=_params(("parallel", "parallel"), 2 * d * tn * 4 + MIB),
        name="modulation",
    )(cpad, w_mod, b_mod.reshape(depth, 1, n))


class _Rows:
    def __init__(self, bs, ts, bp, tp, tm):
        self.bs, self.ts, self.bp, self.tp = bs, ts, bp, tp
        self.ms, self.mp = bs * ts, bp * tp
        self.m = self.ms + self.mp
        assert ts % tm == 0 and self.mp % tm == 0 and self.ms % tp == 0
        self.tm = tm
        self.lat_tiles = self.ms // tm
        self.tiles_per_seq = ts // tm

    def mod_index(self, i):
        return jnp.where(i < self.lat_tiles, 1 + i // self.tiles_per_seq, 0)

    def x_spec(self, d, mode=None):
        return pl.BlockSpec((self.tm, d), lambda i, *_: (i, 0), pipeline_mode=mode)

    def mod_spec(self, d):
        return pl.BlockSpec((None, 3, d), lambda i, *_: (self.mod_index(i), 0, 0))


P_SLAB = 512


def _p_spec(rows, width, row_col, mode=None):
    per = P_SLAB // width
    assert per * width == P_SLAB

    def index_map(*ids):
        r, c = row_col(*ids)
        return (c // per, r, c % per)

    return pl.BlockSpec((None, rows, width), index_map, pipeline_mode=mode)


def _layer_spec(block, index_map, l, mode=None):
    return pl.BlockSpec((None,) + tuple(block), lambda *ids: (l,) + tuple(index_map(*ids)), pipeline_mode=mode)


def _ffn_kernel(nf, x_ref, mod_ref, g_ref, wg_ref, wu_ref, wo_ref, o_ref, h_scr, acc_scr):
    f = pl.program_id(1)

    def hidden_slab(h):
        tf = wg_ref.shape[1]
        halves = [slice(0, tf // 2), slice(tf // 2, tf)] if tf % (2 * LANE) == 0 else [slice(0, tf)]
        gate_up = [(_dot(h, wg_ref[:, sl]), _dot(h, wu_ref[:, sl])) for sl in halves]
        acts = [((g * _sigmoid(g)) * u).astype(BF16) for g, u in gate_up]
        out = None
        for a, sl in zip(acts, halves):
            o = _dot(a, wo_ref[sl, :])
            out = o if out is None else out + o
        return out

    def pre():
        h = _modulated(x_ref, mod_ref, g_ref).astype(BF16)
        h_scr[...] = h
        return h

    def post(z):
        o_ref[...] = x_ref[...] + (0.5 * mod_ref[2:3, :]) * _rms(z, g_ref[1:2, :])

    if nf == 1:
        post(hidden_slab(pre()))
        return

    @pl.when(f == 0)
    def _():
        acc_scr[...] = hidden_slab(pre())

    @pl.when((f > 0) & (f < nf - 1))
    def _():
        acc_scr[...] += hidden_slab(h_scr[...])

    @pl.when(f == nf - 1)
    def _():
        post(acc_scr[...] + hidden_slab(h_scr[...]))


def _ffn(rows, x, mod_s, g2, w_in, w_out, l):
    m, d = x.shape
    dff = w_out.shape[1]
    tm = rows.tm
    tf = _pick_tile(dff, 512)
    nf = dff // tf
    buffers = tm * d * (2 * 4 + 2 * 4 + 4 + 2) + 3 * 2 * d * tf * 2
    return pl.pallas_call(
        functools.partial(_ffn_kernel, nf),
        grid=(m // tm, nf),
        in_specs=[
            rows.x_spec(d),
            rows.mod_spec(d),
            pl.BlockSpec((2, d), lambda i, f: (0, 0)),
            _layer_spec((d, tf), lambda i, f: (0, f), l),
            _layer_spec((d, tf), lambda i, f: (0, f + nf), l),
            _layer_spec((tf, d), lambda i, f: (f, 0), l),
        ],
        out_specs=pl.BlockSpec((tm, d), lambda i, f: (i, 0)),
        out_shape=jax.ShapeDtypeStruct((m, d), F32),
        scratch_shapes=[pltpu.VMEM((tm, d), BF16), pltpu.VMEM((tm, d), F32)],
        compiler_params=_params(("parallel", "arbitrary"), buffers),
        name="ffn",
    )(x, mod_s, g2, w_in, w_in, w_out)


def _inproj_kernel(x_ref, mod_ref, g_ref, w_ref, p_ref, h_ref):
    @pl.when(pl.program_id(1) == 0)
    def _():
        h = _modulated(x_ref, mod_ref, g_ref).astype(BF16)
        h_ref[...] = h
        p_ref[...] = _dot(h, w_ref[...])

    @pl.when(pl.program_id(1) > 0)
    def _():
        p_ref[...] = _dot(h_ref[...], w_ref[...])


def _inproj(rows, x, mod_s, g2, w_p, l):
    m, d = x.shape
    n = w_p.shape[2]
    tn = P_SLAB
    nt = n // tn
    assert nt * tn == n
    tm = rows.tm
    return pl.pallas_call(
        _inproj_kernel,
        grid=(m // tm, nt),
        in_specs=[
            rows.x_spec(d),
            rows.mod_spec(d),
            pl.BlockSpec((2, d), lambda i, j: (0, 0)),
            _layer_spec((d, tn), lambda i, j: (0, j), l),
        ],
        out_specs=[pl.BlockSpec((None, tm, tn), lambda i, j: (j, i, 0)),
                   pl.BlockSpec((tm, d), lambda i, j: (i, 0))],
        out_shape=[jax.ShapeDtypeStruct((nt, m, tn), F32), jax.ShapeDtypeStruct((m, d), BF16)],
        compiler_params=_params(("parallel", "arbitrary"),
                                tm * d * (2 * 4 + 2 * 2) + 2 * d * tn * 2 + 2 * tm * tn * 4),
        name="inproj",
    )(x, mod_s, g2, w_p)


def _merge_kernel(lat_tiles, h_ref, *refs):
    lat_refs, ctx_refs = refs[0:N_BRANCH], refs[N_BRANCH:2 * N_BRANCH]
    wg_ref, wb_ref, y_ref = refs[2 * N_BRANCH:]
    h = h_ref[...]
    is_lat = pl.program_id(0) < lat_tiles
    y = None
    for i in range(N_BRANCH):
        o = jnp.where(is_lat, lat_refs[i][...], ctx_refs[i][...])
        term = _sigmoid(_dot(h, wg_ref[i])) * _dot(o, wb_ref[i])
        y = term if y is None else y + term
    y_ref[...] = y.astype(BF16)


def _merge(rows, h, lat, ctx, w_gate, w_branch, l):
    m, d = h.shape
    tm = rows.tm
    nt, _, bw, tn = w_branch.shape[1:]
    assert w_gate.shape[1:] == (nt, N_BRANCH, d, tn)
    last_lat = rows.lat_tiles - 1
    lat_spec = pl.BlockSpec((tm, bw), lambda i, j: (jnp.minimum(i, last_lat), 0))
    ctx_spec = pl.BlockSpec((tm, bw), lambda i, j: (jnp.maximum(i - rows.lat_tiles, 0), 0))
    return pl.pallas_call(
        functools.partial(_merge_kernel, rows.lat_tiles),
        grid=(m // tm, nt),
        in_specs=[
            rows.x_spec(d),
            *([lat_spec] * N_BRANCH), *([ctx_spec] * N_BRANCH),
            _layer_spec((None, N_BRANCH, d, tn), lambda i, j: (j, 0, 0, 0), l),
            _layer_spec((None, N_BRANCH, bw, tn), lambda i, j: (j, 0, 0, 0), l),
        ],
        out_specs=pl.BlockSpec((tm, tn), lambda i, j: (i, j)),
        out_shape=jax.ShapeDtypeStruct((m, d), BF16),
        compiler_params=_params(("parallel", "arbitrary"),
                                2 * tm * d * 2 + 2 * 2 * N_BRANCH * tm * bw * 2
                                + 2 * N_BRANCH * (d + bw) * tn * 2 + 2 * tm * tn * 2),
        name="merge",
    )(h, *lat, *ctx, w_gate, w_branch)


def _outproj_kernel(x_ref, mod_ref, g_ref, y_ref, w_ref, o_ref):
    z = _dot(y_ref[...], w_ref[...])
    o_ref[...] = x_ref[...] + mod_ref[2:3, :] * _rms(z, g_ref[1:2, :])


def _outproj(rows, x, mod_s, g2, y, w_out, l):
    m, d = x.shape
    tm = rows.tm
    return pl.pallas_call(
        _outproj_kernel,
        grid=(m // tm,),
        in_specs=[
            rows.x_spec(d),
            rows.mod_spec(d),
            pl.BlockSpec((2, d), lambda i: (0, 0)),
            rows.x_spec(d),
            _layer_spec((d, d), lambda i: (0, 0), l, SINGLE),
        ],
        out_specs=pl.BlockSpec((tm, d), lambda i: (i, 0)),
        out_shape=jax.ShapeDtypeStruct((m, d), F32),
        compiler_params=_params(("parallel",), tm * d * (2 * 4 + 2 * 4 + 4 + 2 * 2) + d * d * 2),
        name="outproj",
    )(x, mod_s, g2, y, w_out)


def _rope_tables(t, d):
    rows = t // GRID_W
    row = jnp.repeat(jnp.arange(rows), GRID_W).astype(F32)
    col = jnp.tile(jnp.arange(GRID_W), rows).astype(F32)
    quarter = d // 4
    inv = jnp.power(ROPE_BASE, -jnp.arange(quarter, dtype=F32) / quarter)
    ar = row[:, None] * inv
    ac = col[:, None] * inv
    ang = jnp.concatenate([ar, ar, ac, ac], axis=-1)
    cos, sin = jnp.cos(ang), jnp.sin(ang)
    first = (jnp.arange(d) % (2 * quarter)) < quarter
    sa = jnp.where(first, -sin, 0.0)
    sb = jnp.where(first, 0.0, sin)
    reps = LANE // d
    return tuple(jnp.tile(a, (1, reps)) for a in (cos, sa, sb))


def _rope(x, cos, sa, sb, quarter):
    n = x.shape[-1]
    ahead = pltpu.roll(x, n - quarter, 1)
    behind = pltpu.roll(x, quarter, 1)
    return x * cos + ahead * sa + behind * sb


def _log_sigmoid(x):
    return jnp.minimum(x, 0.0) - jnp.log1p(jnp.exp(-jnp.abs(x)))


def _split_bf16(x):
    hi = x.astype(BF16)
    return hi, (x - hi.astype(F32)).astype(BF16)


def _mlstm_kernel(t, layer, emit_state, *refs):
    has_state = layer is not None
    it = iter(refs)
    q_ref, k_ref, v_ref, og_ref, gt_ref, bias_ref, gain_ref = (next(it) for _ in range(7))
    if has_state:
        c0_ref, n0_ref, m0_ref = next(it), next(it), next(it)
    o_ref = next(it)
    if emit_state:
        cout_ref, nout_ref, mout_ref = next(it), next(it), next(it)
    h_scr, s_scr = next(it), next(it)

    L, D, H = CHUNK, MIX_DIM, MIX_HEADS
    nc = t // L
    r_i = lax.broadcasted_iota(jnp.int32, (L, L), 0)
    c_i = lax.broadcasted_iota(jnp.int32, (L, L), 1)
    masks = (r_i >= c_i, r_i <= c_i)
    cum = tuple(mk.astype(BF16) for mk in masks)
    ones = jnp.ones((L, D), BF16)
    sel = [(r_i == g).astype(BF16) for g in range(2 * H)]
    bias = bias_ref[...]
    scale = D ** -0.5

    ms0 = []
    for d in range(2):
        for h in range(H):
            if has_state:
                s_scr[d, h, :, 0:D] = c0_ref[d, h]
                s_scr[d, h, :, D:2 * D] = jnp.broadcast_to(n0_ref[d, h:h + 1, :], (D, D)).T
                ms0.append(jnp.full((1, D), m0_ref[pl.program_id(0), layer, d, h], F32))
            else:
                s_scr[d, h] = jnp.zeros((D, 2 * D), F32)
                ms0.append(jnp.zeros((1, D), F32))

    def zero_rows(c, _):
        h_scr[_chunk_rows(c), :] = jnp.zeros((L, H * D), F32)
        return 0

    lax.fori_loop(0, nc, zero_rows, 0)

    def body(c, ms):
        chains = []
        for d, cc in ((0, c), (1, nc - 1 - c)):
            rows = _chunk_rows(cc)
            g = gt_ref[rows, :] + bias
            lf_hi, lf_lo = _split_bf16(_log_sigmoid(g[:, LANE:2 * LANE]))
            f_all = _dot(cum[d], lf_hi) + _dot(cum[d], lf_lo)
            b_all = g[:, 0:LANE] - f_all
            b_t = b_all.T
            f_hi, f_lo = _split_bf16(f_all)
            b_hi, b_lo = _split_bf16(b_all)
            for h in range(H):
                j = d * H + h
                cols = slice(h * D, (h + 1) * D)
                k = k_ref[rows, cols] * scale
                qb = q_ref[rows, cols].astype(BF16)
                s_old = s_scr[d, h]
                chains.append(dict(
                    d=d, h=h, j=j, rows=rows, cols=cols, k=k, s_old=s_old, brow=b_t[j:j + 1, :],
                    v_aug=jnp.concatenate([v_ref[rows, cols].astype(BF16), ones], axis=1),
                    f_rep=_dot(f_hi, sel[j]) + _dot(f_lo, sel[j]),
                    b_rep=_dot(b_hi, sel[j]) + _dot(b_lo, sel[j]),
                    qk=_dot_nt(qb, k.astype(BF16)),
                    qs=_dot(qb, s_old.astype(BF16))))
        for ch in chains:
            d, f_rep, b_rep, m = ch["d"], ch["f_rep"], ch["b_rep"], ms[ch["j"]]
            ch["f_last"] = f_rep[L - 1:L, :] if d == 0 else f_rep[0:1, :]
            lw = jnp.where(masks[d], f_rep + ch["brow"], NEG)
            inter = f_rep + m
            m_tok = jnp.maximum(jnp.max(lw, axis=-1, keepdims=True), inter)
            ch["a"] = (ch["qk"] * jnp.exp(lw - m_tok)).astype(BF16)
            ch["w_inter"] = jnp.exp(inter - m_tok)
            ch["floor"] = jnp.exp(-m_tok)
            ch["bmax"] = jnp.max(b_rep, axis=0, keepdims=True)
            ch["kw"] = (ch["k"] * jnp.exp(b_rep - ch["bmax"])).astype(BF16)
        for ch in chains:
            ch["pv"] = _dot(ch["a"], ch["v_aug"])
            ch["u_aug"] = _dot_tn(ch["kw"], ch["v_aug"])
        out = list(ms)
        for ch in chains:
            d, h, pv, qs, w_inter, m = ch["d"], ch["h"], ch["pv"], ch["qs"], ch["w_inter"], ms[ch["j"]]
            num = pv[:, 0:D] + w_inter * qs[:, 0:D]
            den = pv[:, D:2 * D] + w_inter * qs[:, D:2 * D]
            h_scr[ch["rows"], ch["cols"]] += num * (1.0 / jnp.maximum(jnp.abs(den), ch["floor"]))
            a_end = ch["f_last"] + ch["bmax"]
            m_new = jnp.maximum(ch["f_last"] + m, a_end)
            d_old = jnp.exp(ch["f_last"] + m - m_new)
            d_new = jnp.exp(a_end - m_new)
            for half in (slice(0, D), slice(D, 2 * D)):
                s_scr[d, h, :, half] = d_old * ch["s_old"][:, half] + d_new * ch["u_aug"][:, half]
            out[ch["j"]] = m_new
        return tuple(out)

    ms = lax.fori_loop(0, nc, body, tuple(ms0))

    gain = gain_ref[...]

    def finish(c, _):
        rows = _chunk_rows(c)
        for h in range(H):
            cols = slice(h * D, (h + 1) * D)
            y = _head_norm(h_scr[rows, cols], gain[:, cols])
            o_ref[rows, cols] = (_sigmoid(og_ref[rows, cols]) * y).astype(BF16)
        return 0

    lax.fori_loop(0, nc, finish, 0)

    if emit_state:
        for d in range(2):
            for h in range(H):
                cout_ref[d, h] = s_scr[d, h, :, 0:D]
                nout_ref[d, h:h + 1, :] = s_scr[d, h, :, D:2 * D].T[0:1, :]
                mout_ref[d * H + h:d * H + h + 1, :] = ms[d * H + h]


def _mlstm(p, nb, t, row_off, bias, gain, state):
    assert row_off % t == 0
    ob = row_off // t
    has_state = state is not None
    emit_state = not has_state
    hw = MIX_HEADS * MIX_DIM
    st_c = pl.BlockSpec((None, 2, MIX_HEADS, MIX_DIM, MIX_DIM), lambda b: (b, 0, 0, 0, 0))
    st_n = pl.BlockSpec((None, 2, MIX_HEADS, MIX_DIM), lambda b: (b, 0, 0, 0))

    def col(cb):
        return _p_spec(t, hw, lambda b: (ob + b, cb * LANE // hw), mode)

    mode = SINGLE if t * hw * 4 >= 2 * MIB else None
    nbuf = 1 if mode is SINGLE else 2
    buffers = (nbuf * (4 * t * hw + t * 2 * LANE) * 4 + 2 * t * hw * 2 + t * hw * 4
               + 6 * 2 * MIX_HEADS * MIX_DIM * MIX_DIM * 4)
    in_specs = [col(COL_A_Q), col(COL_A_K), col(COL_A_V), col(COL_A_O),
                _p_spec(t, 2 * LANE, lambda b: (ob + b, COL_GATES // 2), mode),
                pl.BlockSpec((1, 2 * LANE), lambda b: (0, 0)),
                pl.BlockSpec((1, hw), lambda b: (0, 0))]
    args = [p, p, p, p, p, bias, gain]
    layer = None
    if has_state:
        layer = state[3]
        in_specs += [pl.BlockSpec((None, None, 2, MIX_HEADS, MIX_DIM, MIX_DIM), lambda b: (b, layer, 0, 0, 0, 0)),
                     pl.BlockSpec((None, None, 2, MIX_HEADS, MIX_DIM), lambda b: (b, layer, 0, 0, 0)),
                     pl.BlockSpec(memory_space=pltpu.SMEM)]
        args += list(state[0:3])
    out_shape = [jax.ShapeDtypeStruct((nb * t, hw), BF16)]
    out_specs = [pl.BlockSpec((t, hw), lambda b: (b, 0))]
    if emit_state:
        out_shape += [jax.ShapeDtypeStruct((nb, 2, MIX_HEADS, MIX_DIM, MIX_DIM), F32),
                      jax.ShapeDtypeStruct((nb, 2, MIX_HEADS, MIX_DIM), F32),
                      jax.ShapeDtypeStruct((nb, 2 * MIX_HEADS, MIX_DIM), F32)]
        out_specs += [st_c, st_n, pl.BlockSpec((None, 2 * MIX_HEADS, MIX_DIM), lambda b: (b, 0, 0))]
    return pl.pallas_call(
        functools.partial(_mlstm_kernel, t, layer, emit_state),
        grid=(nb,),
        in_specs=in_specs,
        out_specs=out_specs,
        out_shape=out_shape,
        scratch_shapes=[pltpu.VMEM((t, hw), F32),
                        pltpu.VMEM((2, MIX_HEADS, MIX_DIM, 2 * MIX_DIM), F32)],
        compiler_params=_params(("parallel",), buffers),
        name="mlstm_lat" if has_state else "mlstm_ctx",
    )(*args)


def _ret_kernel(t, has_state, emit_state, *refs):
    it = iter(refs)
    q_ref, k_ref, v_ref, dg_ref, lg_ref, gain_ref = (next(it) for _ in range(6))
    if has_state:
        cos_ref, sa_ref, sb_ref, s0_ref = (next(it) for _ in range(4))
    o_ref = next(it)
    if emit_state:
        sout_ref = next(it)
    q_scr, u_scr, o_scr, s_scr = (next(it) for _ in range(4))

    L, D, G = CHUNK, MIX_DIM, RET_GROUP
    nc = t // L
    diff = (lax.broadcasted_iota(jnp.int32, (L, L), 0) - lax.broadcasted_iota(jnp.int32, (L, L), 1)).astype(F32)
    tcol = lax.broadcasted_iota(jnp.int32, (L, 1), 0).astype(F32)
    scale = D ** -0.5
    dm, w_end, w_cross, decay = [], [], [], []
    for j in range(G):
        hx = pl.program_id(1) * G + j
        lgs = (lg_ref[0, hx], lg_ref[1, hx])
        dm.append(jnp.where(diff >= 0.0, jnp.exp(jnp.maximum(diff, 0.0) * lgs[0]), 0.0)
                  + jnp.where(diff <= 0.0, jnp.exp(jnp.maximum(-diff, 0.0) * lgs[1]), 0.0))
        w_end.append((jnp.exp((L - 1.0 - tcol) * lgs[0]), jnp.exp(tcol * lgs[1])))
        w_cross.append((jnp.exp((tcol + 1.0) * lgs[0]), jnp.exp((L - tcol) * lgs[1])))
        decay.append(tuple(jnp.exp(jnp.full((1, 1), float(L), F32) * lgs[d]) for d in range(2)))
        for d in range(2):
            s_scr[d, j] = s0_ref[d, j] if has_state else jnp.zeros((D, D), F32)

    def local(cp, _):
        chains = []
        for c in (2 * cp, 2 * cp + 1):
            rows = _chunk_rows(c)
            if has_state:
                cos, sa, sb = cos_ref[rows, :], sa_ref[rows, :], sb_ref[rows, :]
            for j in range(G):
                cols = slice(j * D, (j + 1) * D)
                q = q_ref[rows, cols]
                k = k_ref[rows, cols] * scale
                if has_state:
                    q = _rope(q, cos, sa, sb, D // 4)
                    k = _rope(k, cos, sa, sb, D // 4)
                qb = q.astype(BF16)
                vb = v_ref[rows, cols].astype(BF16)
                q_scr[rows, cols] = qb
                kw = jnp.concatenate([(k * w_end[j][0]).astype(BF16), (k * w_end[j][1]).astype(BF16)], axis=1)
                u_scr[c, j] = _dot_tn(kw, vb)
                chains.append((rows, cols, j, vb, _dot_nt(qb, k.astype(BF16))))
        weighted = [(qk * dm[j]).astype(BF16) for _, _, j, _, qk in chains]
        for (rows, cols, _, vb, _), a in zip(chains, weighted):
            o_scr[rows, cols] = _dot(a, vb)
        return 0

    assert nc % 2 == 0
    lax.fori_loop(0, nc // 2, local, 0)

    def scan(c, _):
        chains = []
        for d, cc in ((0, c), (1, nc - 1 - c)):
            rows = _chunk_rows(cc)
            for j in range(G):
                cols = slice(j * D, (j + 1) * D)
                s_old = s_scr[d, j]
                chains.append((d, cc, j, rows, cols, s_old, _dot(q_scr[rows, cols], s_old.astype(BF16))))
        for d, cc, j, rows, cols, s_old, cross in chains:
            o_scr[rows, cols] += cross * w_cross[j][d]
            s_scr[d, j] = decay[j][d] * s_old + u_scr[cc, j, d * D:(d + 1) * D, :]
        return 0

    lax.fori_loop(0, nc, scan, 0, unroll=2)

    gain = gain_ref[...]

    def finish(c, _):
        rows = _chunk_rows(c)
        for j in range(G):
            cols = slice(j * D, (j + 1) * D)
            y = _head_norm(o_scr[rows, cols], gain[:, cols])
            dg = dg_ref[rows, cols]
            o_ref[rows, cols] = ((dg * _sigmoid(dg)) * y).astype(BF16)
        return 0

    lax.fori_loop(0, nc, finish, 0, unroll=2)

    if emit_state:
        for d in range(2):
            for j in range(G):
                sout_ref[d, j] = s_scr[d, j]


def _retention(p, nb, t, row_off, log_decay, gain, state):
    assert row_off % t == 0 and MIX_HEADS % RET_GROUP == 0 and COL_D_Q % RET_GROUP == 0
    ob = row_off // t
    has_state = state is not None
    emit_state = not has_state
    G = RET_GROUP
    gw = G * MIX_DIM

    def col(cb):
        return _p_spec(t, gw, lambda b, h: (ob + b, cb // G + h))

    in_specs = [col(COL_D_Q), col(COL_D_K), col(COL_D_V), col(COL_D_G),
                pl.BlockSpec(memory_space=pltpu.SMEM),
                pl.BlockSpec((1, gw), lambda b, h: (0, h))]
    args = [p, p, p, p, log_decay, gain]
    if has_state:
        (cos, sa, sb), s0, layer = state
        tab = pl.BlockSpec((t, LANE), lambda b, h: (0, 0))
        in_specs += [tab, tab, tab,
                     pl.BlockSpec((None, None, 2, G, MIX_DIM, MIX_DIM), lambda b, h: (b, layer, 0, h, 0, 0))]
        args += [cos, sa, sb, s0]
    out_shape = [jax.ShapeDtypeStruct((nb * t, BRANCH_W), BF16)]
    out_specs = [pl.BlockSpec((t, gw), lambda b, h: (b, h))]
    if emit_state:
        out_shape.append(jax.ShapeDtypeStruct((nb, 2, MIX_HEADS, MIX_DIM, MIX_DIM), F32))
        out_specs.append(pl.BlockSpec((None, 2, G, MIX_DIM, MIX_DIM), lambda b, h: (b, 0, h, 0, 0)))
    buffers = (2 * 4 * t * gw * 4 + 2 * 3 * t * LANE * 4 + 2 * t * gw * 2
               + t * gw * (2 + 4) + (t // CHUNK) * G * 2 * MIX_DIM * MIX_DIM * 4 + 6 * G * MIX_DIM * MIX_DIM * 4)
    return pl.pallas_call(
        functools.partial(_ret_kernel, t, has_state, emit_state),
        grid=(nb, MIX_HEADS // G),
        in_specs=in_specs,
        out_specs=out_specs,
        out_shape=out_shape,
        scratch_shapes=[pltpu.VMEM((t, gw), BF16),
                        pltpu.VMEM((t // CHUNK, G, 2 * MIX_DIM, MIX_DIM), F32),
                        pltpu.VMEM((t, gw), F32),
                        pltpu.VMEM((2, G, MIX_DIM, MIX_DIM), F32)],
        compiler_params=_params(("parallel", "parallel"), buffers),
        name="ret_lat" if has_state else "ret_ctx",
    )(*args)


def _softmax_sink_pv(s_parts, v_parts, sink):
    m = sink
    for s in s_parts:
        m = jnp.maximum(m, jnp.max(s, axis=-1, keepdims=True))
    es = [jnp.exp(s - m) for s in s_parts]
    den = jnp.exp(sink - m)
    for e in es:
        den = den + jnp.sum(e, axis=-1, keepdims=True)
    out = None
    for e, v in zip(es, v_parts):
        o = _dot(e.astype(BF16), v)
        out = o if out is None else out + o
    return out * (1.0 / den)


def _attn_ctx_kernel(q_ref, k_ref, v_ref, sink_ref, o_ref, ko_ref, vo_ref):
    k = k_ref[...]
    v = v_ref[...]
    ko_ref[...] = k
    vo_ref[...] = v
    hd = ATT_HEAD_DIM
    scale = hd ** -0.5
    kjs = [k[:, j * hd:(j + 1) * hd].astype(BF16) for j in range(ATT_KV_HEADS)]
    vjs = [v[:, j * hd:(j + 1) * hd].astype(BF16) for j in range(ATT_KV_HEADS)]

    def scores(hx):
        return [_dot_nt(q_ref[:, hx * hd:(hx + 1) * hd].astype(BF16), kjs[hx // ATT_GROUPS]) * scale]

    s_next = scores(0)
    for hx in range(ATT_HEADS):
        s_parts = s_next
        if hx + 1 < ATT_HEADS:
            s_next = scores(hx + 1)
        sink = jnp.full((1, 1), sink_ref[hx], F32)
        o_ref[:, hx * hd:(hx + 1) * hd] = _softmax_sink_pv(s_parts, [vjs[hx // ATT_GROUPS]], sink).astype(BF16)


def _attn_ctx(p, nb, t, row_off, sink):
    assert row_off % t == 0
    ob = row_off // t
    qw = ATT_HEADS * ATT_HEAD_DIM
    kvw = ATT_KV_HEADS * ATT_HEAD_DIM
    return pl.pallas_call(
        _attn_ctx_kernel,
        grid=(nb,),
        in_specs=[
            _p_spec(t, qw, lambda b: (ob + b, COL_B_Q * LANE // qw)),
            _p_spec(t, kvw, lambda b: (ob + b, COL_B_K)),
            _p_spec(t, kvw, lambda b: (ob + b, COL_B_V)),
            pl.BlockSpec(memory_space=pltpu.SMEM),
        ],
        out_specs=[pl.BlockSpec((t, qw), lambda b: (b, 0)),
                   pl.BlockSpec((None, t, kvw), lambda b: (b, 0, 0)),
                   pl.BlockSpec((None, t, kvw), lambda b: (b, 0, 0))],
        out_shape=[jax.ShapeDtypeStruct((nb * t, qw), BF16),
                   jax.ShapeDtypeStruct((nb, t, kvw), F32),
                   jax.ShapeDtypeStruct((nb, t, kvw), F32)],
        compiler_params=_params(("parallel",), 2 * t * (qw * 6 + 4 * kvw * 4)),
        name="attn_ctx",
    )(p, p, p, sink)


def _band_bias():
    Q = QBLOCK
    r = np.arange(Q)[:, None]
    c = np.arange(3 * Q)[None, :]
    band = (c >= r) & (c <= r + 2 * Q)
    out = []
    for kind in range(3):
        ok = band.copy()
        if kind == 0:
            ok &= c >= Q
        if kind == 2:
            ok &= c < 2 * Q
        out.append(np.where(ok, 0.0, NEG).astype(np.float32))
    return jnp.asarray(np.stack(out))


def _half_variants(x):
    low = lax.broadcasted_iota(jnp.int32, x.shape, 1) < ATT_HEAD_DIM
    sw = pltpu.roll(x, ATT_HEAD_DIM, 1)
    zero = jnp.zeros_like(x)
    return [jnp.where(low, a, b).astype(BF16) for a, b in ((x, zero), (zero, sw), (sw, zero), (zero, x))]


def _attn_lat_kernel(t, q_ref, k_ref, v_ref, kc_ref, vc_ref, cq_ref, saq_ref, sbq_ref,
                     ck_ref, sak_ref, sbk_ref, bias_ref, sink_ref, o_ref, kp_scr, vp_scr, kc_scr, vc_scr):
    n = pl.program_id(1)
    Q = QBLOCK
    hd = ATT_HEAD_DIM
    quarter = hd // 4
    nvar = 2 * ATT_KV_HEADS

    @pl.when(n == 0)
    def _():
        zero = jnp.zeros((Q, LANE), BF16)
        for vi in range(nvar):
            kp_scr[vi, 0:Q, :] = zero
            vp_scr[vi, 0:Q, :] = zero
            kp_scr[vi, Q + t:2 * Q + t, :] = zero
            vp_scr[vi, Q + t:2 * Q + t, :] = zero

        def fill(c, _):
            rows = _chunk_rows(c, Q)
            dst = pl.ds(pl.multiple_of(c * Q + Q, Q), Q)
            kk = _rope(k_ref[rows, :], ck_ref[rows, :], sak_ref[rows, :], sbk_ref[rows, :], quarter)
            for vi, (kv, vv) in enumerate(zip(_half_variants(kk), _half_variants(v_ref[rows, :]))):
                kp_scr[vi, dst, :] = kv
                vp_scr[vi, dst, :] = vv
            return 0

        lax.fori_loop(0, t // Q, fill, 0)
        for vi, (kv, vv) in enumerate(zip(_half_variants(kc_ref[...]), _half_variants(vc_ref[...]))):
            kc_scr[vi] = kv
            vc_scr[vi] = vv

    band = pl.ds(pl.multiple_of(n * Q, Q), 3 * Q)
    cq, saq, sbq = cq_ref[...], saq_ref[...], sbq_ref[...]
    bias = bias_ref[...]
    scale = hd ** -0.5
    heads_per_chunk = LANE // hd
    qcs = [(_rope(q_ref[:, c * LANE:(c + 1) * LANE], cq, saq, sbq, quarter) * scale).astype(BF16)
           for c in range(ATT_HEADS // heads_per_chunk)]

    def variant(hx):
        return 2 * (hx // ATT_GROUPS) + hx % heads_per_chunk

    def scores(hx):
        qc, vi = qcs[hx // heads_per_chunk], variant(hx)
        return [_dot_nt(qc, kp_scr[vi, band, :]) + bias, _dot_nt(qc, kc_scr[vi])]

    acc = None
    pending = [scores(hx) for hx in range(ATT_AHEAD)]
    for hx in range(ATT_HEADS):
        s_parts = pending.pop(0)
        if hx + ATT_AHEAD < ATT_HEADS:
            pending.append(scores(hx + ATT_AHEAD))
        vi = variant(hx)
        sink = jnp.full((1, 1), sink_ref[hx], F32)
        o = _softmax_sink_pv(s_parts, [vp_scr[vi, band, :], vc_scr[vi]], sink)
        acc = o if acc is None else acc + o
        if hx % heads_per_chunk == heads_per_chunk - 1:
            c = hx // heads_per_chunk
            o_ref[:, c * LANE:(c + 1) * LANE] = acc.astype(BF16)
            acc = None


def _attn_lat(p, nb, t, kc, vc, layer, tabs, sink):
    assert (ATT_HEAD_DIM ** -0.5) == 2.0 ** round(np.log2(ATT_HEAD_DIM ** -0.5))
    qw = ATT_HEADS * ATT_HEAD_DIM
    kvw = ATT_KV_HEADS * ATT_HEAD_DIM
    assert kvw == LANE
    nq = t // QBLOCK
    past = kc.shape[2]
    nvar = 2 * ATT_KV_HEADS
    cos, sa, sb = tabs
    bias = _band_bias()
    tq = pl.BlockSpec((QBLOCK, LANE), lambda b, n: (n, 0))
    tk = pl.BlockSpec((t, LANE), lambda b, n: (0, 0))
    return pl.pallas_call(
        functools.partial(_attn_lat_kernel, t),
        grid=(nb, nq),
        in_specs=[
            _p_spec(QBLOCK, qw, lambda b, n: (b * nq + n, COL_B_Q * LANE // qw)),
            _p_spec(t, kvw, lambda b, n: (b, COL_B_K)),
            _p_spec(t, kvw, lambda b, n: (b, COL_B_V)),
            pl.BlockSpec((None, None, past, kvw), lambda b, n: (b, layer, 0, 0)),
            pl.BlockSpec((None, None, past, kvw), lambda b, n: (b, layer, 0, 0)),
            tq, tq, tq, tk, tk, tk,
            pl.BlockSpec((None,) + bias.shape[1:],
                         lambda b, n: (jnp.where(n == 0, 0, jnp.where(n == nq - 1, 2, 1)), 0, 0)),
            pl.BlockSpec(memory_space=pltpu.SMEM),
        ],
        out_specs=pl.BlockSpec((QBLOCK, qw), lambda b, n: (b * nq + n, 0)),
        out_shape=jax.ShapeDtypeStruct((nb * t, qw), BF16),
        scratch_shapes=[pltpu.VMEM((nvar, t + 2 * QBLOCK, LANE), BF16), pltpu.VMEM((nvar, t + 2 * QBLOCK, LANE), BF16),
                        pltpu.VMEM((nvar, past, LANE), BF16), pltpu.VMEM((nvar, past, LANE), BF16)],
        compiler_params=_params(("parallel", "arbitrary"),
                                2 * 5 * t * LANE * 4 + 2 * nvar * (t + 2 * QBLOCK + past) * LANE * 2 + 2 * MIB),
        name="attn_lat",
    )(p, p, p, kc, vc, cos, sa, sb, cos, sa, sb, bias, sink)


def _conv_kernel(t, cb_ref, cc_ref, cx_ref, w_ref, o_ref):
    u = cc_ref[...] * cx_ref[...]
    tok = lax.broadcasted_iota(jnp.int32, u.shape, 0)
    prev = jnp.where(tok == 0, 0.0, pltpu.roll(u, 1, 0))
    nxt = jnp.where(tok == t - 1, 0.0, pltpu.roll(u, t - 1, 0))
    o_ref[...] = (cb_ref[...] * (w_ref[0:1, :] * prev + w_ref[1:2, :] * u + w_ref[2:3, :] * nxt)).astype(BF16)


def _conv(p, nb, t, row_off, w):
    assert row_off % t == 0
    ob = row_off // t
    nblk = w.shape[1] // LANE

    def col(cb):
        return _p_spec(t, LANE, lambda b, j: (ob + b, cb + j))

    return pl.pallas_call(
        functools.partial(_conv_kernel, t),
        grid=(nb, nblk),
        in_specs=[col(COL_C_B), col(COL_C_C), col(COL_C_X),
                  pl.BlockSpec((w.shape[0], LANE), lambda b, j: (0, j))],
        out_specs=pl.BlockSpec((t, LANE), lambda b, j: (b, j)),
        out_shape=jax.ShapeDtypeStruct((nb * t, w.shape[1]), BF16),
        compiler_params=_params(("parallel", "parallel"), 2 * 4 * t * LANE * 4),
        name="conv",
    )(p, p, p, w)


GATE_SRC = 4 * MIX_HEADS * MIX_DIM
N_GATES = 4 * MIX_HEADS


def _gate_lanes(x):
    g = x.reshape(x.shape[:-1] + (2, 2, MIX_HEADS))
    pad = [(0, 0)] * (x.ndim - 1) + [(0, LANE - 2 * MIX_HEADS)]
    blocks = [jnp.pad(g[..., kind, :].reshape(x.shape[:-1] + (2 * MIX_HEADS,)), pad) for kind in range(2)]
    return jnp.concatenate(blocks, axis=-1)


def _column_slabs(w, tile):
    k, n = w.shape[-2:]
    return jnp.swapaxes(w.reshape(w.shape[:-1] + (n // tile, tile)), -3, -2)


def _stage_mixer_proj(w_in):
    head = w_in[..., 0:GATE_SRC]
    tail = w_in[..., GATE_SRC + N_GATES:]
    gates = _gate_lanes(w_in[..., GATE_SRC:GATE_SRC + N_GATES])
    assert head.shape[-1] + tail.shape[-1] == COL_GATES * LANE
    assert head.shape[-1] + tail.shape[-1] + gates.shape[-1] == PROJ_P
    return jnp.concatenate([head, tail, gates], axis=-1).astype(BF16)


def kernel(x_prompt, x_sample, cache_attn_k, cache_attn_v, state_mlstm_C, state_mlstm_n, state_mlstm_m,
           state_ret_S, c, c_ctx, w_mod, b_mod, norm_g, ffn1_w_in, ffn1_w_out, ffn2_w_in, ffn2_w_out,
           w_in, mlstm_gate_b, mlstm_norm_g, attn_sink, conv_w, ret_log_decay, ret_norm_g,
           w_branch, w_gate, w_out):
    bp, tp, d = x_prompt.shape
    bs, ts, _ = x_sample.shape
    depth = w_mod.shape[0]
    past = cache_attn_k.shape[2]
    kvw = ATT_KV_HEADS * ATT_HEAD_DIM
    fits = [t for t in (1024, 512, 256) if ts % t == 0 and (bp * tp) % t == 0]
    rows = _Rows(bs, ts, bp, tp, fits[0])
    rows_ffn = _Rows(bs, ts, bp, tp, fits[min(1, len(fits) - 1)])

    nbp = -(-(1 + bs) // 8) * 8
    cpad = jnp.concatenate([c_ctx[None], c, jnp.zeros((nbp - 1 - bs, d), F32)], axis=0)
    mod = _modulation(cpad, w_mod, b_mod).reshape(depth, nbp, 3, 3, d)

    w_p = _stage_mixer_proj(w_in)
    ffn1_in, ffn1_out = ffn1_w_in.astype(BF16), ffn1_w_out.astype(BF16)
    ffn2_in, ffn2_out = ffn2_w_in.astype(BF16), ffn2_w_out.astype(BF16)
    tn = _pick_tile(d, 256)
    wg_b = jnp.swapaxes(_column_slabs(w_gate.astype(BF16), tn), 1, 2)
    wb_b = jnp.swapaxes(_column_slabs(w_branch.astype(BF16), tn), 1, 2)
    wo_b = w_out.astype(BF16)

    gate_bias = _gate_lanes(mlstm_gate_b.reshape(depth, 1, N_GATES))
    tabs_att = _rope_tables(ts, ATT_HEAD_DIM)
    tabs_ret = _rope_tables(ts, MIX_DIM)

    x = jnp.concatenate([x_sample.reshape(bs * ts, d), x_prompt.reshape(bp * tp, d)], axis=0)
    ms = bs * ts
    kc_all = cache_attn_k.reshape(bs, depth, past, kvw)
    vc_all = cache_attn_v.reshape(bs, depth, past, kvw)
    ks_, vs_, cs_, ns_, mms_, ss_ = [], [], [], [], [], []
    for l in range(depth):
        g = norm_g[l]
        x = _ffn(rows_ffn, x, mod[l, :, 0], g[0:2], ffn1_in, ffn1_out, l)

        p, h = _inproj(rows, x, mod[l, :, 1], g[2:4], w_p, l)
        a_gain = mlstm_norm_g[l][None]
        d_gain = ret_norm_g[l][None]
        lat_a = _mlstm(p, bs, ts, 0, gate_bias[l], a_gain, (state_mlstm_C, state_mlstm_n, state_mlstm_m, l))[0]
        lat_b = _attn_lat(p, bs, ts, kc_all, vc_all, l, tabs_att, attn_sink[l])
        lat_c = _conv(p, bs, ts, 0, conv_w[l])
        lat_d = _retention(p, bs, ts, 0, ret_log_decay[l], d_gain, (tabs_ret, state_ret_S, l))[0]
        ctx_a, c_fin, n_fin, m_fin = _mlstm(p, bp, tp, ms, gate_bias[l], a_gain, None)
        ctx_b, k_new, v_new = _attn_ctx(p, bp, tp, ms, attn_sink[l])
        ctx_c = _conv(p, bp, tp, ms, conv_w[l])
        ctx_d, s_fin = _retention(p, bp, tp, ms, ret_log_decay[l], d_gain, None)
        y = _merge(rows, h, (lat_a, lat_b, lat_c, lat_d), (ctx_a, ctx_b, ctx_c, ctx_d), wg_b, wb_b, l)
        x = _outproj(rows_ffn, x, mod[l, :, 1], g[2:4], y, wo_b, l)

        x = _ffn(rows_ffn, x, mod[l, :, 2], g[4:6], ffn2_in, ffn2_out, l)

        ks_.append(k_new.reshape(bp, tp, ATT_KV_HEADS, ATT_HEAD_DIM))
        vs_.append(v_new.reshape(bp, tp, ATT_KV_HEADS, ATT_HEAD_DIM))
        cs_.append(c_fin)
        ns_.append(n_fin)
        mms_.append(m_fin[:, :, 0].reshape(bp, 2, MIX_HEADS))
        ss_.append(s_fin)

    y_sample = x[:ms].reshape(bs, ts, d)
    y_prompt = x[ms:].reshape(bp, tp, d)
    return (y_prompt, y_sample, jnp.stack(ks_, axis=1), jnp.stack(vs_, axis=1), jnp.stack(cs_, axis=1),
            jnp.stack(ns_, axis=1), jnp.stack(mms_, axis=1), jnp.stack(ss_, axis=1))
```

```python
import functools

import numpy as np
import jax
import jax.numpy as jnp
from jax import lax
from jax.experimental import pallas as pl
from jax.experimental.pallas import tpu as pltpu

F32 = jnp.float32
BF16 = jnp.bfloat16

N_MOD = 9
N_BRANCH = 4
BRANCH_W = 512
MIX_HEADS = 4
MIX_DIM = 128
RET_GROUP = 2
ATT_HEADS = 8
ATT_KV_HEADS = 2
ATT_HEAD_DIM = 64
ATT_GROUPS = ATT_HEADS // ATT_KV_HEADS
QBLOCK = 128
ATT_AHEAD = 4
CHUNK = 128
GRID_W = 64
ROPE_BASE = 10000.0
EPS = 1e-6
NEG = -1e30

LANE = 128
COL_A_Q, COL_A_K, COL_A_V, COL_A_O = 0, 4, 8, 12
COL_B_Q, COL_B_K, COL_B_V = 16, 20, 21
COL_C_B, COL_C_C, COL_C_X = 22, 26, 30
COL_D_Q, COL_D_K, COL_D_V, COL_D_G = 34, 38, 42, 46
COL_GATES = 50
PROJ_BLOCKS = 52
PROJ_P = PROJ_BLOCKS * LANE

MIB = 1024 * 1024
VMEM_CAP = 44 * MIB
VMEM_TEMPS = 12 * MIB
SINGLE = pl.Buffered(1)


def _params(sem, buffers):
    return pltpu.CompilerParams(dimension_semantics=sem,
                                vmem_limit_bytes=int(min(buffers + VMEM_TEMPS, VMEM_CAP)))


def _pick_tile(n, pref):
    if n <= pref:
        return n
    t = (pref // LANE) * LANE
    while t >= LANE:
        if n % t == 0:
            return t
        t -= LANE
    raise ValueError(f"no lane-aligned tile divides {n}")


def _dot(a, b):
    return jnp.dot(a, b, preferred_element_type=F32)


def _dot_nt(a, b):
    return lax.dot_general(a, b, (((1,), (1,)), ((), ())), preferred_element_type=F32)


def _dot_tn(a, b):
    return lax.dot_general(a, b, (((0,), (0,)), ((), ())), preferred_element_type=F32)


def _sigmoid(x):
    return 1.0 / (1.0 + jnp.exp(-x))


def _rms(xf, g):
    ms = jnp.mean(xf * xf, axis=-1, keepdims=True)
    return xf * lax.rsqrt(ms + EPS) * g


def _modulated(x_ref, mod_ref, g_ref):
    return _rms(x_ref[...], g_ref[0:1, :]) * (1.0 + mod_ref[1:2, :]) + mod_ref[0:1, :]


def _head_norm(x, g):
    mu = jnp.mean(x, axis=-1, keepdims=True)
    xc = x - mu
    var = jnp.mean(xc * xc, axis=-1, keepdims=True)
    return xc * lax.rsqrt(var + EPS) * g


def _chunk_rows(c, n=CHUNK):
    return pl.ds(pl.multiple_of(c * n, n), n)


def _mod_kernel(c_ref, w_ref, b_ref, o_ref):
    c = c_ref[...]
    s = (c * _sigmoid(c)).astype(BF16)
    o_ref[...] = _dot(s, w_ref[...].astype(BF16)) + b_ref[...]


def _modulation(cpad, w_mod, b_mod):
    depth, d, n = w_mod.shape
    nbp = cpad.shape[0]
    tn = _pick_tile(n, 1024)
    return pl.pallas_call(
        _mod_kernel,
        grid=(depth, n // tn),
        in_specs=[
            pl.BlockSpec((nbp, d), lambda l, j: (0, 0)),
            pl.BlockSpec((None, d, tn), lambda l, j: (l, 0, j)),
            pl.BlockSpec((None, 1, tn), lambda l, j: (l, 0, j)),
        ],
        out_specs=pl.BlockSpec((None, nbp, tn), lambda l, j: (l, 0, j)),
        out_shape=jax.ShapeDtypeStruct((depth, nbp, n), F32),
        compiler_params=_params(("parallel", "parallel"), 2 * d * tn * 4 + MIB),
        name="modulation",
    )(cpad, w_mod, b_mod.reshape(depth, 1, n))


class _Rows:
    def __init__(self, bs, ts, bp, tp, tm):
        self.bs, self.ts, self.bp, self.tp = bs, ts, bp, tp
        self.ms, self.mp = bs * ts, bp * tp
        self.m = self.ms + self.mp
        assert ts % tm == 0 and self.mp % tm == 0 and self.ms % tp == 0
        self.tm = tm
        self.lat_tiles = self.ms // tm
        self.tiles_per_seq = ts // tm

    def mod_index(self, i):
        return jnp.where(i < self.lat_tiles, 1 + i // self.tiles_per_seq, 0)

    def x_spec(self, d, mode=None):
        return pl.BlockSpec((self.tm, d), lambda i, *_: (i, 0), pipeline_mode=mode)

    def mod_spec(self, d):
        return pl.BlockSpec((None, 3, d), lambda i, *_: (self.mod_index(i), 0, 0))


P_SLAB = 512


def _p_spec(rows, width, row_col, mode=None):
    per = P_SLAB // width
    assert per * width == P_SLAB

    def index_map(*ids):
        r, c = row_col(*ids)
        return (c // per, r, c % per)

    return pl.BlockSpec((None, rows, width), index_map, pipeline_mode=mode)


def _layer_spec(block, index_map, l, mode=None):
    return pl.BlockSpec((None,) + tuple(block), lambda *ids: (l,) + tuple(index_map(*ids)), pipeline_mode=mode)


def _ffn_kernel(nf, x_ref, mod_ref, g_ref, wg_ref, wu_ref, wo_ref, o_ref, h_scr, acc_scr):
    f = pl.program_id(1)

    def hidden_slab(h):
        tf = wg_ref.shape[1]
        halves = [slice(0, tf // 2), slice(tf // 2, tf)] if tf % (2 * LANE) == 0 else [slice(0, tf)]
        gate_up = [(_dot(h, wg_ref[:, sl]), _dot(h, wu_ref[:, sl])) for sl in halves]
        acts = [((g * _sigmoid(g)) * u).astype(BF16) for g, u in gate_up]
        out = None
        for a, sl in zip(acts, halves):
            o = _dot(a, wo_ref[sl, :])
            out = o if out is None else out + o
        return out

    def pre():
        h = _modulated(x_ref, mod_ref, g_ref).astype(BF16)
        h_scr[...] = h
        return h

    def post(z):
        o_ref[...] = x_ref[...] + (0.5 * mod_ref[2:3, :]) * _rms(z, g_ref[1:2, :])

    if nf == 1:
        post(hidden_slab(pre()))
        return

    @pl.when(f == 0)
    def _():
        acc_scr[...] = hidden_slab(pre())

    @pl.when((f > 0) & (f < nf - 1))
    def _():
        acc_scr[...] += hidden_slab(h_scr[...])

    @pl.when(f == nf - 1)
    def _():
        post(acc_scr[...] + hidden_slab(h_scr[...]))


def _ffn(rows, x, mod_s, g2, w_in, w_out, l):
    m, d = x.shape
    dff = w_out.shape[1]
    tm = rows.tm
    tf = _pick_tile(dff, 512)
    nf = dff // tf
    buffers = tm * d * (2 * 4 + 2 * 4 + 4 + 2) + 3 * 2 * d * tf * 2
    return pl.pallas_call(
        functools.partial(_ffn_kernel, nf),
        grid=(m // tm, nf),
        in_specs=[
            rows.x_spec(d),
            rows.mod_spec(d),
            pl.BlockSpec((2, d), lambda i, f: (0, 0)),
            _layer_spec((d, tf), lambda i, f: (0, f), l),
            _layer_spec((d, tf), lambda i, f: (0, f + nf), l),
            _layer_spec((tf, d), lambda i, f: (f, 0), l),
        ],
        out_specs=pl.BlockSpec((tm, d), lambda i, f: (i, 0)),
        out_shape=jax.ShapeDtypeStruct((m, d), F32),
        scratch_shapes=[pltpu.VMEM((tm, d), BF16), pltpu.VMEM((tm, d), F32)],
        compiler_params=_params(("parallel", "arbitrary"), buffers),
        name="ffn",
    )(x, mod_s, g2, w_in, w_in, w_out)


def _inproj_kernel(x_ref, mod_ref, g_ref, w_ref, p_ref, h_ref, h_scr):
    @pl.when(pl.program_id(1) == 0)
    def _():
        h = _modulated(x_ref, mod_ref, g_ref).astype(BF16)
        h_ref[...] = h
        h_scr[...] = h
        p_ref[...] = _dot(h, w_ref[...])

    @pl.when(pl.program_id(1) > 0)
    def _():
        p_ref[...] = _dot(h_scr[...], w_ref[...])


def _inproj(rows, x, mod_s, g2, w_p, l):
    m, d = x.shape
    n = w_p.shape[2]
    tn = P_SLAB
    nt = n // tn
    assert nt * tn == n
    tm = rows.tm
    return pl.pallas_call(
        _inproj_kernel,
        grid=(m // tm, nt),
        in_specs=[
            rows.x_spec(d),
            rows.mod_spec(d),
            pl.BlockSpec((2, d), lambda i, j: (0, 0)),
            _layer_spec((d, tn), lambda i, j: (0, j), l),
        ],
        out_specs=[pl.BlockSpec((None, tm, tn), lambda i, j: (j, i, 0)),
                   pl.BlockSpec((tm, d), lambda i, j: (i, 0))],
        out_shape=[jax.ShapeDtypeStruct((nt, m, tn), F32), jax.ShapeDtypeStruct((m, d), BF16)],
        scratch_shapes=[pltpu.VMEM((tm, d), BF16)],
        compiler_params=_params(("parallel", "arbitrary"),
                                tm * d * (2 * 4 + 3 * 2) + 2 * d * tn * 2 + 2 * tm * tn * 4),
        name="inproj",
    )(x, mod_s, g2, w_p)


def _merge_kernel(lat_tiles, h_ref, *refs):
    lat_refs, ctx_refs = refs[0:N_BRANCH], refs[N_BRANCH:2 * N_BRANCH]
    wg_ref, wb_ref, y_ref = refs[2 * N_BRANCH:]
    h = h_ref[...]
    is_lat = pl.program_id(0) < lat_tiles
    y = None
    for i in range(N_BRANCH):
        o = jnp.where(is_lat, lat_refs[i][...], ctx_refs[i][...])
        term = _sigmoid(_dot(h, wg_ref[i])) * _dot(o, wb_ref[i])
        y = term if y is None else y + term
    y_ref[...] = y.astype(BF16)


def _merge(rows, h, lat, ctx, w_gate, w_branch, l):
    m, d = h.shape
    tm = rows.tm
    nt, _, bw, tn = w_branch.shape[1:]
    assert w_gate.shape[1:] == (nt, N_BRANCH, d, tn)
    last_lat = rows.lat_tiles - 1
    lat_spec = pl.BlockSpec((tm, bw), lambda i, j: (jnp.minimum(i, last_lat), 0))
    ctx_spec = pl.BlockSpec((tm, bw), lambda i, j: (jnp.maximum(i - rows.lat_tiles, 0), 0))
    return pl.pallas_call(
        functools.partial(_merge_kernel, rows.lat_tiles),
        grid=(m // tm, nt),
        in_specs=[
            rows.x_spec(d),
            *([lat_spec] * N_BRANCH), *([ctx_spec] * N_BRANCH),
            _layer_spec((None, N_BRANCH, d, tn), lambda i, j: (j, 0, 0, 0), l),
            _layer_spec((None, N_BRANCH, bw, tn), lambda i, j: (j, 0, 0, 0), l),
        ],
        out_specs=pl.BlockSpec((tm, tn), lambda i, j: (i, j)),
        out_shape=jax.ShapeDtypeStruct((m, d), BF16),
        compiler_params=_params(("parallel", "arbitrary"),
                                2 * tm * d * 2 + 2 * 2 * N_BRANCH * tm * bw * 2
                                + 2 * N_BRANCH * (d + bw) * tn * 2 + 2 * tm * tn * 2),
        name="merge",
    )(h, *lat, *ctx, w_gate, w_branch)


def _outproj_kernel(x_ref, mod_ref, g_ref, y_ref, w_ref, o_ref):
    z = _dot(y_ref[...], w_ref[...])
    o_ref[...] = x_ref[...] + mod_ref[2:3, :] * _rms(z, g_ref[1:2, :])


def _outproj(rows, x, mod_s, g2, y, w_out, l):
    m, d = x.shape
    tm = rows.tm
    return pl.pallas_call(
        _outproj_kernel,
        grid=(m // tm,),
        in_specs=[
            rows.x_spec(d),
            rows.mod_spec(d),
            pl.BlockSpec((2, d), lambda i: (0, 0)),
            rows.x_spec(d),
            _layer_spec((d, d), lambda i: (0, 0), l, SINGLE),
        ],
        out_specs=pl.BlockSpec((tm, d), lambda i: (i, 0)),
        out_shape=jax.ShapeDtypeStruct((m, d), F32),
        compiler_params=_params(("parallel",), tm * d * (2 * 4 + 2 * 4 + 4 + 2 * 2) + d * d * 2),
        name="outproj",
    )(x, mod_s, g2, y, w_out)


def _rope_tables(t, d):
    rows = t // GRID_W
    row = jnp.repeat(jnp.arange(rows), GRID_W).astype(F32)
    col = jnp.tile(jnp.arange(GRID_W), rows).astype(F32)
    quarter = d // 4
    inv = jnp.power(ROPE_BASE, -jnp.arange(quarter, dtype=F32) / quarter)
    ar = row[:, None] * inv
    ac = col[:, None] * inv
    ang = jnp.concatenate([ar, ar, ac, ac], axis=-1)
    cos, sin = jnp.cos(ang), jnp.sin(ang)
    first = (jnp.arange(d) % (2 * quarter)) < quarter
    sa = jnp.where(first, -sin, 0.0)
    sb = jnp.where(first, 0.0, sin)
    reps = LANE // d
    return tuple(jnp.tile(a, (1, reps)) for a in (cos, sa, sb))


def _rope(x, cos, sa, sb, quarter):
    n = x.shape[-1]
    ahead = pltpu.roll(x, n - quarter, 1)
    behind = pltpu.roll(x, quarter, 1)
    return x * cos + ahead * sa + behind * sb


def _log_sigmoid(x):
    return jnp.minimum(x, 0.0) - jnp.log1p(jnp.exp(-jnp.abs(x)))


def _split_bf16(x):
    hi = x.astype(BF16)
    return hi, (x - hi.astype(F32)).astype(BF16)


def _mlstm_kernel(t, layer, emit_state, *refs):
    has_state = layer is not None
    it = iter(refs)
    q_ref, k_ref, v_ref, og_ref, gt_ref, bias_ref, gain_ref = (next(it) for _ in range(7))
    if has_state:
        c0_ref, n0_ref, m0_ref = next(it), next(it), next(it)
    o_ref = next(it)
    if emit_state:
        cout_ref, nout_ref, mout_ref = next(it), next(it), next(it)
    h_scr, s_scr = next(it), next(it)

    L, D, H = CHUNK, MIX_DIM, MIX_HEADS
    nc = t // L
    r_i = lax.broadcasted_iota(jnp.int32, (L, L), 0)
    c_i = lax.broadcasted_iota(jnp.int32, (L, L), 1)
    masks = (r_i >= c_i, r_i <= c_i)
    cum = tuple(mk.astype(BF16) for mk in masks)
    ones = jnp.ones((L, D), BF16)
    sel = [(r_i == g).astype(BF16) for g in range(2 * H)]
    bias = bias_ref[...]
    scale = D ** -0.5

    ms0 = []
    for d in range(2):
        for h in range(H):
            if has_state:
                s_scr[d, h, :, 0:D] = c0_ref[d, h]
                s_scr[d, h, :, D:2 * D] = jnp.broadcast_to(n0_ref[d, h:h + 1, :], (D, D)).T
                ms0.append(jnp.full((1, D), m0_ref[pl.program_id(0), layer, d, h], F32))
            else:
                s_scr[d, h] = jnp.zeros((D, 2 * D), F32)
                ms0.append(jnp.zeros((1, D), F32))

    def zero_rows(c, _):
        h_scr[_chunk_rows(c), :] = jnp.zeros((L, H * D), F32)
        return 0

    lax.fori_loop(0, nc, zero_rows, 0)

    def body(c, ms):
        chains = []
        for d, cc in ((0, c), (1, nc - 1 - c)):
            rows = _chunk_rows(cc)
            g = gt_ref[rows, :] + bias
            lf_hi, lf_lo = _split_bf16(_log_sigmoid(g[:, LANE:2 * LANE]))
            f_all = _dot(cum[d], lf_hi) + _dot(cum[d], lf_lo)
            b_all = g[:, 0:LANE] - f_all
            b_t = b_all.T
            f_hi, f_lo = _split_bf16(f_all)
            b_hi, b_lo = _split_bf16(b_all)
            for h in range(H):
                j = d * H + h
                cols = slice(h * D, (h + 1) * D)
                k = k_ref[rows, cols] * scale
                qb = q_ref[rows, cols].astype(BF16)
                s_old = s_scr[d, h]
                chains.append(dict(
                    d=d, h=h, j=j, rows=rows, cols=cols, k=k, s_old=s_old, brow=b_t[j:j + 1, :],
                    v_aug=jnp.concatenate([v_ref[rows, cols].astype(BF16), ones], axis=1),
                    f_rep=_dot(f_hi, sel[j]) + _dot(f_lo, sel[j]),
                    b_rep=_dot(b_hi, sel[j]) + _dot(b_lo, sel[j]),
                    qk=_dot_nt(qb, k.astype(BF16)),
                    qs=_dot(qb, s_old.astype(BF16))))
        for ch in chains:
            d, f_rep, b_rep, m = ch["d"], ch["f_rep"], ch["b_rep"], ms[ch["j"]]
            ch["f_last"] = f_rep[L - 1:L, :] if d == 0 else f_rep[0:1, :]
            lw = jnp.where(masks[d], f_rep + ch["brow"], NEG)
            inter = f_rep + m
            m_tok = jnp.maximum(jnp.max(lw, axis=-1, keepdims=True), inter)
            ch["a"] = (ch["qk"] * jnp.exp(lw - m_tok)).astype(BF16)
            ch["w_inter"] = jnp.exp(inter - m_tok)
            ch["floor"] = jnp.exp(-m_tok)
            ch["bmax"] = jnp.max(b_rep, axis=0, keepdims=True)
            ch["kw"] = (ch["k"] * jnp.exp(b_rep - ch["bmax"])).astype(BF16)
        for ch in chains:
            ch["pv"] = _dot(ch["a"], ch["v_aug"])
            ch["u_aug"] = _dot_tn(ch["kw"], ch["v_aug"])
        out = list(ms)
        for ch in chains:
            d, h, pv, qs, w_inter, m = ch["d"], ch["h"], ch["pv"], ch["qs"], ch["w_inter"], ms[ch["j"]]
            num = pv[:, 0:D] + w_inter * qs[:, 0:D]
            den = pv[:, D:2 * D] + w_inter * qs[:, D:2 * D]
            h_scr[ch["rows"], ch["cols"]] += num * (1.0 / jnp.maximum(jnp.abs(den), ch["floor"]))
            a_end = ch["f_last"] + ch["bmax"]
            m_new = jnp.maximum(ch["f_last"] + m, a_end)
            d_old = jnp.exp(ch["f_last"] + m - m_new)
            d_new = jnp.exp(a_end - m_new)
            for half in (slice(0, D), slice(D, 2 * D)):
                s_scr[d, h, :, half] = d_old * ch["s_old"][:, half] + d_new * ch["u_aug"][:, half]
            out[ch["j"]] = m_new
        return tuple(out)

    ms = lax.fori_loop(0, nc, body, tuple(ms0))

    gain = gain_ref[...]

    def finish(c, _):
        rows = _chunk_rows(c)
        for h in range(H):
            cols = slice(h * D, (h + 1) * D)
            y = _head_norm(h_scr[rows, cols], gain[:, cols])
            o_ref[rows, cols] = (_sigmoid(og_ref[rows, cols]) * y).astype(BF16)
        return 0

    lax.fori_loop(0, nc, finish, 0)

    if emit_state:
        for d in range(2):
            for h in range(H):
                cout_ref[d, h] = s_scr[d, h, :, 0:D]
                nout_ref[d, h:h + 1, :] = s_scr[d, h, :, D:2 * D].T[0:1, :]
                mout_ref[d * H + h:d * H + h + 1, :] = ms[d * H + h]


def _mlstm(p, nb, t, row_off, bias, gain, state):
    assert row_off % t == 0
    ob = row_off // t
    has_state = state is not None
    emit_state = not has_state
    hw = MIX_HEADS * MIX_DIM
    st_c = pl.BlockSpec((None, 2, MIX_HEADS, MIX_DIM, MIX_DIM), lambda b: (b, 0, 0, 0, 0))
    st_n = pl.BlockSpec((None, 2, MIX_HEADS, MIX_DIM), lambda b: (b, 0, 0, 0))

    def col(cb):
        return _p_spec(t, hw, lambda b: (ob + b, cb * LANE // hw), mode)

    mode = SINGLE if t * hw * 4 >= 2 * MIB else None
    nbuf = 1 if mode is SINGLE else 2
    buffers = (nbuf * (4 * t * hw + t * 2 * LANE) * 4 + 2 * t * hw * 2 + t * hw * 4
               + 6 * 2 * MIX_HEADS * MIX_DIM * MIX_DIM * 4)
    in_specs = [col(COL_A_Q), col(COL_A_K), col(COL_A_V), col(COL_A_O),
                _p_spec(t, 2 * LANE, lambda b: (ob + b, COL_GATES // 2), mode),
                pl.BlockSpec((1, 2 * LANE), lambda b: (0, 0)),
                pl.BlockSpec((1, hw), lambda b: (0, 0))]
    args = [p, p, p, p, p, bias, gain]
    layer = None
    if has_state:
        layer = state[3]
        in_specs += [pl.BlockSpec((None, None, 2, MIX_HEADS, MIX_DIM, MIX_DIM), lambda b: (b, layer, 0, 0, 0, 0)),
                     pl.BlockSpec((None, None, 2, MIX_HEADS, MIX_DIM), lambda b: (b, layer, 0, 0, 0)),
                     pl.BlockSpec(memory_space=pltpu.SMEM)]
        args += list(state[0:3])
    out_shape = [jax.ShapeDtypeStruct((nb * t, hw), BF16)]
    out_specs = [pl.BlockSpec((t, hw), lambda b: (b, 0))]
    if emit_state:
        out_shape += [jax.ShapeDtypeStruct((nb, 2, MIX_HEADS, MIX_DIM, MIX_DIM), F32),
                      jax.ShapeDtypeStruct((nb, 2, MIX_HEADS, MIX_DIM), F32),
                      jax.ShapeDtypeStruct((nb, 2 * MIX_HEADS, MIX_DIM), F32)]
        out_specs += [st_c, st_n, pl.BlockSpec((None, 2 * MIX_HEADS, MIX_DIM), lambda b: (b, 0, 0))]
    return pl.pallas_call(
        functools.partial(_mlstm_kernel, t, layer, emit_state),
        grid=(nb,),
        in_specs=in_specs,
        out_specs=out_specs,
        out_shape=out_shape,
        scratch_shapes=[pltpu.VMEM((t, hw), F32),
                        pltpu.VMEM((2, MIX_HEADS, MIX_DIM, 2 * MIX_DIM), F32)],
        compiler_params=_params(("parallel",), buffers),
        name="mlstm_lat" if has_state else "mlstm_ctx",
    )(*args)


def _ret_kernel(t, has_state, emit_state, *refs):
    it = iter(refs)
    q_ref, k_ref, v_ref, dg_ref, lg_ref, gain_ref = (next(it) for _ in range(6))
    if has_state:
        cos_ref, sa_ref, sb_ref, s0_ref = (next(it) for _ in range(4))
    o_ref = next(it)
    if emit_state:
        sout_ref = next(it)
    q_scr, u_scr, o_scr, s_scr = (next(it) for _ in range(4))

    L, D, G = CHUNK, MIX_DIM, RET_GROUP
    nc = t // L
    diff = (lax.broadcasted_iota(jnp.int32, (L, L), 0) - lax.broadcasted_iota(jnp.int32, (L, L), 1)).astype(F32)
    tcol = lax.broadcasted_iota(jnp.int32, (L, 1), 0).astype(F32)
    scale = D ** -0.5
    dm, w_end, w_cross, decay = [], [], [], []
    for j in range(G):
        hx = pl.program_id(1) * G + j
        lgs = (lg_ref[0, hx], lg_ref[1, hx])
        dm.append(jnp.where(diff >= 0.0, jnp.exp(jnp.maximum(diff, 0.0) * lgs[0]), 0.0)
                  + jnp.where(diff <= 0.0, jnp.exp(jnp.maximum(-diff, 0.0) * lgs[1]), 0.0))
        w_end.append((jnp.exp((L - 1.0 - tcol) * lgs[0]), jnp.exp(tcol * lgs[1])))
        w_cross.append((jnp.exp((tcol + 1.0) * lgs[0]), jnp.exp((L - tcol) * lgs[1])))
        decay.append(tuple(jnp.exp(jnp.full((1, 1), float(L), F32) * lgs[d]) for d in range(2)))
        for d in range(2):
            s_scr[d, j] = s0_ref[d, j] if has_state else jnp.zeros((D, D), F32)

    def local(cp, _):
        chains = []
        for c in (2 * cp, 2 * cp + 1):
            rows = _chunk_rows(c)
            if has_state:
                cos, sa, sb = cos_ref[rows, :], sa_ref[rows, :], sb_ref[rows, :]
            for j in range(G):
                cols = slice(j * D, (j + 1) * D)
                q = q_ref[rows, cols]
                k = k_ref[rows, cols] * scale
                if has_state:
                    q = _rope(q, cos, sa, sb, D // 4)
                    k = _rope(k, cos, sa, sb, D // 4)
                qb = q.astype(BF16)
                vb = v_ref[rows, cols].astype(BF16)
                q_scr[rows, cols] = qb
                kw = jnp.concatenate([(k * w_end[j][0]).astype(BF16), (k * w_end[j][1]).astype(BF16)], axis=1)
                u_scr[c, j] = _dot_tn(kw, vb)
                chains.append((rows, cols, j, vb, _dot_nt(qb, k.astype(BF16))))
        weighted = [(qk * dm[j]).astype(BF16) for _, _, j, _, qk in chains]
        for (rows, cols, _, vb, _), a in zip(chains, weighted):
            o_scr[rows, cols] = _dot(a, vb)
        return 0

    assert nc % 2 == 0
    lax.fori_loop(0, nc // 2, local, 0)

    def scan(c, _):
        chains = []
        for d, cc in ((0, c), (1, nc - 1 - c)):
            rows = _chunk_rows(cc)
            for j in range(G):
                cols = slice(j * D, (j + 1) * D)
                s_old = s_scr[d, j]
                chains.append((d, cc, j, rows, cols, s_old, _dot(q_scr[rows, cols], s_old.astype(BF16))))
        for d, cc, j, rows, cols, s_old, cross in chains:
            o_scr[rows, cols] += cross * w_cross[j][d]
            s_scr[d, j] = decay[j][d] * s_old + u_scr[cc, j, d * D:(d + 1) * D, :]
        return 0

    lax.fori_loop(0, nc, scan, 0, unroll=2)

    gain = gain_ref[...]

    def finish(c, _):
        rows = _chunk_rows(c)
        for j in range(G):
            cols = slice(j * D, (j + 1) * D)
            y = _head_norm(o_scr[rows, cols], gain[:, cols])
            dg = dg_ref[rows, cols]
            o_ref[rows, cols] = ((dg * _sigmoid(dg)) * y).astype(BF16)
        return 0

    lax.fori_loop(0, nc, finish, 0, unroll=2)

    if emit_state:
        for d in range(2):
            for j in range(G):
                sout_ref[d, j] = s_scr[d, j]


def _retention(p, nb, t, row_off, log_decay, gain, state):
    assert row_off % t == 0 and MIX_HEADS % RET_GROUP == 0 and COL_D_Q % RET_GROUP == 0
    ob = row_off // t
    has_state = state is not None
    emit_state = not has_state
    G = RET_GROUP
    gw = G * MIX_DIM

    def col(cb):
        return _p_spec(t, gw, lambda b, h: (ob + b, cb // G + h))

    in_specs = [col(COL_D_Q), col(COL_D_K), col(COL_D_V), col(COL_D_G),
                pl.BlockSpec(memory_space=pltpu.SMEM),
                pl.BlockSpec((1, gw), lambda b, h: (0, h))]
    args = [p, p, p, p, log_decay, gain]
    if has_state:
        (cos, sa, sb), s0, layer = state
        tab = pl.BlockSpec((t, LANE), lambda b, h: (0, 0))
        in_specs += [tab, tab, tab,
                     pl.BlockSpec((None, None, 2, G, MIX_DIM, MIX_DIM), lambda b, h: (b, layer, 0, h, 0, 0))]
        args += [cos, sa, sb, s0]
    out_shape = [jax.ShapeDtypeStruct((nb * t, BRANCH_W), BF16)]
    out_specs = [pl.BlockSpec((t, gw), lambda b, h: (b, h))]
    if emit_state:
        out_shape.append(jax.ShapeDtypeStruct((nb, 2, MIX_HEADS, MIX_DIM, MIX_DIM), F32))
        out_specs.append(pl.BlockSpec((None, 2, G, MIX_DIM, MIX_DIM), lambda b, h: (b, 0, h, 0, 0)))
    buffers = (2 * 4 * t * gw * 4 + 2 * 3 * t * LANE * 4 + 2 * t * gw * 2
               + t * gw * (2 + 4) + (t // CHUNK) * G * 2 * MIX_DIM * MIX_DIM * 4 + 6 * G * MIX_DIM * MIX_DIM * 4)
    return pl.pallas_call(
        functools.partial(_ret_kernel, t, has_state, emit_state),
        grid=(nb, MIX_HEADS // G),
        in_specs=in_specs,
        out_specs=out_specs,
        out_shape=out_shape,
        scratch_shapes=[pltpu.VMEM((t, gw), BF16),
                        pltpu.VMEM((t // CHUNK, G, 2 * MIX_DIM, MIX_DIM), F32),
                        pltpu.VMEM((t, gw), F32),
                        pltpu.VMEM((2, G, MIX_DIM, MIX_DIM), F32)],
        compiler_params=_params(("parallel", "parallel"), buffers),
        name="ret_lat" if has_state else "ret_ctx",
    )(*args)


def _softmax_sink_pv(s_parts, v_parts, sink):
    m = sink
    for s in s_parts:
        m = jnp.maximum(m, jnp.max(s, axis=-1, keepdims=True))
    es = [jnp.exp(s - m) for s in s_parts]
    den = jnp.exp(sink - m)
    for e in es:
        den = den + jnp.sum(e, axis=-1, keepdims=True)
    out = None
    for e, v in zip(es, v_parts):
        o = _dot(e.astype(BF16), v)
        out = o if out is None else out + o
    return out * (1.0 / den)


def _attn_ctx_kernel(q_ref, k_ref, v_ref, sink_ref, o_ref, ko_ref, vo_ref):
    k = k_ref[...]
    v = v_ref[...]
    ko_ref[...] = k
    vo_ref[...] = v
    hd = ATT_HEAD_DIM
    scale = hd ** -0.5
    kjs = [k[:, j * hd:(j + 1) * hd].astype(BF16) for j in range(ATT_KV_HEADS)]
    vjs = [v[:, j * hd:(j + 1) * hd].astype(BF16) for j in range(ATT_KV_HEADS)]

    def scores(hx):
        return [_dot_nt(q_ref[:, hx * hd:(hx + 1) * hd].astype(BF16), kjs[hx // ATT_GROUPS]) * scale]

    s_next = scores(0)
    for hx in range(ATT_HEADS):
        s_parts = s_next
        if hx + 1 < ATT_HEADS:
            s_next = scores(hx + 1)
        sink = jnp.full((1, 1), sink_ref[hx], F32)
        o_ref[:, hx * hd:(hx + 1) * hd] = _softmax_sink_pv(s_parts, [vjs[hx // ATT_GROUPS]], sink).astype(BF16)


def _attn_ctx(p, nb, t, row_off, sink):
    assert row_off % t == 0
    ob = row_off // t
    qw = ATT_HEADS * ATT_HEAD_DIM
    kvw = ATT_KV_HEADS * ATT_HEAD_DIM
    return pl.pallas_call(
        _attn_ctx_kernel,
        grid=(nb,),
        in_specs=[
            _p_spec(t, qw, lambda b: (ob + b, COL_B_Q * LANE // qw)),
            _p_spec(t, kvw, lambda b: (ob + b, COL_B_K)),
            _p_spec(t, kvw, lambda b: (ob + b, COL_B_V)),
            pl.BlockSpec(memory_space=pltpu.SMEM),
        ],
        out_specs=[pl.BlockSpec((t, qw), lambda b: (b, 0)),
                   pl.BlockSpec((None, t, kvw), lambda b: (b, 0, 0)),
                   pl.BlockSpec((None, t, kvw), lambda b: (b, 0, 0))],
        out_shape=[jax.ShapeDtypeStruct((nb * t, qw), BF16),
                   jax.ShapeDtypeStruct((nb, t, kvw), F32),
                   jax.ShapeDtypeStruct((nb, t, kvw), F32)],
        compiler_params=_params(("parallel",), 2 * t * (qw * 6 + 4 * kvw * 4)),
        name="attn_ctx",
    )(p, p, p, sink)


def _band_bias():
    Q = QBLOCK
    r = np.arange(Q)[:, None]
    c = np.arange(3 * Q)[None, :]
    band = (c >= r) & (c <= r + 2 * Q)
    out = []
    for kind in range(3):
        ok = band.copy()
        if kind == 0:
            ok &= c >= Q
        if kind == 2:
            ok &= c < 2 * Q
        out.append(np.where(ok, 0.0, NEG).astype(np.float32))
    return jnp.asarray(np.stack(out))


def _half_variants(x):
    low = lax.broadcasted_iota(jnp.int32, x.shape, 1) < ATT_HEAD_DIM
    sw = pltpu.roll(x, ATT_HEAD_DIM, 1)
    zero = jnp.zeros_like(x)
    return [jnp.where(low, a, b).astype(BF16) for a, b in ((x, zero), (zero, sw), (sw, zero), (zero, x))]


def _attn_lat_kernel(t, q_ref, k_ref, v_ref, kc_ref, vc_ref, cq_ref, saq_ref, sbq_ref,
                     ck_ref, sak_ref, sbk_ref, bias_ref, sink_ref, o_ref, kp_scr, vp_scr, kc_scr, vc_scr):
    n = pl.program_id(1)
    Q = QBLOCK
    hd = ATT_HEAD_DIM
    quarter = hd // 4
    nvar = 2 * ATT_KV_HEADS

    @pl.when(n == 0)
    def _():
        zero = jnp.zeros((Q, LANE), BF16)
        for vi in range(nvar):
            kp_scr[vi, 0:Q, :] = zero
            vp_scr[vi, 0:Q, :] = zero
            kp_scr[vi, Q + t:2 * Q + t, :] = zero
            vp_scr[vi, Q + t:2 * Q + t, :] = zero

        def fill(c, _):
            rows = _chunk_rows(c, Q)
            dst = pl.ds(pl.multiple_of(c * Q + Q, Q), Q)
            kk = _rope(k_ref[rows, :], ck_ref[rows, :], sak_ref[rows, :], sbk_ref[rows, :], quarter)
            for vi, (kv, vv) in enumerate(zip(_half_variants(kk), _half_variants(v_ref[rows, :]))):
                kp_scr[vi, dst, :] = kv
                vp_scr[vi, dst, :] = vv
            return 0

        lax.fori_loop(0, t // Q, fill, 0)
        for vi, (kv, vv) in enumerate(zip(_half_variants(kc_ref[...]), _half_variants(vc_ref[...]))):
            kc_scr[vi] = kv
            vc_scr[vi] = vv

    band = pl.ds(pl.multiple_of(n * Q, Q), 3 * Q)
    cq, saq, sbq = cq_ref[...], saq_ref[...], sbq_ref[...]
    bias = bias_ref[...]
    scale = hd ** -0.5
    heads_per_chunk = LANE // hd
    qcs = [(_rope(q_ref[:, c * LANE:(c + 1) * LANE], cq, saq, sbq, quarter) * scale).astype(BF16)
           for c in range(ATT_HEADS // heads_per_chunk)]

    def variant(hx):
        return 2 * (hx // ATT_GROUPS) + hx % heads_per_chunk

    def scores(hx):
        qc, vi = qcs[hx // heads_per_chunk], variant(hx)
        return [_dot_nt(qc, kp_scr[vi, band, :]) + bias, _dot_nt(qc, kc_scr[vi])]

    acc = None
    pending = [scores(hx) for hx in range(ATT_AHEAD)]
    for hx in range(ATT_HEADS):
        s_parts = pending.pop(0)
        if hx + ATT_AHEAD < ATT_HEADS:
            pending.append(scores(hx + ATT_AHEAD))
        vi = variant(hx)
        sink = jnp.full((1, 1), sink_ref[hx], F32)
        o = _softmax_sink_pv(s_parts, [vp_scr[vi, band, :], vc_scr[vi]], sink)
        acc = o if acc is None else acc + o
        if hx % heads_per_chunk == heads_per_chunk - 1:
            c = hx // heads_per_chunk
            o_ref[:, c * LANE:(c + 1) * LANE] = acc.astype(BF16)
            acc = None


def _attn_lat(p, nb, t, kc, vc, layer, tabs, sink):
    assert (ATT_HEAD_DIM ** -0.5) == 2.0 ** round(np.log2(ATT_HEAD_DIM ** -0.5))
    qw = ATT_HEADS * ATT_HEAD_DIM
    kvw = ATT_KV_HEADS * ATT_HEAD_DIM
    assert kvw == LANE
    nq = t // QBLOCK
    past = kc.shape[2]
    nvar = 2 * ATT_KV_HEADS
    cos, sa, sb = tabs
    bias = _band_bias()
    tq = pl.BlockSpec((QBLOCK, LANE), lambda b, n: (n, 0))
    tk = pl.BlockSpec((t, LANE), lambda b, n: (0, 0))
    return pl.pallas_call(
        functools.partial(_attn_lat_kernel, t),
        grid=(nb, nq),
        in_specs=[
            _p_spec(QBLOCK, qw, lambda b, n: (b * nq + n, COL_B_Q * LANE // qw)),
            _p_spec(t, kvw, lambda b, n: (b, COL_B_K)),
            _p_spec(t, kvw, lambda b, n: (b, COL_B_V)),
            pl.BlockSpec((None, None, past, kvw), lambda b, n: (b, layer, 0, 0)),
            pl.BlockSpec((None, None, past, kvw), lambda b, n: (b, layer, 0, 0)),
            tq, tq, tq, tk, tk, tk,
            pl.BlockSpec((None,) + bias.shape[1:],
                         lambda b, n: (jnp.where(n == 0, 0, jnp.where(n == nq - 1, 2, 1)), 0, 0)),
            pl.BlockSpec(memory_space=pltpu.SMEM),
        ],
        out_specs=pl.BlockSpec((QBLOCK, qw), lambda b, n: (b * nq + n, 0)),
        out_shape=jax.ShapeDtypeStruct((nb * t, qw), BF16),
        scratch_shapes=[pltpu.VMEM((nvar, t + 2 * QBLOCK, LANE), BF16), pltpu.VMEM((nvar, t + 2 * QBLOCK, LANE), BF16),
                        pltpu.VMEM((nvar, past, LANE), BF16), pltpu.VMEM((nvar, past, LANE), BF16)],
        compiler_params=_params(("parallel", "arbitrary"),
                                2 * 5 * t * LANE * 4 + 2 * nvar * (t + 2 * QBLOCK + past) * LANE * 2 + 2 * MIB),
        name="attn_lat",
    )(p, p, p, kc, vc, cos, sa, sb, cos, sa, sb, bias, sink)


def _conv_kernel(t, cb_ref, cc_ref, cx_ref, w_ref, o_ref):
    u = cc_ref[...] * cx_ref[...]
    tok = lax.broadcasted_iota(jnp.int32, u.shape, 0)
    prev = jnp.where(tok == 0, 0.0, pltpu.roll(u, 1, 0))
    nxt = jnp.where(tok == t - 1, 0.0, pltpu.roll(u, t - 1, 0))
    o_ref[...] = (cb_ref[...] * (w_ref[0:1, :] * prev + w_ref[1:2, :] * u + w_ref[2:3, :] * nxt)).astype(BF16)


def _conv(p, nb, t, row_off, w):
    assert row_off % t == 0
    ob = row_off // t
    nblk = w.shape[1] // LANE

    def col(cb):
        return _p_spec(t, LANE, lambda b, j: (ob + b, cb + j))

    return pl.pallas_call(
        functools.partial(_conv_kernel, t),
        grid=(nb, nblk),
        in_specs=[col(COL_C_B), col(COL_C_C), col(COL_C_X),
                  pl.BlockSpec((w.shape[0], LANE), lambda b, j: (0, j))],
        out_specs=pl.BlockSpec((t, LANE), lambda b, j: (b, j)),
        out_shape=jax.ShapeDtypeStruct((nb * t, w.shape[1]), BF16),
        compiler_params=_params(("parallel", "parallel"), 2 * 4 * t * LANE * 4),
        name="conv",
    )(p, p, p, w)


GATE_SRC = 4 * MIX_HEADS * MIX_DIM
N_GATES = 4 * MIX_HEADS


def _gate_lanes(x):
    g = x.reshape(x.shape[:-1] + (2, 2, MIX_HEADS))
    pad = [(0, 0)] * (x.ndim - 1) + [(0, LANE - 2 * MIX_HEADS)]
    blocks = [jnp.pad(g[..., kind, :].reshape(x.shape[:-1] + (2 * MIX_HEADS,)), pad) for kind in range(2)]
    return jnp.concatenate(blocks, axis=-1)


def _column_slabs(w, tile):
    k, n = w.shape[-2:]
    return jnp.swapaxes(w.reshape(w.shape[:-1] + (n // tile, tile)), -3, -2)


def _stage_mixer_proj(w_in):
    head = w_in[..., 0:GATE_SRC]
    tail = w_in[..., GATE_SRC + N_GATES:]
    gates = _gate_lanes(w_in[..., GATE_SRC:GATE_SRC + N_GATES])
    assert head.shape[-1] + tail.shape[-1] == COL_GATES * LANE
    assert head.shape[-1] + tail.shape[-1] + gates.shape[-1] == PROJ_P
    return jnp.concatenate([head, tail, gates], axis=-1).astype(BF16)


def kernel(x_prompt, x_sample, cache_attn_k, cache_attn_v, state_mlstm_C, state_mlstm_n, state_mlstm_m,
           state_ret_S, c, c_ctx, w_mod, b_mod, norm_g, ffn1_w_in, ffn1_w_out, ffn2_w_in, ffn2_w_out,
           w_in, mlstm_gate_b, mlstm_norm_g, attn_sink, conv_w, ret_log_decay, ret_norm_g,
           w_branch, w_gate, w_out):
    bp, tp, d = x_prompt.shape
    bs, ts, _ = x_sample.shape
    depth = w_mod.shape[0]
    past = cache_attn_k.shape[2]
    kvw = ATT_KV_HEADS * ATT_HEAD_DIM
    fits = [t for t in (1024, 512, 256) if ts % t == 0 and (bp * tp) % t == 0]
    rows = _Rows(bs, ts, bp, tp, fits[0])
    rows_ffn = _Rows(bs, ts, bp, tp, fits[min(1, len(fits) - 1)])

    nbp = -(-(1 + bs) // 8) * 8
    cpad = jnp.concatenate([c_ctx[None], c, jnp.zeros((nbp - 1 - bs, d), F32)], axis=0)
    mod = _modulation(cpad, w_mod, b_mod).reshape(depth, nbp, 3, 3, d)

    w_p = _stage_mixer_proj(w_in)
    ffn1_in, ffn1_out = ffn1_w_in.astype(BF16), ffn1_w_out.astype(BF16)
    ffn2_in, ffn2_out = ffn2_w_in.astype(BF16), ffn2_w_out.astype(BF16)
    tn = _pick_tile(d, 256)
    wg_b = jnp.swapaxes(_column_slabs(w_gate.astype(BF16), tn), 1, 2)
    wb_b = jnp.swapaxes(_column_slabs(w_branch.astype(BF16), tn), 1, 2)
    wo_b = w_out.astype(BF16)

    gate_bias = _gate_lanes(mlstm_gate_b.reshape(depth, 1, N_GATES))
    tabs_att = _rope_tables(ts, ATT_HEAD_DIM)
    tabs_ret = _rope_tables(ts, MIX_DIM)

    x = jnp.concatenate([x_sample.reshape(bs * ts, d), x_prompt.reshape(bp * tp, d)], axis=0)
    ms = bs * ts
    kc_all = cache_attn_k.reshape(bs, depth, past, kvw)
    vc_all = cache_attn_v.reshape(bs, depth, past, kvw)
    ks_, vs_, cs_, ns_, mms_, ss_ = [], [], [], [], [], []
    for l in range(depth):
        g = norm_g[l]
        x = _ffn(rows_ffn, x, mod[l, :, 0], g[0:2], ffn1_in, ffn1_out, l)

        p, h = _inproj(rows, x, mod[l, :, 1], g[2:4], w_p, l)
        a_gain = mlstm_norm_g[l][None]
        d_gain = ret_norm_g[l][None]
        lat_a = _mlstm(p, bs, ts, 0, gate_bias[l], a_gain, (state_mlstm_C, state_mlstm_n, state_mlstm_m, l))[0]
        lat_b = _attn_lat(p, bs, ts, kc_all, vc_all, l, tabs_att, attn_sink[l])
        lat_c = _conv(p, bs, ts, 0, conv_w[l])
        lat_d = _retention(p, bs, ts, 0, ret_log_decay[l], d_gain, (tabs_ret, state_ret_S, l))[0]
        ctx_a, c_fin, n_fin, m_fin = _mlstm(p, bp, tp, ms, gate_bias[l], a_gain, None)
        ctx_b, k_new, v_new = _attn_ctx(p, bp, tp, ms, attn_sink[l])
        ctx_c = _conv(p, bp, tp, ms, conv_w[l])
        ctx_d, s_fin = _retention(p, bp, tp, ms, ret_log_decay[l], d_gain, None)
        y = _merge(rows, h, (lat_a, lat_b, lat_c, lat_d), (ctx_a, ctx_b, ctx_c, ctx_d), wg_b, wb_b, l)
        x = _outproj(rows_ffn, x, mod[l, :, 1], g[2:4], y, wo_b, l)

        x = _ffn(rows_ffn, x, mod[l, :, 2], g[4:6], ffn2_in, ffn2_out, l)

        ks_.append(k_new.reshape(bp, tp, ATT_KV_HEADS, ATT_HEAD_DIM))
        vs_.append(v_new.reshape(bp, tp, ATT_KV_HEADS, ATT_HEAD_DIM))
        cs_.append(c_fin)
        ns_.append(n_fin)
        mms_.append(m_fin[:, :, 0].reshape(bp, 2, MIX_HEADS))
        ss_.append(s_fin)

    y_sample = x[:ms].reshape(bs, ts, d)
    y_prompt = x[ms:].reshape(bp, tp, d)
    return (y_prompt, y_sample, jnp.stack(ks_, axis=1), jnp.stack(vs_, axis=1), jnp.stack(cs_, axis=1),
            jnp.stack(ns_, axis=1), jnp.stack(mms_, axis=1), jnp.stack(ss_, axis=1))
```

```python
import functools

import numpy as np
import jax
import jax.numpy as jnp
from jax import lax
from jax.experimental import pallas as pl
from jax.experimental.pallas import tpu as pltpu

F32 = jnp.float32
BF16 = jnp.bfloat16

N_MOD = 9
N_BRANCH = 4
BRANCH_W = 512
MIX_HEADS = 4
MIX_DIM = 128
RET_GROUP = 2
ATT_HEADS = 8
ATT_KV_HEADS = 2
ATT_HEAD_DIM = 64
ATT_GROUPS = ATT_HEADS // ATT_KV_HEADS
QBLOCK = 128
ATT_AHEAD = 4
CHUNK = 128
GRID_W = 64
ROPE_BASE = 10000.0
EPS = 1e-6
NEG = -1e30

LANE = 128
COL_A_Q, COL_A_K, COL_A_V, COL_A_O = 0, 4, 8, 12
COL_B_Q, COL_B_K, COL_B_V = 16, 20, 21
COL_C_B, COL_C_C, COL_C_X = 22, 26, 30
COL_D_Q, COL_D_K, COL_D_V, COL_D_G = 34, 38, 42, 46
COL_GATES = 50
PROJ_BLOCKS = 52
PROJ_P = PROJ_BLOCKS * LANE

MIB = 1024 * 1024
VMEM_CAP = 44 * MIB
VMEM_TEMPS = 12 * MIB
SINGLE = pl.Buffered(1)


def _params(sem, buffers):
    return pltpu.CompilerParams(dimension_semantics=sem,
                                vmem_limit_bytes=int(min(buffers + VMEM_TEMPS, VMEM_CAP)))


def _pick_tile(n, pref):
    if n <= pref:
        return n
    t = (pref // LANE) * LANE
    while t >= LANE:
        if n % t == 0:
            return t
        t -= LANE
    raise ValueError(f"no lane-aligned tile divides {n}")


def _dot(a, b):
    return jnp.dot(a, b, preferred_element_type=F32)


def _dot_nt(a, b):
    return lax.dot_general(a, b, (((1,), (1,)), ((), ())), preferred_element_type=F32)


def _dot_tn(a, b):
    return lax.dot_general(a, b, (((0,), (0,)), ((), ())), preferred_element_type=F32)


def _sigmoid(x):
    return 1.0 / (1.0 + jnp.exp(-x))


def _rms(xf, g):
    ms = jnp.mean(xf * xf, axis=-1, keepdims=True)
    return xf * lax.rsqrt(ms + EPS) * g


def _modulated(x_ref, mod_ref, g_ref):
    return _rms(x_ref[...], g_ref[0:1, :]) * (1.0 + mod_ref[1:2, :]) + mod_ref[0:1, :]


def _head_norm(x, g):
    mu = jnp.mean(x, axis=-1, keepdims=True)
    xc = x - mu
    var = jnp.mean(xc * xc, axis=-1, keepdims=True)
    return xc * lax.rsqrt(var + EPS) * g


def _chunk_rows(c, n=CHUNK):
    return pl.ds(pl.multiple_of(c * n, n), n)


def _mod_kernel(c_ref, w_ref, b_ref, o_ref):
    c = c_ref[...]
    s = (c * _sigmoid(c)).astype(BF16)
    o_ref[...] = _dot(s, w_ref[...].astype(BF16)) + b_ref[...]


def _modulation(cpad, w_mod, b_mod):
    depth, d, n = w_mod.shape
    nbp = cpad.shape[0]
    tn = _pick_tile(n, 1024)
    return pl.pallas_call(
        _mod_kernel,
        grid=(depth, n // tn),
        in_specs=[
            pl.BlockSpec((nbp, d), lambda l, j: (0, 0)),
            pl.BlockSpec((None, d, tn), lambda l, j: (l, 0, j)),
            pl.BlockSpec((None, 1, tn), lambda l, j: (l, 0, j)),
        ],
        out_specs=pl.BlockSpec((None, nbp, tn), lambda l, j: (l, 0, j)),
        out_shape=jax.ShapeDtypeStruct((depth, nbp, n), F32),
        compiler_params=_params(("parallel", "parallel"), 2 * d * tn * 4 + MIB),
        name="modulation",
    )(cpad, w_mod, b_mod.reshape(depth, 1, n))


class _Rows:
    def __init__(self, bs, ts, bp, tp, tm):
        self.bs, self.ts, self.bp, self.tp = bs, ts, bp, tp
        self.ms, self.mp = bs * ts, bp * tp
        self.m = self.ms + self.mp
        assert ts % tm == 0 and self.mp % tm == 0 and self.ms % tp == 0
        self.tm = tm
        self.lat_tiles = self.ms // tm
        self.tiles_per_seq = ts // tm

    def mod_index(self, i):
        return jnp.where(i < self.lat_tiles, 1 + i // self.tiles_per_seq, 0)

    def x_spec(self, d, mode=None):
        return pl.BlockSpec((self.tm, d), lambda i, *_: (i, 0), pipeline_mode=mode)

    def mod_spec(self, d):
        return pl.BlockSpec((None, 3, d), lambda i, *_: (self.mod_index(i), 0, 0))


P_SLAB = 512


def _p_spec(rows, width, row_col, mode=None):
    per = P_SLAB // width
    assert per * width == P_SLAB

    def index_map(*ids):
        r, c = row_col(*ids)
        return (c // per, r, c % per)

    return pl.BlockSpec((None, rows, width), index_map, pipeline_mode=mode)


def _layer_spec(block, index_map, l, mode=None):
    return pl.BlockSpec((None,) + tuple(block), lambda *ids: (l,) + tuple(index_map(*ids)), pipeline_mode=mode)


def _ffn_kernel(nf, x_ref, mod_ref, g_ref, wg_ref, wu_ref, wo_ref, o_ref, h_scr, acc_scr):
    f = pl.program_id(1)

    def hidden_slab(h):
        tf = wg_ref.shape[1]
        halves = [slice(0, tf // 2), slice(tf // 2, tf)] if tf % (2 * LANE) == 0 else [slice(0, tf)]
        gate_up = [(_dot(h, wg_ref[:, sl]), _dot(h, wu_ref[:, sl])) for sl in halves]
        acts = [((g * _sigmoid(g)) * u).astype(BF16) for g, u in gate_up]
        out = None
        for a, sl in zip(acts, halves):
            o = _dot(a, wo_ref[sl, :])
            out = o if out is None else out + o
        return out

    def pre():
        h = _modulated(x_ref, mod_ref, g_ref).astype(BF16)
        h_scr[...] = h
        return h

    def post(z):
        o_ref[...] = x_ref[...] + (0.5 * mod_ref[2:3, :]) * _rms(z, g_ref[1:2, :])

    if nf == 1:
        post(hidden_slab(pre()))
        return

    @pl.when(f == 0)
    def _():
        acc_scr[...] = hidden_slab(pre())

    @pl.when((f > 0) & (f < nf - 1))
    def _():
        acc_scr[...] += hidden_slab(h_scr[...])

    @pl.when(f == nf - 1)
    def _():
        post(acc_scr[...] + hidden_slab(h_scr[...]))


def _ffn(rows, x, mod_s, g2, w_in, w_out, l, tiles=None):
    m, d = x.shape
    dff = w_out.shape[1]
    tm = rows.tm
    tf = _pick_tile(dff, 512)
    nf = dff // tf
    first, count = tiles if tiles is not None else (0, m // tm)
    m = count * tm
    buffers = tm * d * (2 * 4 + 2 * 4 + 4 + 2) + 3 * 2 * d * tf * 2
    return pl.pallas_call(
        functools.partial(_ffn_kernel, nf),
        grid=(count, nf),
        in_specs=[
            pl.BlockSpec((tm, d), lambda i, f: (first + i, 0)),
            pl.BlockSpec((None, 3, d), lambda i, f: (rows.mod_index(first + i), 0, 0)),
            pl.BlockSpec((2, d), lambda i, f: (0, 0)),
            _layer_spec((d, tf), lambda i, f: (0, f), l),
            _layer_spec((d, tf), lambda i, f: (0, f + nf), l),
            _layer_spec((tf, d), lambda i, f: (f, 0), l),
        ],
        out_specs=pl.BlockSpec((tm, d), lambda i, f: (i, 0)),
        out_shape=jax.ShapeDtypeStruct((m, d), F32),
        scratch_shapes=[pltpu.VMEM((tm, d), BF16), pltpu.VMEM((tm, d), F32)],
        compiler_params=_params(("parallel", "arbitrary"), buffers),
        name="ffn",
    )(x, mod_s, g2, w_in, w_in, w_out)


def _inproj_kernel(x_ref, mod_ref, g_ref, w_ref, p_ref, h_ref, h_scr):
    @pl.when(pl.program_id(1) == 0)
    def _():
        h = _modulated(x_ref, mod_ref, g_ref).astype(BF16)
        h_ref[...] = h
        h_scr[...] = h
        p_ref[...] = _dot(h, w_ref[...])

    @pl.when(pl.program_id(1) > 0)
    def _():
        p_ref[...] = _dot(h_scr[...], w_ref[...])


def _inproj(rows, x, mod_s, g2, w_p, l):
    m, d = x.shape
    n = w_p.shape[2]
    tn = P_SLAB
    nt = n // tn
    assert nt * tn == n
    tm = rows.tm
    return pl.pallas_call(
        _inproj_kernel,
        grid=(m // tm, nt),
        in_specs=[
            rows.x_spec(d),
            rows.mod_spec(d),
            pl.BlockSpec((2, d), lambda i, j: (0, 0)),
            _layer_spec((d, tn), lambda i, j: (0, j), l),
        ],
        out_specs=[pl.BlockSpec((None, tm, tn), lambda i, j: (j, i, 0)),
                   pl.BlockSpec((tm, d), lambda i, j: (i, 0))],
        out_shape=[jax.ShapeDtypeStruct((nt, m, tn), F32), jax.ShapeDtypeStruct((m, d), BF16)],
        scratch_shapes=[pltpu.VMEM((tm, d), BF16)],
        compiler_params=_params(("parallel", "arbitrary"),
                                tm * d * (2 * 4 + 3 * 2) + 2 * d * tn * 2 + 2 * tm * tn * 4),
        name="inproj",
    )(x, mod_s, g2, w_p)


def _merge_kernel(lat_tiles, h_ref, *refs):
    lat_refs, ctx_refs = refs[0:N_BRANCH], refs[N_BRANCH:2 * N_BRANCH]
    wg_ref, wb_ref, y_ref = refs[2 * N_BRANCH:]
    h = h_ref[...]
    is_lat = pl.program_id(0) < lat_tiles
    y = None
    for i in range(N_BRANCH):
        o = jnp.where(is_lat, lat_refs[i][...], ctx_refs[i][...])
        term = _sigmoid(_dot(h, wg_ref[i])) * _dot(o, wb_ref[i])
        y = term if y is None else y + term
    y_ref[...] = y.astype(BF16)


def _merge(rows, h, lat, ctx, w_gate, w_branch, l):
    m, d = h.shape
    tm = rows.tm
    nt, _, bw, tn = w_branch.shape[1:]
    assert w_gate.shape[1:] == (nt, N_BRANCH, d, tn)
    last_lat = rows.lat_tiles - 1
    lat_spec = pl.BlockSpec((tm, bw), lambda i, j: (jnp.minimum(i, last_lat), 0))
    ctx_spec = pl.BlockSpec((tm, bw), lambda i, j: (jnp.maximum(i - rows.lat_tiles, 0), 0))
    return pl.pallas_call(
        functools.partial(_merge_kernel, rows.lat_tiles),
        grid=(m // tm, nt),
        in_specs=[
            rows.x_spec(d),
            *([lat_spec] * N_BRANCH), *([ctx_spec] * N_BRANCH),
            _layer_spec((None, N_BRANCH, d, tn), lambda i, j: (j, 0, 0, 0), l),
            _layer_spec((None, N_BRANCH, bw, tn), lambda i, j: (j, 0, 0, 0), l),
        ],
        out_specs=pl.BlockSpec((tm, tn), lambda i, j: (i, j)),
        out_shape=jax.ShapeDtypeStruct((m, d), BF16),
        compiler_params=_params(("parallel", "arbitrary"),
                                2 * tm * d * 2 + 2 * 2 * N_BRANCH * tm * bw * 2
                                + 2 * N_BRANCH * (d + bw) * tn * 2 + 2 * tm * tn * 2),
        name="merge",
    )(h, *lat, *ctx, w_gate, w_branch)


def _outproj_kernel(x_ref, mod_ref, g_ref, y_ref, w_ref, o_ref):
    z = _dot(y_ref[...], w_ref[...])
    o_ref[...] = x_ref[...] + mod_ref[2:3, :] * _rms(z, g_ref[1:2, :])


def _outproj(rows, x, mod_s, g2, y, w_out, l):
    m, d = x.shape
    tm = rows.tm
    return pl.pallas_call(
        _outproj_kernel,
        grid=(m // tm,),
        in_specs=[
            rows.x_spec(d),
            rows.mod_spec(d),
            pl.BlockSpec((2, d), lambda i: (0, 0)),
            rows.x_spec(d),
            _layer_spec((d, d), lambda i: (0, 0), l, SINGLE),
        ],
        out_specs=pl.BlockSpec((tm, d), lambda i: (i, 0)),
        out_shape=jax.ShapeDtypeStruct((m, d), F32),
        compiler_params=_params(("parallel",), tm * d * (2 * 4 + 2 * 4 + 4 + 2 * 2) + d * d * 2),
        name="outproj",
    )(x, mod_s, g2, y, w_out)


def _rope_tables(t, d):
    rows = t // GRID_W
    row = jnp.repeat(jnp.arange(rows), GRID_W).astype(F32)
    col = jnp.tile(jnp.arange(GRID_W), rows).astype(F32)
    quarter = d // 4
    inv = jnp.power(ROPE_BASE, -jnp.arange(quarter, dtype=F32) / quarter)
    ar = row[:, None] * inv
    ac = col[:, None] * inv
    ang = jnp.concatenate([ar, ar, ac, ac], axis=-1)
    cos, sin = jnp.cos(ang), jnp.sin(ang)
    first = (jnp.arange(d) % (2 * quarter)) < quarter
    sa = jnp.where(first, -sin, 0.0)
    sb = jnp.where(first, 0.0, sin)
    reps = LANE // d
    return tuple(jnp.tile(a, (1, reps)) for a in (cos, sa, sb))


def _rope(x, cos, sa, sb, quarter):
    n = x.shape[-1]
    ahead = pltpu.roll(x, n - quarter, 1)
    behind = pltpu.roll(x, quarter, 1)
    return x * cos + ahead * sa + behind * sb


def _log_sigmoid(x):
    return jnp.minimum(x, 0.0) - jnp.log1p(jnp.exp(-jnp.abs(x)))


def _split_bf16(x):
    hi = x.astype(BF16)
    return hi, (x - hi.astype(F32)).astype(BF16)


def _mlstm_kernel(t, layer, emit_state, *refs):
    has_state = layer is not None
    it = iter(refs)
    q_ref, k_ref, v_ref, og_ref, gt_ref, bias_ref, gain_ref = (next(it) for _ in range(7))
    if has_state:
        c0_ref, n0_ref, m0_ref = next(it), next(it), next(it)
    o_ref = next(it)
    if emit_state:
        cout_ref, nout_ref, mout_ref = next(it), next(it), next(it)
    h_scr, s_scr = next(it), next(it)

    L, D, H = CHUNK, MIX_DIM, MIX_HEADS
    nc = t // L
    r_i = lax.broadcasted_iota(jnp.int32, (L, L), 0)
    c_i = lax.broadcasted_iota(jnp.int32, (L, L), 1)
    masks = (r_i >= c_i, r_i <= c_i)
    cum = tuple(mk.astype(BF16) for mk in masks)
    ones = jnp.ones((L, D), BF16)
    sel = [(r_i == g).astype(BF16) for g in range(2 * H)]
    bias = bias_ref[...]
    scale = D ** -0.5

    ms0 = []
    for d in range(2):
        for h in range(H):
            if has_state:
                s_scr[d, h, :, 0:D] = c0_ref[d, h]
                s_scr[d, h, :, D:2 * D] = jnp.broadcast_to(n0_ref[d, h:h + 1, :], (D, D)).T
                ms0.append(jnp.full((1, D), m0_ref[pl.program_id(0), layer, d, h], F32))
            else:
                s_scr[d, h] = jnp.zeros((D, 2 * D), F32)
                ms0.append(jnp.zeros((1, D), F32))

    def zero_rows(c, _):
        h_scr[_chunk_rows(c), :] = jnp.zeros((L, H * D), F32)
        return 0

    lax.fori_loop(0, nc, zero_rows, 0)

    def body(c, ms):
        chains = []
        for d, cc in ((0, c), (1, nc - 1 - c)):
            rows = _chunk_rows(cc)
            g = gt_ref[rows, :] + bias
            lf_hi, lf_lo = _split_bf16(_log_sigmoid(g[:, LANE:2 * LANE]))
            f_all = _dot(cum[d], lf_hi) + _dot(cum[d], lf_lo)
            b_all = g[:, 0:LANE] - f_all
            b_t = b_all.T
            f_hi, f_lo = _split_bf16(f_all)
            b_hi, b_lo = _split_bf16(b_all)
            for h in range(H):
                j = d * H + h
                cols = slice(h * D, (h + 1) * D)
                k = k_ref[rows, cols] * scale
                qb = q_ref[rows, cols].astype(BF16)
                s_old = s_scr[d, h]
                chains.append(dict(
                    d=d, h=h, j=j, rows=rows, cols=cols, k=k, s_old=s_old, brow=b_t[j:j + 1, :],
                    v_aug=jnp.concatenate([v_ref[rows, cols].astype(BF16), ones], axis=1),
                    f_rep=_dot(f_hi, sel[j]) + _dot(f_lo, sel[j]),
                    b_rep=_dot(b_hi, sel[j]) + _dot(b_lo, sel[j]),
                    qk=_dot_nt(qb, k.astype(BF16)),
                    qs=_dot(qb, s_old.astype(BF16))))
        for ch in chains:
            d, f_rep, b_rep, m = ch["d"], ch["f_rep"], ch["b_rep"], ms[ch["j"]]
            ch["f_last"] = f_rep[L - 1:L, :] if d == 0 else f_rep[0:1, :]
            lw = jnp.where(masks[d], f_rep + ch["brow"], NEG)
            inter = f_rep + m
            m_tok = jnp.maximum(jnp.max(lw, axis=-1, keepdims=True), inter)
            ch["a"] = (ch["qk"] * jnp.exp(lw - m_tok)).astype(BF16)
            ch["w_inter"] = jnp.exp(inter - m_tok)
            ch["floor"] = jnp.exp(-m_tok)
            ch["bmax"] = jnp.max(b_rep, axis=0, keepdims=True)
            ch["kw"] = (ch["k"] * jnp.exp(b_rep - ch["bmax"])).astype(BF16)
        for ch in chains:
            ch["pv"] = _dot(ch["a"], ch["v_aug"])
            ch["u_aug"] = _dot_tn(ch["kw"], ch["v_aug"])
        out = list(ms)
        for ch in chains:
            d, h, pv, qs, w_inter, m = ch["d"], ch["h"], ch["pv"], ch["qs"], ch["w_inter"], ms[ch["j"]]
            num = pv[:, 0:D] + w_inter * qs[:, 0:D]
            den = pv[:, D:2 * D] + w_inter * qs[:, D:2 * D]
            h_scr[ch["rows"], ch["cols"]] += num * (1.0 / jnp.maximum(jnp.abs(den), ch["floor"]))
            a_end = ch["f_last"] + ch["bmax"]
            m_new = jnp.maximum(ch["f_last"] + m, a_end)
            d_old = jnp.exp(ch["f_last"] + m - m_new)
            d_new = jnp.exp(a_end - m_new)
            for half in (slice(0, D), slice(D, 2 * D)):
                s_scr[d, h, :, half] = d_old * ch["s_old"][:, half] + d_new * ch["u_aug"][:, half]
            out[ch["j"]] = m_new
        return tuple(out)

    ms = lax.fori_loop(0, nc, body, tuple(ms0))

    gain = gain_ref[...]

    def finish(c, _):
        rows = _chunk_rows(c)
        for h in range(H):
            cols = slice(h * D, (h + 1) * D)
            y = _head_norm(h_scr[rows, cols], gain[:, cols])
            o_ref[rows, cols] = (_sigmoid(og_ref[rows, cols]) * y).astype(BF16)
        return 0

    lax.fori_loop(0, nc, finish, 0)

    if emit_state:
        for d in range(2):
            for h in range(H):
                cout_ref[d, h] = s_scr[d, h, :, 0:D]
                nout_ref[d, h:h + 1, :] = s_scr[d, h, :, D:2 * D].T[0:1, :]
                mout_ref[d * H + h:d * H + h + 1, :] = ms[d * H + h]


def _mlstm(p, nb, t, row_off, bias, gain, state):
    assert row_off % t == 0
    ob = row_off // t
    has_state = state is not None
    emit_state = not has_state
    hw = MIX_HEADS * MIX_DIM
    st_c = pl.BlockSpec((None, 2, MIX_HEADS, MIX_DIM, MIX_DIM), lambda b: (b, 0, 0, 0, 0))
    st_n = pl.BlockSpec((None, 2, MIX_HEADS, MIX_DIM), lambda b: (b, 0, 0, 0))

    def col(cb):
        return _p_spec(t, hw, lambda b: (ob + b, cb * LANE // hw), mode)

    mode = SINGLE if t * hw * 4 >= 2 * MIB else None
    nbuf = 1 if mode is SINGLE else 2
    buffers = (nbuf * (4 * t * hw + t * 2 * LANE) * 4 + 2 * t * hw * 2 + t * hw * 4
               + 6 * 2 * MIX_HEADS * MIX_DIM * MIX_DIM * 4)
    in_specs = [col(COL_A_Q), col(COL_A_K), col(COL_A_V), col(COL_A_O),
                _p_spec(t, 2 * LANE, lambda b: (ob + b, COL_GATES // 2), mode),
                pl.BlockSpec((1, 2 * LANE), lambda b: (0, 0)),
                pl.BlockSpec((1, hw), lambda b: (0, 0))]
    args = [p, p, p, p, p, bias, gain]
    layer = None
    if has_state:
        layer = state[3]
        in_specs += [pl.BlockSpec((None, None, 2, MIX_HEADS, MIX_DIM, MIX_DIM), lambda b: (b, layer, 0, 0, 0, 0)),
                     pl.BlockSpec((None, None, 2, MIX_HEADS, MIX_DIM), lambda b: (b, layer, 0, 0, 0)),
                     pl.BlockSpec(memory_space=pltpu.SMEM)]
        args += list(state[0:3])
    out_shape = [jax.ShapeDtypeStruct((nb * t, hw), BF16)]
    out_specs = [pl.BlockSpec((t, hw), lambda b: (b, 0))]
    if emit_state:
        out_shape += [jax.ShapeDtypeStruct((nb, 2, MIX_HEADS, MIX_DIM, MIX_DIM), F32),
                      jax.ShapeDtypeStruct((nb, 2, MIX_HEADS, MIX_DIM), F32),
                      jax.ShapeDtypeStruct((nb, 2 * MIX_HEADS, MIX_DIM), F32)]
        out_specs += [st_c, st_n, pl.BlockSpec((None, 2 * MIX_HEADS, MIX_DIM), lambda b: (b, 0, 0))]
    return pl.pallas_call(
        functools.partial(_mlstm_kernel, t, layer, emit_state),
        grid=(nb,),
        in_specs=in_specs,
        out_specs=out_specs,
        out_shape=out_shape,
        scratch_shapes=[pltpu.VMEM((t, hw), F32),
                        pltpu.VMEM((2, MIX_HEADS, MIX_DIM, 2 * MIX_DIM), F32)],
        compiler_params=_params(("parallel",), buffers),
        name="mlstm_lat" if has_state else "mlstm_ctx",
    )(*args)


def _ret_kernel(t, has_state, emit_state, *refs):
    it = iter(refs)
    q_ref, k_ref, v_ref, dg_ref, lg_ref, gain_ref = (next(it) for _ in range(6))
    if has_state:
        cos_ref, sa_ref, sb_ref, s0_ref = (next(it) for _ in range(4))
    o_ref = next(it)
    if emit_state:
        sout_ref = next(it)
    q_scr, u_scr, o_scr, s_scr = (next(it) for _ in range(4))

    L, D, G = CHUNK, MIX_DIM, RET_GROUP
    nc = t // L
    diff = (lax.broadcasted_iota(jnp.int32, (L, L), 0) - lax.broadcasted_iota(jnp.int32, (L, L), 1)).astype(F32)
    tcol = lax.broadcasted_iota(jnp.int32, (L, 1), 0).astype(F32)
    scale = D ** -0.5
    dm, w_end, w_cross, decay = [], [], [], []
    for j in range(G):
        hx = pl.program_id(1) * G + j
        lgs = (lg_ref[0, hx], lg_ref[1, hx])
        dm.append(jnp.where(diff >= 0.0, jnp.exp(jnp.maximum(diff, 0.0) * lgs[0]), 0.0)
                  + jnp.where(diff <= 0.0, jnp.exp(jnp.maximum(-diff, 0.0) * lgs[1]), 0.0))
        w_end.append((jnp.exp((L - 1.0 - tcol) * lgs[0]), jnp.exp(tcol * lgs[1])))
        w_cross.append((jnp.exp((tcol + 1.0) * lgs[0]), jnp.exp((L - tcol) * lgs[1])))
        decay.append(tuple(jnp.exp(jnp.full((1, 1), float(L), F32) * lgs[d]) for d in range(2)))
        for d in range(2):
            s_scr[d, j] = s0_ref[d, j] if has_state else jnp.zeros((D, D), F32)

    def local(cp, _):
        chains = []
        for c in (2 * cp, 2 * cp + 1):
            rows = _chunk_rows(c)
            if has_state:
                cos, sa, sb = cos_ref[rows, :], sa_ref[rows, :], sb_ref[rows, :]
            for j in range(G):
                cols = slice(j * D, (j + 1) * D)
                q = q_ref[rows, cols]
                k = k_ref[rows, cols] * scale
                if has_state:
                    q = _rope(q, cos, sa, sb, D // 4)
                    k = _rope(k, cos, sa, sb, D // 4)
                qb = q.astype(BF16)
                vb = v_ref[rows, cols].astype(BF16)
                q_scr[rows, cols] = qb
                kw = jnp.concatenate([(k * w_end[j][0]).astype(BF16), (k * w_end[j][1]).astype(BF16)], axis=1)
                u_scr[c, j] = _dot_tn(kw, vb)
                chains.append((rows, cols, j, vb, _dot_nt(qb, k.astype(BF16))))
        weighted = [(qk * dm[j]).astype(BF16) for _, _, j, _, qk in chains]
        for (rows, cols, _, vb, _), a in zip(chains, weighted):
            o_scr[rows, cols] = _dot(a, vb)
        return 0

    assert nc % 2 == 0
    lax.fori_loop(0, nc // 2, local, 0)

    def scan(c, _):
        chains = []
        for d, cc in ((0, c), (1, nc - 1 - c)):
            rows = _chunk_rows(cc)
            for j in range(G):
                cols = slice(j * D, (j + 1) * D)
                s_old = s_scr[d, j]
                chains.append((d, cc, j, rows, cols, s_old, _dot(q_scr[rows, cols], s_old.astype(BF16))))
        for d, cc, j, rows, cols, s_old, cross in chains:
            o_scr[rows, cols] += cross * w_cross[j][d]
            s_scr[d, j] = decay[j][d] * s_old + u_scr[cc, j, d * D:(d + 1) * D, :]
        return 0

    lax.fori_loop(0, nc, scan, 0, unroll=2)

    gain = gain_ref[...]

    def finish(c, _):
        rows = _chunk_rows(c)
        for j in range(G):
            cols = slice(j * D, (j + 1) * D)
            y = _head_norm(o_scr[rows, cols], gain[:, cols])
            dg = dg_ref[rows, cols]
            o_ref[rows, cols] = ((dg * _sigmoid(dg)) * y).astype(BF16)
        return 0

    lax.fori_loop(0, nc, finish, 0, unroll=2)

    if emit_state:
        for d in range(2):
            for j in range(G):
                sout_ref[d, j] = s_scr[d, j]


def _retention(p, nb, t, row_off, log_decay, gain, state):
    assert row_off % t == 0 and MIX_HEADS % RET_GROUP == 0 and COL_D_Q % RET_GROUP == 0
    ob = row_off // t
    has_state = state is not None
    emit_state = not has_state
    G = RET_GROUP
    gw = G * MIX_DIM

    def col(cb):
        return _p_spec(t, gw, lambda b, h: (ob + b, cb // G + h))

    in_specs = [col(COL_D_Q), col(COL_D_K), col(COL_D_V), col(COL_D_G),
                pl.BlockSpec(memory_space=pltpu.SMEM),
                pl.BlockSpec((1, gw), lambda b, h: (0, h))]
    args = [p, p, p, p, log_decay, gain]
    if has_state:
        (cos, sa, sb), s0, layer = state
        tab = pl.BlockSpec((t, LANE), lambda b, h: (0, 0))
        in_specs += [tab, tab, tab,
                     pl.BlockSpec((None, None, 2, G, MIX_DIM, MIX_DIM), lambda b, h: (b, layer, 0, h, 0, 0))]
        args += [cos, sa, sb, s0]
    out_shape = [jax.ShapeDtypeStruct((nb * t, BRANCH_W), BF16)]
    out_specs = [pl.BlockSpec((t, gw), lambda b, h: (b, h))]
    if emit_state:
        out_shape.append(jax.ShapeDtypeStruct((nb, 2, MIX_HEADS, MIX_DIM, MIX_DIM), F32))
        out_specs.append(pl.BlockSpec((None, 2, G, MIX_DIM, MIX_DIM), lambda b, h: (b, 0, h, 0, 0)))
    buffers = (2 * 4 * t * gw * 4 + 2 * 3 * t * LANE * 4 + 2 * t * gw * 2
               + t * gw * (2 + 4) + (t // CHUNK) * G * 2 * MIX_DIM * MIX_DIM * 4 + 6 * G * MIX_DIM * MIX_DIM * 4)
    return pl.pallas_call(
        functools.partial(_ret_kernel, t, has_state, emit_state),
        grid=(nb, MIX_HEADS // G),
        in_specs=in_specs,
        out_specs=out_specs,
        out_shape=out_shape,
        scratch_shapes=[pltpu.VMEM((t, gw), BF16),
                        pltpu.VMEM((t // CHUNK, G, 2 * MIX_DIM, MIX_DIM), F32),
                        pltpu.VMEM((t, gw), F32),
                        pltpu.VMEM((2, G, MIX_DIM, MIX_DIM), F32)],
        compiler_params=_params(("parallel", "parallel"), buffers),
        name="ret_lat" if has_state else "ret_ctx",
    )(*args)


def _softmax_sink_pv(s_parts, v_parts, sink):
    m = sink
    for s in s_parts:
        m = jnp.maximum(m, jnp.max(s, axis=-1, keepdims=True))
    es = [jnp.exp(s - m) for s in s_parts]
    den = jnp.exp(sink - m)
    for e in es:
        den = den + jnp.sum(e, axis=-1, keepdims=True)
    out = None
    for e, v in zip(es, v_parts):
        o = _dot(e.astype(BF16), v)
        out = o if out is None else out + o
    return out * (1.0 / den)


def _attn_ctx_kernel(q_ref, k_ref, v_ref, sink_ref, o_ref, ko_ref, vo_ref):
    k = k_ref[...]
    v = v_ref[...]
    ko_ref[...] = k
    vo_ref[...] = v
    hd = ATT_HEAD_DIM
    scale = hd ** -0.5
    kjs = [k[:, j * hd:(j + 1) * hd].astype(BF16) for j in range(ATT_KV_HEADS)]
    vjs = [v[:, j * hd:(j + 1) * hd].astype(BF16) for j in range(ATT_KV_HEADS)]

    def scores(hx):
        return [_dot_nt(q_ref[:, hx * hd:(hx + 1) * hd].astype(BF16), kjs[hx // ATT_GROUPS]) * scale]

    s_next = scores(0)
    for hx in range(ATT_HEADS):
        s_parts = s_next
        if hx + 1 < ATT_HEADS:
            s_next = scores(hx + 1)
        sink = jnp.full((1, 1), sink_ref[hx], F32)
        o_ref[:, hx * hd:(hx + 1) * hd] = _softmax_sink_pv(s_parts, [vjs[hx // ATT_GROUPS]], sink).astype(BF16)


def _attn_ctx(p, nb, t, row_off, sink):
    assert row_off % t == 0
    ob = row_off // t
    qw = ATT_HEADS * ATT_HEAD_DIM
    kvw = ATT_KV_HEADS * ATT_HEAD_DIM
    return pl.pallas_call(
        _attn_ctx_kernel,
        grid=(nb,),
        in_specs=[
            _p_spec(t, qw, lambda b: (ob + b, COL_B_Q * LANE // qw)),
            _p_spec(t, kvw, lambda b: (ob + b, COL_B_K)),
            _p_spec(t, kvw, lambda b: (ob + b, COL_B_V)),
            pl.BlockSpec(memory_space=pltpu.SMEM),
        ],
        out_specs=[pl.BlockSpec((t, qw), lambda b: (b, 0)),
                   pl.BlockSpec((None, t, kvw), lambda b: (b, 0, 0)),
                   pl.BlockSpec((None, t, kvw), lambda b: (b, 0, 0))],
        out_shape=[jax.ShapeDtypeStruct((nb * t, qw), BF16),
                   jax.ShapeDtypeStruct((nb, t, kvw), F32),
                   jax.ShapeDtypeStruct((nb, t, kvw), F32)],
        compiler_params=_params(("parallel",), 2 * t * (qw * 6 + 4 * kvw * 4)),
        name="attn_ctx",
    )(p, p, p, sink)


def _band_bias():
    Q = QBLOCK
    r = np.arange(Q)[:, None]
    c = np.arange(3 * Q)[None, :]
    band = (c >= r) & (c <= r + 2 * Q)
    out = []
    for kind in range(3):
        ok = band.copy()
        if kind == 0:
            ok &= c >= Q
        if kind == 2:
            ok &= c < 2 * Q
        out.append(np.where(ok, 0.0, NEG).astype(np.float32))
    return jnp.asarray(np.stack(out))


def _half_variants(x):
    low = lax.broadcasted_iota(jnp.int32, x.shape, 1) < ATT_HEAD_DIM
    sw = pltpu.roll(x, ATT_HEAD_DIM, 1)
    zero = jnp.zeros_like(x)
    return [jnp.where(low, a, b).astype(BF16) for a, b in ((x, zero), (zero, sw), (sw, zero), (zero, x))]


def _attn_lat_kernel(t, q_ref, k_ref, v_ref, kc_ref, vc_ref, cq_ref, saq_ref, sbq_ref,
                     ck_ref, sak_ref, sbk_ref, bias_ref, sink_ref, o_ref, kp_scr, vp_scr, kc_scr, vc_scr):
    n = pl.program_id(1)
    Q = QBLOCK
    hd = ATT_HEAD_DIM
    quarter = hd // 4
    nvar = 2 * ATT_KV_HEADS

    @pl.when(n == 0)
    def _():
        zero = jnp.zeros((Q, LANE), BF16)
        for vi in range(nvar):
            kp_scr[vi, 0:Q, :] = zero
            vp_scr[vi, 0:Q, :] = zero
            kp_scr[vi, Q + t:2 * Q + t, :] = zero
            vp_scr[vi, Q + t:2 * Q + t, :] = zero

        def fill(c, _):
            rows = _chunk_rows(c, Q)
            dst = pl.ds(pl.multiple_of(c * Q + Q, Q), Q)
            kk = _rope(k_ref[rows, :], ck_ref[rows, :], sak_ref[rows, :], sbk_ref[rows, :], quarter)
            for vi, (kv, vv) in enumerate(zip(_half_variants(kk), _half_variants(v_ref[rows, :]))):
                kp_scr[vi, dst, :] = kv
                vp_scr[vi, dst, :] = vv
            return 0

        lax.fori_loop(0, t // Q, fill, 0)
        for vi, (kv, vv) in enumerate(zip(_half_variants(kc_ref[...]), _half_variants(vc_ref[...]))):
            kc_scr[vi] = kv
            vc_scr[vi] = vv

    band = pl.ds(pl.multiple_of(n * Q, Q), 3 * Q)
    cq, saq, sbq = cq_ref[...], saq_ref[...], sbq_ref[...]
    bias = bias_ref[...]
    scale = hd ** -0.5
    heads_per_chunk = LANE // hd
    qcs = [(_rope(q_ref[:, c * LANE:(c + 1) * LANE], cq, saq, sbq, quarter) * scale).astype(BF16)
           for c in range(ATT_HEADS // heads_per_chunk)]

    def variant(hx):
        return 2 * (hx // ATT_GROUPS) + hx % heads_per_chunk

    def scores(hx):
        qc, vi = qcs[hx // heads_per_chunk], variant(hx)
        return [_dot_nt(qc, kp_scr[vi, band, :]) + bias, _dot_nt(qc, kc_scr[vi])]

    acc = None
    pending = [scores(hx) for hx in range(ATT_AHEAD)]
    for hx in range(ATT_HEADS):
        s_parts = pending.pop(0)
        if hx + ATT_AHEAD < ATT_HEADS:
            pending.append(scores(hx + ATT_AHEAD))
        vi = variant(hx)
        sink = jnp.full((1, 1), sink_ref[hx], F32)
        o = _softmax_sink_pv(s_parts, [vp_scr[vi, band, :], vc_scr[vi]], sink)
        acc = o if acc is None else acc + o
        if hx % heads_per_chunk == heads_per_chunk - 1:
            c = hx // heads_per_chunk
            o_ref[:, c * LANE:(c + 1) * LANE] = acc.astype(BF16)
            acc = None


def _attn_lat(p, nb, t, kc, vc, layer, tabs, sink):
    assert (ATT_HEAD_DIM ** -0.5) == 2.0 ** round(np.log2(ATT_HEAD_DIM ** -0.5))
    qw = ATT_HEADS * ATT_HEAD_DIM
    kvw = ATT_KV_HEADS * ATT_HEAD_DIM
    assert kvw == LANE
    nq = t // QBLOCK
    past = kc.shape[2]
    nvar = 2 * ATT_KV_HEADS
    cos, sa, sb = tabs
    bias = _band_bias()
    tq = pl.BlockSpec((QBLOCK, LANE), lambda b, n: (n, 0))
    tk = pl.BlockSpec((t, LANE), lambda b, n: (0, 0))
    return pl.pallas_call(
        functools.partial(_attn_lat_kernel, t),
        grid=(nb, nq),
        in_specs=[
            _p_spec(QBLOCK, qw, lambda b, n: (b * nq + n, COL_B_Q * LANE // qw)),
            _p_spec(t, kvw, lambda b, n: (b, COL_B_K)),
            _p_spec(t, kvw, lambda b, n: (b, COL_B_V)),
            pl.BlockSpec((None, None, past, kvw), lambda b, n: (b, layer, 0, 0)),
            pl.BlockSpec((None, None, past, kvw), lambda b, n: (b, layer, 0, 0)),
            tq, tq, tq, tk, tk, tk,
            pl.BlockSpec((None,) + bias.shape[1:],
                         lambda b, n: (jnp.where(n == 0, 0, jnp.where(n == nq - 1, 2, 1)), 0, 0)),
            pl.BlockSpec(memory_space=pltpu.SMEM),
        ],
        out_specs=pl.BlockSpec((QBLOCK, qw), lambda b, n: (b * nq + n, 0)),
        out_shape=jax.ShapeDtypeStruct((nb * t, qw), BF16),
        scratch_shapes=[pltpu.VMEM((nvar, t + 2 * QBLOCK, LANE), BF16), pltpu.VMEM((nvar, t + 2 * QBLOCK, LANE), BF16),
                        pltpu.VMEM((nvar, past, LANE), BF16), pltpu.VMEM((nvar, past, LANE), BF16)],
        compiler_params=_params(("parallel", "arbitrary"),
                                2 * 5 * t * LANE * 4 + 2 * nvar * (t + 2 * QBLOCK + past) * LANE * 2 + 2 * MIB),
        name="attn_lat",
    )(p, p, p, kc, vc, cos, sa, sb, cos, sa, sb, bias, sink)


def _conv_kernel(t, cb_ref, cc_ref, cx_ref, w_ref, o_ref):
    u = cc_ref[...] * cx_ref[...]
    tok = lax.broadcasted_iota(jnp.int32, u.shape, 0)
    prev = jnp.where(tok == 0, 0.0, pltpu.roll(u, 1, 0))
    nxt = jnp.where(tok == t - 1, 0.0, pltpu.roll(u, t - 1, 0))
    o_ref[...] = (cb_ref[...] * (w_ref[0:1, :] * prev + w_ref[1:2, :] * u + w_ref[2:3, :] * nxt)).astype(BF16)


def _conv(p, nb, t, row_off, w):
    assert row_off % t == 0
    ob = row_off // t
    nblk = w.shape[1] // LANE

    def col(cb):
        return _p_spec(t, LANE, lambda b, j: (ob + b, cb + j))

    return pl.pallas_call(
        functools.partial(_conv_kernel, t),
        grid=(nb, nblk),
        in_specs=[col(COL_C_B), col(COL_C_C), col(COL_C_X),
                  pl.BlockSpec((w.shape[0], LANE), lambda b, j: (0, j))],
        out_specs=pl.BlockSpec((t, LANE), lambda b, j: (b, j)),
        out_shape=jax.ShapeDtypeStruct((nb * t, w.shape[1]), BF16),
        compiler_params=_params(("parallel", "parallel"), 2 * 4 * t * LANE * 4),
        name="conv",
    )(p, p, p, w)


GATE_SRC = 4 * MIX_HEADS * MIX_DIM
N_GATES = 4 * MIX_HEADS


def _gate_lanes(x):
    g = x.reshape(x.shape[:-1] + (2, 2, MIX_HEADS))
    pad = [(0, 0)] * (x.ndim - 1) + [(0, LANE - 2 * MIX_HEADS)]
    blocks = [jnp.pad(g[..., kind, :].reshape(x.shape[:-1] + (2 * MIX_HEADS,)), pad) for kind in range(2)]
    return jnp.concatenate(blocks, axis=-1)


def _column_slabs(w, tile):
    k, n = w.shape[-2:]
    return jnp.swapaxes(w.reshape(w.shape[:-1] + (n // tile, tile)), -3, -2)


def _stage_mixer_proj(w_in):
    head = w_in[..., 0:GATE_SRC]
    tail = w_in[..., GATE_SRC + N_GATES:]
    gates = _gate_lanes(w_in[..., GATE_SRC:GATE_SRC + N_GATES])
    assert head.shape[-1] + tail.shape[-1] == COL_GATES * LANE
    assert head.shape[-1] + tail.shape[-1] + gates.shape[-1] == PROJ_P
    return jnp.concatenate([head, tail, gates], axis=-1).astype(BF16)


def kernel(x_prompt, x_sample, cache_attn_k, cache_attn_v, state_mlstm_C, state_mlstm_n, state_mlstm_m,
           state_ret_S, c, c_ctx, w_mod, b_mod, norm_g, ffn1_w_in, ffn1_w_out, ffn2_w_in, ffn2_w_out,
           w_in, mlstm_gate_b, mlstm_norm_g, attn_sink, conv_w, ret_log_decay, ret_norm_g,
           w_branch, w_gate, w_out):
    bp, tp, d = x_prompt.shape
    bs, ts, _ = x_sample.shape
    depth = w_mod.shape[0]
    past = cache_attn_k.shape[2]
    kvw = ATT_KV_HEADS * ATT_HEAD_DIM
    fits = [t for t in (1024, 512, 256) if ts % t == 0 and (bp * tp) % t == 0]
    rows = _Rows(bs, ts, bp, tp, fits[0])
    rows_ffn = _Rows(bs, ts, bp, tp, fits[min(1, len(fits) - 1)])

    nbp = -(-(1 + bs) // 8) * 8
    cpad = jnp.concatenate([c_ctx[None], c, jnp.zeros((nbp - 1 - bs, d), F32)], axis=0)
    mod = _modulation(cpad, w_mod, b_mod).reshape(depth, nbp, 3, 3, d)

    w_p = _stage_mixer_proj(w_in)
    ffn1_in, ffn1_out = ffn1_w_in.astype(BF16), ffn1_w_out.astype(BF16)
    ffn2_in, ffn2_out = ffn2_w_in.astype(BF16), ffn2_w_out.astype(BF16)
    tn = _pick_tile(d, 256)
    wg_b = jnp.swapaxes(_column_slabs(w_gate.astype(BF16), tn), 1, 2)
    wb_b = jnp.swapaxes(_column_slabs(w_branch.astype(BF16), tn), 1, 2)
    wo_b = w_out.astype(BF16)

    gate_bias = _gate_lanes(mlstm_gate_b.reshape(depth, 1, N_GATES))
    tabs_att = _rope_tables(ts, ATT_HEAD_DIM)
    tabs_ret = _rope_tables(ts, MIX_DIM)

    x = jnp.concatenate([x_sample.reshape(bs * ts, d), x_prompt.reshape(bp * tp, d)], axis=0)
    ms = bs * ts
    kc_all = cache_attn_k.reshape(bs, depth, past, kvw)
    vc_all = cache_attn_v.reshape(bs, depth, past, kvw)
    ks_, vs_, cs_, ns_, mms_, ss_ = [], [], [], [], [], []
    for l in range(depth):
        g = norm_g[l]
        x = _ffn(rows_ffn, x, mod[l, :, 0], g[0:2], ffn1_in, ffn1_out, l)

        p, h = _inproj(rows, x, mod[l, :, 1], g[2:4], w_p, l)
        a_gain = mlstm_norm_g[l][None]
        d_gain = ret_norm_g[l][None]
        lat_a = _mlstm(p, bs, ts, 0, gate_bias[l], a_gain, (state_mlstm_C, state_mlstm_n, state_mlstm_m, l))[0]
        lat_b = _attn_lat(p, bs, ts, kc_all, vc_all, l, tabs_att, attn_sink[l])
        lat_c = _conv(p, bs, ts, 0, conv_w[l])
        lat_d = _retention(p, bs, ts, 0, ret_log_decay[l], d_gain, (tabs_ret, state_ret_S, l))[0]
        ctx_a, c_fin, n_fin, m_fin = _mlstm(p, bp, tp, ms, gate_bias[l], a_gain, None)
        ctx_b, k_new, v_new = _attn_ctx(p, bp, tp, ms, attn_sink[l])
        ctx_c = _conv(p, bp, tp, ms, conv_w[l])
        ctx_d, s_fin = _retention(p, bp, tp, ms, ret_log_decay[l], d_gain, None)
        y = _merge(rows, h, (lat_a, lat_b, lat_c, lat_d), (ctx_a, ctx_b, ctx_c, ctx_d), wg_b, wb_b, l)
        x = _outproj(rows_ffn, x, mod[l, :, 1], g[2:4], y, wo_b, l)

        if l + 1 < depth:
            x = _ffn(rows_ffn, x, mod[l, :, 2], g[4:6], ffn2_in, ffn2_out, l)
        else:
            lat_tiles = rows_ffn.lat_tiles
            y_lat = _ffn(rows_ffn, x, mod[l, :, 2], g[4:6], ffn2_in, ffn2_out, l, (0, lat_tiles))
            y_ctx = _ffn(rows_ffn, x, mod[l, :, 2], g[4:6], ffn2_in, ffn2_out, l,
                         (lat_tiles, rows_ffn.m // rows_ffn.tm - lat_tiles))

        ks_.append(k_new.reshape(bp, tp, ATT_KV_HEADS, ATT_HEAD_DIM))
        vs_.append(v_new.reshape(bp, tp, ATT_KV_HEADS, ATT_HEAD_DIM))
        cs_.append(c_fin)
        ns_.append(n_fin)
        mms_.append(m_fin[:, :, 0].reshape(bp, 2, MIX_HEADS))
        ss_.append(s_fin)

    y_sample = y_lat.reshape(bs, ts, d)
    y_prompt = y_ctx.reshape(bp, tp, d)
    return (y_prompt, y_sample, jnp.stack(ks_, axis=1), jnp.stack(vs_, axis=1), jnp.stack(cs_, axis=1),
            jnp.stack(ns_, axis=1), jnp.stack(mms_, axis=1), jnp.stack(ss_, axis=1))
```
